```python
import math
import jax
import jax.numpy as jnp
from jax import lax
import numpy as np

D_MODEL = 2048
BATCH = 4
SEQ = 2048
DEPTH = 2
DEC_BATCH = 128
DEC_SEQ = 4
PAST_LEN = 2048
PAGE_SIZE = 128

N_META = 16
ATT_WIDTH_A = D_MODEL // 2
ATT_WIDTH_B = D_MODEL // 2
DV_A = 128
HA = ATT_WIDTH_A // DV_A
DQK_A = DV_A // 2
DH_B = 128
HB = ATT_WIDTH_B // DH_B
ROT_DIM = DQK_A // 4
ROPE_THETA = 500000.0
QBLK = 128
POOL_WINDOWS = (2, 4, 8, 16)
N_POOL_GROUPS = len(POOL_WINDOWS)
POOL_CH = D_MODEL // N_POOL_GROUPS
POOL_BUF = max(POOL_WINDOWS) - 1
N_GROUPS = 4
EXPERTS_PER_GROUP = 8
N_EXPERTS = N_GROUPS * EXPERTS_PER_GROUP
TOP_K = 2
D_EXPERT = D_MODEL // 4
EPS = 1e-6
NEG_INF = -1e30
N_ATTN_LAYERS = (DEPTH + 1) // 2
N_POOL_LAYERS = DEPTH // 2
PROJ_SIZES = (HA * 2 * DQK_A, HA * 2 * DQK_A, HA * DV_A, HB * DH_B, HB * DH_B, HB * DH_B, HB)
PROJ_SPLITS = tuple(int(v) for v in np.cumsum(PROJ_SIZES)[:-1])
D_IN = sum(PROJ_SIZES)
D_ATT_OUT = HA * DV_A + HB * DH_B

kernel_name = "hybrid_diff_fox_pool_hmoe_step"


def _rmsnorm(x, g):
    xf = x.astype(jnp.float32)
    y = xf * lax.rsqrt(jnp.mean(xf * xf, axis=-1, keepdims=True) + EPS)
    return (y * g.astype(jnp.float32)).astype(x.dtype)


def _rope(x, pos):
    half = ROT_DIM // 2
    inv = ROPE_THETA ** (-jnp.arange(0, ROT_DIM, 2, dtype=jnp.float32) / ROT_DIM)
    ang = pos.astype(jnp.float32)[:, None] * inv[None, :]
    cos = jnp.cos(ang)[:, None, None, :]
    sin = jnp.sin(ang)[:, None, None, :]
    xf = x.astype(jnp.float32)
    x1 = xf[..., :half]
    x2 = xf[..., half:ROT_DIM]
    out = jnp.concatenate([x1 * cos - x2 * sin, x2 * cos + x1 * sin, xf[..., ROT_DIM:]], axis=-1)
    return out.astype(x.dtype)


def _attn_project(h, pos, w_in, b_f):
    B, T, _ = h.shape
    qa, ka, va, qb, kb, vb, fl = jnp.split(h @ w_in, PROJ_SPLITS, axis=-1)
    qa = _rope(qa.reshape(B, T, HA, 2, DQK_A), pos)
    ka = _rope(ka.reshape(B, T, HA, 2, DQK_A), pos)
    va = va.reshape(B, T, HA, DV_A)
    qb = qb.reshape(B, T, HB, DH_B)
    kb = kb.reshape(B, T, HB, DH_B)
    vb = vb.reshape(B, T, HB, DH_B)
    logf = jax.nn.log_sigmoid(fl.astype(jnp.float32) + b_f.astype(jnp.float32))
    return qa, ka, va, qb, kb, vb, logf


def _diff_attention(q, q_pos, segs, lam):
    scores = []
    for k, _, k_pos in segs:
        s = jnp.einsum('bqhmd,bkhmd->bmhqk', q, k, preferred_element_type=jnp.float32) * (DQK_A ** -0.5)
        scores.append(jnp.where(k_pos[None, :] <= q_pos[:, None], s, NEG_INF))
    p = jax.nn.softmax(jnp.concatenate(scores, axis=-1), axis=-1)
    w = p[:, 0] - lam * p[:, 1]
    out = 0.0
    off = 0
    for k, v, _ in segs:
        n = k.shape[1]
        out = out + jnp.einsum('bhqk,bkhe->bqhe', w[..., off:off + n].astype(v.dtype), v)
        off += n
    return out


def _forgetting_attention(q, fq, q_pos, segs):
    fq_t = jnp.swapaxes(fq, 1, 2)[..., :, None]
    scores = []
    for k, _, fk, k_pos in segs:
        s = jnp.einsum('bqhd,bkhd->bhqk', q, k, preferred_element_type=jnp.float32) * (DH_B ** -0.5)
        s = s + fq_t - jnp.swapaxes(fk, 1, 2)[..., None, :]
        scores.append(jnp.where(k_pos[None, :] <= q_pos[:, None], s, NEG_INF))
    p = jax.nn.softmax(jnp.concatenate(scores, axis=-1), axis=-1)
    out = 0.0
    off = 0
    for k, v, _, _ in segs:
        n = k.shape[1]
        out = out + jnp.einsum('bhqk,bkhe->bqhe', p[..., off:off + n].astype(v.dtype), v)
        off += n
    return out


def _attn_merge(oa, ob, g_sub, lam_init, w_out):
    B, T = oa.shape[:2]
    oa = _rmsnorm(oa, g_sub) * (1.0 - lam_init)
    o = jnp.concatenate([oa.reshape(B, T, -1), ob.reshape(B, T, -1).astype(oa.dtype)], axis=-1)
    return o @ w_out


def _to_blocks(a, nb):
    pad = nb * QBLK - a.shape[1]
    a = jnp.pad(a, [(0, 0), (0, pad)] + [(0, 0)] * (a.ndim - 2))
    a = a.reshape(a.shape[0], nb, QBLK, *a.shape[2:])
    return jnp.moveaxis(a, 1, 0)


def _from_blocks(a, T):
    a = jnp.moveaxis(a, 0, 1)
    return a.reshape(a.shape[0], -1, *a.shape[3:])[:, :T]


def _attn_prompt(h, pos, w_in, b_f, lam, g_sub, lam_init, w_out):
    B, T, _ = h.shape
    qa, ka, va, qb, kb, vb, logf = _attn_project(h, pos, w_in, b_f)
    fcum = jnp.cumsum(logf, axis=1)
    nb = -(-T // QBLK)
    segs_a = ((ka, va, pos),)
    segs_b = ((kb, vb, fcum, pos),)

    def block(args):
        qa_b, qb_b, f_b, pos_b = args
        return (_diff_attention(qa_b, pos_b, segs_a, lam),
                _forgetting_attention(qb_b, f_b, pos_b, segs_b))

    qpos = jnp.arange(nb * QBLK, dtype=jnp.int32).reshape(nb, QBLK)
    oa, ob = lax.map(block, (_to_blocks(qa, nb), _to_blocks(qb, nb), _to_blocks(fcum, nb), qpos))
    y = _attn_merge(_from_blocks(oa, T), _from_blocks(ob, T), g_sub, lam_init, w_out)
    return y, (ka.reshape(B, T, HA, 2 * DQK_A), va, kb, vb, logf)


def _attn_sample(h, pos, pos_past, ka_p, va_p, kb_p, vb_p, lf_p, w_in, b_f, lam, g_sub, lam_init, w_out):
    B, T, _ = h.shape
    qa, ka, va, qb, kb, vb, logf = _attn_project(h, pos, w_in, b_f)
    f_past = jnp.cumsum(lf_p.astype(jnp.float32), axis=1)
    f_new = f_past[:, -1:] + jnp.cumsum(logf, axis=1)
    oa = _diff_attention(qa, pos, ((ka_p, va_p, pos_past), (ka, va, pos)), lam)
    ob = _forgetting_attention(qb, f_new, pos, ((kb_p, vb_p, f_past, pos_past), (kb, vb, f_new, pos)))
    y = _attn_merge(oa, ob, g_sub, lam_init, w_out)
    return y, (ka.reshape(B, T, HA, 2 * DQK_A), va, kb, vb, logf)


def _pool_mixer(u, prev, pos0, w_pool, scale):
    B, T, D = u.shape
    ext = jnp.concatenate([prev.astype(u.dtype), u], axis=1)
    csum = jnp.pad(jnp.cumsum(ext.astype(jnp.float32), axis=1), ((0, 0), (1, 0), (0, 0)))
    pos = pos0 + jnp.arange(T, dtype=jnp.int32)
    uf = u.astype(jnp.float32)
    diffs = []
    for g, w in enumerate(POOL_WINDOWS):
        c = slice(g * POOL_CH, (g + 1) * POOL_CH)
        win_sum = csum[:, POOL_BUF + 1:POOL_BUF + 1 + T, c] - csum[:, POOL_BUF + 1 - w:POOL_BUF + 1 - w + T, c]
        cnt = jnp.minimum(w, pos + 1).astype(jnp.float32)[None, :, None]
        diffs.append(win_sum / cnt - uf[:, :, c])
    d = jnp.stack(diffs, axis=2).astype(u.dtype)
    y = jnp.einsum('btgc,gce->btge', d, w_pool).reshape(B, T, D) * scale
    return y.astype(u.dtype), ext[:, -POOL_BUF:]


def _hier_moe(h, w_rg, b_rg, w_re, b_re, w_gate, w_up, w_down):
    B, T, D = h.shape
    t = h.reshape(B * T, D)
    lg = (t @ w_rg).astype(jnp.float32) + b_rg.astype(jnp.float32)
    grp = jnp.argmax(lg, axis=-1)
    g_oh = jax.nn.one_hot(grp, N_GROUPS, dtype=jnp.float32)
    p_grp = jnp.sum(jax.nn.softmax(lg, axis=-1) * g_oh, axis=-1, keepdims=True)
    le = jnp.einsum('nd,dge->nge', t, w_re).astype(jnp.float32) + b_re.astype(jnp.float32)
    le = jnp.einsum('nge,ng->ne', le, g_oh)
    top_v, top_i = lax.top_k(le, TOP_K)
    wts = jax.nn.softmax(top_v, axis=-1) * p_grp
    eidx = grp[:, None] * EXPERTS_PER_GROUP + top_i
    gates = jnp.einsum('nke,nk->ne', jax.nn.one_hot(eidx, N_EXPERTS, dtype=jnp.float32), wts)
    hg = jnp.einsum('nd,edf->nef', t, w_gate)
    hu = jnp.einsum('nd,edf->nef', t, w_up)
    act = jax.nn.silu(hg) * hu * gates[:, :, None].astype(t.dtype)
    return jnp.einsum('nef,efd->nd', act, w_down).reshape(B, T, D)


def setup_inputs(seed: int = 0) -> dict:
    key = jax.random.key(seed)
    ks = jax.random.split(key, 32)
    f32 = jnp.float32

    def nrm(k, shape, s=1.0):
        return s * jax.random.normal(k, shape, f32)

    n_pages = PAST_LEN // PAGE_SIZE
    n_used = DEC_BATCH * n_pages
    n_phys = n_used + max(1, n_used // 4)
    perm = jax.random.permutation(ks[8], n_phys)
    page_table = perm[:n_used].reshape(DEC_BATCH, n_pages).astype(jnp.int32)
    return {
        "x_prompt": nrm(ks[0], (BATCH, SEQ, D_MODEL)),
        "x_sample": nrm(ks[1], (DEC_BATCH, DEC_SEQ, D_MODEL)),
        "cache_k_a": nrm(ks[2], (N_ATTN_LAYERS, n_phys, PAGE_SIZE, HA, 2 * DQK_A)),
        "cache_v_a": nrm(ks[3], (N_ATTN_LAYERS, n_phys, PAGE_SIZE, HA, DV_A)),
        "cache_k_b": nrm(ks[4], (N_ATTN_LAYERS, n_phys, PAGE_SIZE, HB, DH_B)),
        "cache_v_b": nrm(ks[5], (N_ATTN_LAYERS, n_phys, PAGE_SIZE, HB, DH_B)),
        "cache_logf_b": jax.nn.log_sigmoid(nrm(ks[6], (N_ATTN_LAYERS, n_phys, PAGE_SIZE, HB)) + 3.0),
        "state_pool": nrm(ks[7], (N_POOL_LAYERS, DEC_BATCH, POOL_BUF, D_MODEL)),
        "page_table": page_table,
        "meta_tokens": nrm(ks[9], (N_META, D_MODEL)),
        "g_mix": 1.0 + nrm(ks[10], (DEPTH, D_MODEL), 0.02),
        "w_in": nrm(ks[11], (N_ATTN_LAYERS, D_MODEL, D_IN), D_MODEL ** -0.5),
        "b_forget": jnp.linspace(1.0, 5.0, HB, dtype=f32)[None, :] + nrm(ks[12], (N_ATTN_LAYERS, HB), 0.01),
        "lambda_qk": nrm(ks[13], (N_ATTN_LAYERS, 4, DQK_A), 0.1),
        "g_subln": 1.0 + nrm(ks[14], (N_ATTN_LAYERS, DV_A), 0.02),
        "w_out": nrm(ks[15], (N_ATTN_LAYERS, D_ATT_OUT, D_MODEL), D_ATT_OUT ** -0.5),
        "w_pool": nrm(ks[16], (N_POOL_LAYERS, N_POOL_GROUPS, POOL_CH, POOL_CH), POOL_CH ** -0.5),
        "pool_scale": 1.0 + nrm(ks[17], (N_POOL_LAYERS, D_MODEL), 0.02),
        "g_ffn": 1.0 + nrm(ks[18], (DEPTH, D_MODEL), 0.02),
        "w_route_group": nrm(ks[19], (DEPTH, D_MODEL, N_GROUPS), D_MODEL ** -0.5),
        "b_route_group": nrm(ks[20], (DEPTH, N_GROUPS), 0.01),
        "w_route_expert": nrm(ks[21], (DEPTH, D_MODEL, N_GROUPS, EXPERTS_PER_GROUP), D_MODEL ** -0.5),
        "b_route_expert": nrm(ks[22], (DEPTH, N_GROUPS, EXPERTS_PER_GROUP), 0.01),
        "w_gate": nrm(ks[23], (DEPTH, N_EXPERTS, D_MODEL, D_EXPERT), D_MODEL ** -0.5),
        "w_up": nrm(ks[24], (DEPTH, N_EXPERTS, D_MODEL, D_EXPERT), D_MODEL ** -0.5),
        "w_down": nrm(ks[25], (DEPTH, N_EXPERTS, D_EXPERT, D_MODEL), D_EXPERT ** -0.5),
        "g_final": 1.0 + nrm(ks[26], (D_MODEL,), 0.02),
    }


def reference(x_prompt, x_sample, cache_k_a, cache_v_a, cache_k_b, cache_v_b, cache_logf_b, state_pool,
              page_table, meta_tokens, g_mix, w_in, b_forget, lambda_qk, g_subln, w_out, w_pool, pool_scale,
              g_ffn, w_route_group, b_route_group, w_route_expert, b_route_expert, w_gate, w_up, w_down, g_final):
    B = x_prompt.shape[0]
    xp = jnp.concatenate([jnp.broadcast_to(meta_tokens[None].astype(x_prompt.dtype), (B, N_META, D_MODEL)),
                          x_prompt], axis=1)
    xs = x_sample
    Tp = xp.shape[1]
    DB, Ts = xs.shape[0], xs.shape[1]
    P = page_table.shape[1] * PAGE_SIZE
    pos_p = jnp.arange(Tp, dtype=jnp.int32)
    pos_past = jnp.arange(P, dtype=jnp.int32)
    pos_s = P + jnp.arange(Ts, dtype=jnp.int32)
    pb = jnp.zeros((B, POOL_BUF, D_MODEL), xp.dtype)

    ka_p, va_p, kb_p, vb_p, lf_p, pool_p = [], [], [], [], [], []
    ka_s, va_s, kb_s, vb_s, lf_s, pool_s = [], [], [], [], [], []
    for l in range(DEPTH):
        if l % 2 == 0:
            i = l // 2
            lam_init = 0.8 - 0.6 * math.exp(-0.3 * l)
            lq = lambda_qk[i].astype(jnp.float32)
            lam = jnp.exp(jnp.sum(lq[0] * lq[1])) - jnp.exp(jnp.sum(lq[2] * lq[3])) + lam_init
            yp, rp = _attn_prompt(_rmsnorm(xp, g_mix[l]), pos_p, w_in[i], b_forget[i], lam,
                                  g_subln[i], lam_init, w_out[i])
            ka_past = cache_k_a[i, page_table].reshape(DB, P, HA, 2, DQK_A)
            va_past = cache_v_a[i, page_table].reshape(DB, P, HA, DV_A)
            kb_past = cache_k_b[i, page_table].reshape(DB, P, HB, DH_B)
            vb_past = cache_v_b[i, page_table].reshape(DB, P, HB, DH_B)
            lf_past = cache_logf_b[i, page_table].reshape(DB, P, HB)
            ys, rs = _attn_sample(_rmsnorm(xs, g_mix[l]), pos_s, pos_past, ka_past, va_past, kb_past, vb_past,
                                  lf_past, w_in[i], b_forget[i], lam, g_subln[i], lam_init, w_out[i])
            ka_p.append(rp[0]); va_p.append(rp[1]); kb_p.append(rp[2]); vb_p.append(rp[3]); lf_p.append(rp[4])
            ka_s.append(rs[0]); va_s.append(rs[1]); kb_s.append(rs[2]); vb_s.append(rs[3]); lf_s.append(rs[4])
        else:
            j = l // 2
            yp, sp = _pool_mixer(_rmsnorm(xp, g_mix[l]), pb, 0, w_pool[j], pool_scale[j])
            ys, ss = _pool_mixer(_rmsnorm(xs, g_mix[l]), state_pool[j], P, w_pool[j], pool_scale[j])
            pool_p.append(sp)
            pool_s.append(ss)
        xp = xp + yp
        xs = xs + ys
        xp = xp + _hier_moe(_rmsnorm(xp, g_ffn[l]), w_route_group[l], b_route_group[l], w_route_expert[l],
                            b_route_expert[l], w_gate[l], w_up[l], w_down[l])
        xs = xs + _hier_moe(_rmsnorm(xs, g_ffn[l]), w_route_group[l], b_route_group[l], w_route_expert[l],
                            b_route_expert[l], w_gate[l], w_up[l], w_down[l])

    y_prompt = _rmsnorm(xp, g_final)[:, N_META:]
    y_sample = _rmsnorm(xs, g_final)
    new_k_a_p = jnp.stack(ka_p)
    new_v_a_p = jnp.stack(va_p)
    new_k_b_p = jnp.stack(kb_p)
    new_v_b_p = jnp.stack(vb_p)
    new_logf_b_p = jnp.stack(lf_p)
    new_pool_p = jnp.stack(pool_p)
    new_k_a_s = jnp.stack(ka_s)
    new_v_a_s = jnp.stack(va_s)
    new_k_b_s = jnp.stack(kb_s)
    new_v_b_s = jnp.stack(vb_s)
    new_logf_b_s = jnp.stack(lf_s)
    new_pool_s = jnp.stack(pool_s)
    return (y_prompt, y_sample, new_k_a_p, new_v_a_p, new_k_b_p, new_v_b_p, new_logf_b_p, new_pool_p,
            new_k_a_s, new_v_a_s, new_k_b_s, new_v_b_s, new_logf_b_s, new_pool_s)
```

```python
import functools
import math

import jax
import jax.numpy as jnp
import numpy as np
from jax import lax
from jax.experimental import pallas as pl
from jax.experimental.pallas import tpu as pltpu

F32 = jnp.float32
BF16 = jnp.bfloat16

D_MODEL = 2048
N_META = 16
HA = 8
HB = 8
DH = 128
DQK_A = 64
ROT_DIM = 16
ROPE_THETA = 500000.0
POOL_WINDOWS = (2, 4, 8, 16)
POOL_CH = D_MODEL // len(POOL_WINDOWS)
POOL_BUF = max(POOL_WINDOWS) - 1
N_GROUPS = 4
EXPERTS_PER_GROUP = 8
N_EXPERTS = N_GROUPS * EXPERTS_PER_GROUP
D_EXPERT = D_MODEL // 4
EPS = 1e-6
NEG_INF = -1e30
PAGE_SIZE = 128
D_QKV = 6 * HA * DH

LANES = 128
VMEM_LIMIT = 56 * 1024 * 1024

TQ = 256
TM_PROJ = 512
TN_PROJ = 512
TM_ROUTE = 256
TM_EXPERT = 256
TM_COMBINE = 256
TP_POOL = 256
GATHER_CHUNK = 128


def _rms(x, g):
    return x * lax.rsqrt(jnp.mean(x * x, axis=-1, keepdims=True) + EPS) * g


def _dot(a, b):
    return jnp.dot(a, b, preferred_element_type=F32)


def _dot_nt(a, b):
    return lax.dot_general(a, b, (((1,), (1,)), ((), ())), preferred_element_type=F32)


def _inproj_body(x_ref, g_ref, w_ref, wf_ref, bf_ref, rope_ref, p_ref, lf_ref, h_scr, *, n_rope_tiles, tn):
    j = pl.program_id(1)

    @pl.when(j == 0)
    def _():
        hb = _rms(x_ref[...], g_ref[...]).astype(BF16)
        h_scr[...] = hb
        fl = _dot(hb, wf_ref[...]) + bf_ref[...]
        lf_ref[...] = jnp.minimum(fl, 0.0) - jnp.log1p(jnp.exp(-jnp.abs(fl)))

    acc = _dot(h_scr[...], w_ref[...])

    @pl.when(j < n_rope_tiles)
    def _():
        c = rope_ref[0]
        s_lo = rope_ref[1]
        s_hi = rope_ref[2]
        for cc in range(tn // LANES):
            a = acc[:, cc * LANES:(cc + 1) * LANES]
            p_ref[:, cc * LANES:(cc + 1) * LANES] = (
                a * c + pltpu.roll(a, ROT_DIM // 2, 1) * s_lo + pltpu.roll(a, LANES - ROT_DIM // 2, 1) * s_hi)

    @pl.when(j >= n_rope_tiles)
    def _():
        p_ref[...] = acc


def _inproj(x, g, w_bf, wf_bf, bf, rope):
    nt = x.shape[0]
    tm, tn = TM_PROJ, TN_PROJ
    n_rope_tiles = (2 * HA * DH) // tn
    return pl.pallas_call(
        functools.partial(_inproj_body, n_rope_tiles=n_rope_tiles, tn=tn),
        grid=(nt // tm, D_QKV // tn),
        in_specs=[
            pl.BlockSpec((tm, D_MODEL), lambda i, j: (i, 0)),
            pl.BlockSpec((1, D_MODEL), lambda i, j: (0, 0)),
            pl.BlockSpec((D_MODEL, tn), lambda i, j: (0, j)),
            pl.BlockSpec((D_MODEL, LANES), lambda i, j: (0, 0)),
            pl.BlockSpec((1, LANES), lambda i, j: (0, 0)),
            pl.BlockSpec((3, tm, LANES), lambda i, j: (0, i, 0)),
        ],
        out_specs=[
            pl.BlockSpec((tm, tn), lambda i, j: (i, j)),
            pl.BlockSpec((tm, LANES), lambda i, j: (i, 0)),
        ],
        out_shape=[jax.ShapeDtypeStruct((nt, D_QKV), F32), jax.ShapeDtypeStruct((nt, LANES), F32)],
        scratch_shapes=[pltpu.VMEM((tm, D_MODEL), BF16)],
        compiler_params=pltpu.CompilerParams(
            dimension_semantics=("arbitrary", "arbitrary"), vmem_limit_bytes=VMEM_LIMIT),
        name="inproj",
    )(x, g, w_bf, wf_bf, bf, rope)


def _fcum_body(lf_ref, fc_ref, fct_ref, *, lp):
    blk = TQ
    row = lax.broadcasted_iota(jnp.int32, (blk, blk), 0)
    col = lax.broadcasted_iota(jnp.int32, (blk, blk), 1)
    tri = jnp.where(row >= col, 1.0, 0.0).astype(BF16)
    carry = jnp.zeros((1, LANES), F32)
    for b in range(lp // blk):
        x = lf_ref[pl.ds(b * blk, blk), :]
        hi = x.astype(BF16)
        r1 = x - hi.astype(F32)
        mid = r1.astype(BF16)
        lo = (r1 - mid.astype(F32)).astype(BF16)
        c = _dot(tri, hi) + _dot(tri, mid) + _dot(tri, lo) + carry
        fc_ref[pl.ds(b * blk, blk), :] = c
        carry = c[blk - 1:blk, :]
    fct_ref[0] = fc_ref[...].T


def _fcum(lf, nb, lp):
    return pl.pallas_call(
        functools.partial(_fcum_body, lp=lp),
        grid=(nb,),
        in_specs=[pl.BlockSpec((lp, LANES), lambda b: (b, 0))],
        out_specs=[pl.BlockSpec((lp, LANES), lambda b: (b, 0)),
                   pl.BlockSpec((1, LANES, lp), lambda b: (b, 0, 0))],
        out_shape=[jax.ShapeDtypeStruct((nb * lp, LANES), F32), jax.ShapeDtypeStruct((nb, LANES, lp), F32)],
        name="fcum",
    )(lf)


def _online_update(s, v_bf, m_ref, l_ref, acc_ref):
    m_prev = m_ref[...]
    m_new = jnp.maximum(m_prev, jnp.max(s, axis=1, keepdims=True))
    alpha = jnp.exp(m_prev - m_new)
    p = jnp.exp(s - m_new)
    l_ref[...] = alpha * l_ref[...] + jnp.sum(p, axis=1, keepdims=True)
    acc_ref[...] = alpha * acc_ref[...] + _dot(p.astype(BF16), v_bf)
    m_ref[...] = m_new


def _split_maps(q):
    lane = lax.broadcasted_iota(jnp.int32, q.shape, 1)
    return jnp.concatenate([jnp.where(lane < DQK_A, q, 0.0), jnp.where(lane >= DQK_A, q, 0.0)], axis=0)


def _subln(o, g, lam_init):
    return _rms(o, g) * (1.0 - lam_init)


def _flash_a_body(lam_ref, q_ref, k_ref, v_ref, g_ref, o_ref, m_scr, l_scr, acc_scr, *, lam_init):
    qi = pl.program_id(2)
    tq = TQ
    qs = _split_maps(q_ref[...] * (DQK_A ** -0.5)).astype(BF16)
    m_scr[...] = jnp.full(m_scr.shape, NEG_INF, F32)
    l_scr[...] = jnp.zeros(l_scr.shape, F32)
    acc_scr[...] = jnp.zeros(acc_scr.shape, F32)

    def step(kb, masked):
        start = pl.multiple_of(kb * tq, tq)
        k = k_ref[pl.ds(start, tq), :].astype(BF16)
        v = v_ref[pl.ds(start, tq), :].astype(BF16)
        s = _dot_nt(qs, k)
        if masked:
            row = lax.broadcasted_iota(jnp.int32, s.shape, 0) % tq
            col = lax.broadcasted_iota(jnp.int32, s.shape, 1)
            s = jnp.where(col <= row, s, NEG_INF)
        _online_update(s, v, m_scr, l_scr, acc_scr)

    def body(kb, c):
        step(kb, False)
        return c

    lax.fori_loop(0, qi, body, 0)
    step(qi, True)
    o = acc_scr[...] / l_scr[...]
    o = o[:tq] - lam_ref[0] * o[tq:]
    o_ref[...] = _subln(o, g_ref[...], lam_init)


def _flash_b_body(q_ref, k_ref, v_ref, fq_ref, fk_ref, o_ref, m_scr, l_scr, acc_scr):
    h = pl.program_id(1)
    qi = pl.program_id(2)
    tq = TQ
    qs = (q_ref[...] * (DH ** -0.5)).astype(BF16)
    lane = lax.broadcasted_iota(jnp.int32, (tq, LANES), 1)
    fq_col = jnp.sum(jnp.where(lane == h, fq_ref[...], 0.0), axis=1, keepdims=True)
    m_scr[...] = jnp.full(m_scr.shape, NEG_INF, F32)
    l_scr[...] = jnp.zeros(l_scr.shape, F32)
    acc_scr[...] = jnp.zeros(acc_scr.shape, F32)

    def step(kb, masked):
        start = pl.multiple_of(kb * tq, tq)
        k = k_ref[pl.ds(start, tq), :].astype(BF16)
        v = v_ref[pl.ds(start, tq), :].astype(BF16)
        s = _dot_nt(qs, k) + (fq_col - fk_ref[0, 0, pl.ds(kb, 1), :])
        if masked:
            row = lax.broadcasted_iota(jnp.int32, s.shape, 0)
            col = lax.broadcasted_iota(jnp.int32, s.shape, 1)
            s = jnp.where(col <= row, s, NEG_INF)
        _online_update(s, v, m_scr, l_scr, acc_scr)

    def body(kb, c):
        step(kb, False)
        return c

    lax.fori_loop(0, qi, body, 0)
    step(qi, True)
    o_ref[...] = acc_scr[...] / l_scr[...]


def _flash_a(p, lam, g_sub, nb, lp, lam_init):
    nq = lp // TQ
    qcol, kcol, vcol = 0, HA, 2 * HA
    return pl.pallas_call(
        functools.partial(_flash_a_body, lam_init=lam_init),
        grid=(nb, HA, nq),
        in_specs=[
            pl.BlockSpec(memory_space=pltpu.SMEM),
            pl.BlockSpec((TQ, DH), lambda b, h, q: (b * nq + q, qcol + h)),
            pl.BlockSpec((lp, DH), lambda b, h, q: (b, kcol + h)),
            pl.BlockSpec((lp, DH), lambda b, h, q: (b, vcol + h)),
            pl.BlockSpec((1, DH), lambda b, h, q: (0, 0)),
        ],
        out_specs=pl.BlockSpec((TQ, DH), lambda b, h, q: (b * nq + q, h)),
        out_shape=jax.ShapeDtypeStruct((nb * lp, HA * DH), F32),
        scratch_shapes=[pltpu.VMEM((2 * TQ, 1), F32), pltpu.VMEM((2 * TQ, 1), F32), pltpu.VMEM((2 * TQ, DH), F32)],
        compiler_params=pltpu.CompilerParams(
            dimension_semantics=("arbitrary", "arbitrary", "arbitrary"), vmem_limit_bytes=VMEM_LIMIT),
        name="flash_a",
    )(lam, p, p, p, g_sub)


def _flash_b(p, fc, fk, nb, lp):
    nq = lp // TQ
    qcol, kcol, vcol = 3 * HA, 4 * HA, 5 * HA
    return pl.pallas_call(
        _flash_b_body,
        grid=(nb, HB, nq),
        in_specs=[
            pl.BlockSpec((TQ, DH), lambda b, h, q: (b * nq + q, qcol + h)),
            pl.BlockSpec((lp, DH), lambda b, h, q: (b, kcol + h)),
            pl.BlockSpec((lp, DH), lambda b, h, q: (b, vcol + h)),
            pl.BlockSpec((TQ, LANES), lambda b, h, q: (b * nq + q, 0)),
            pl.BlockSpec((1, 1, nq, TQ), lambda b, h, q: (b, h, 0, 0)),
        ],
        out_specs=pl.BlockSpec((TQ, DH), lambda b, h, q: (b * nq + q, h)),
        out_shape=jax.ShapeDtypeStruct((nb * lp, HB * DH), F32),
        scratch_shapes=[pltpu.VMEM((TQ, 1), F32), pltpu.VMEM((TQ, 1), F32), pltpu.VMEM((TQ, DH), F32)],
        compiler_params=pltpu.CompilerParams(
            dimension_semantics=("arbitrary", "arbitrary", "arbitrary"), vmem_limit_bytes=VMEM_LIMIT),
        name="flash_b",
    )(p, p, p, fc, fk)


def _page_suffix_bias(x, carry):
    lane = lax.broadcasted_iota(jnp.int32, x.shape, 1)
    row = lax.broadcasted_iota(jnp.int32, x.shape, 0)
    y = x
    t = x
    for s in (8, 16, 32, 64):
        y = y + jnp.where(lane + s < LANES, pltpu.roll(y, LANES - s, 1), 0.0)
        t = t + pltpu.roll(t, s, 1)
    z = t
    for s in (1, 2, 4):
        z = z + jnp.where(row + s < 8, pltpu.roll(z, 8 - s, 0), 0.0)
    g = (y - x) + (z - t) + carry
    return g, carry + z[0:1, :]


def _sattn_body(pt_ref, lam_ref, qa_ref, qb_ref, kan_ref, van_ref, kbn_ref, vbn_ref, lfn_ref, g_ref,
                ka_ref, va_ref, kb_ref, vb_ref, lf_ref, oa_ref, ob_ref,
                qa_s, qb_s, ma, la, acca, mb, lb, accb, carry, *, n_pages, dec_seq, lam_init):
    del pt_ref
    pstep = pl.program_id(1)
    nrow = dec_seq * HB
    ncol = PAGE_SIZE * HB

    @pl.when(pstep == 0)
    def _():
        qa_s[...] = _split_maps(qa_ref[0] * (DQK_A ** -0.5)).astype(BF16)
        qb_s[...] = (qb_ref[0] * (DH ** -0.5)).astype(BF16)
        for m_ref, l_ref, acc_ref in ((ma, la, acca), (mb, lb, accb)):
            m_ref[...] = jnp.full(m_ref.shape, NEG_INF, F32)
            l_ref[...] = jnp.zeros(l_ref.shape, F32)
            acc_ref[...] = jnp.zeros(acc_ref.shape, F32)
        carry[...] = jnp.zeros(carry.shape, F32)

    def head_match(shape):
        r = lax.broadcasted_iota(jnp.int32, shape, 0)
        c = lax.broadcasted_iota(jnp.int32, shape, 1)
        return (r % HB) == (c % HB)

    s = _dot_nt(qa_s[...], ka_ref[0].astype(BF16))
    s = jnp.where(head_match(s.shape), s, NEG_INF)
    _online_update(s, va_ref[0].astype(BF16), ma, la, acca)

    g, total = _page_suffix_bias(lf_ref[0], carry[...])
    carry[...] = total
    bias = jnp.concatenate([g[r:r + 1, :] for r in range(8)], axis=1)
    s = _dot_nt(qb_s[...], kb_ref[0].astype(BF16)) + bias
    s = jnp.where(head_match(s.shape), s, NEG_INF)
    _online_update(s, vb_ref[0].astype(BF16), mb, lb, accb)

    @pl.when(pstep == n_pages - 1)
    def _():
        def new_mask(shape):
            r = lax.broadcasted_iota(jnp.int32, shape, 0)
            c = lax.broadcasted_iota(jnp.int32, shape, 1)
            return ((r % HB) == (c % HB)) & ((c // HB) <= ((r // HB) % dec_seq))

        s = _dot_nt(qa_s[...], kan_ref[0].astype(BF16))
        s = jnp.where(new_mask(s.shape), s, NEG_INF)
        _online_update(s, van_ref[0].astype(BF16), ma, la, acca)

        x = lfn_ref[0]
        cs = x
        for t in range(1, dec_seq):
            cs = cs + pltpu.roll(x, t * HB, 1)
        s = _dot_nt(qb_s[...], kbn_ref[0].astype(BF16)) - cs[:, :nrow]
        s = jnp.where(new_mask(s.shape), s, NEG_INF)
        _online_update(s, vbn_ref[0].astype(BF16), mb, lb, accb)

        o = acca[...] / la[...]
        o = o[:nrow] - lam_ref[0] * o[nrow:]
        oa_ref[0] = _subln(o, g_ref[...], lam_init)
        ob_ref[0] = accb[...] / lb[...]

    del ncol


def _sattn(page_table, lam, qa, qb, kan, van, kbn, vbn, lfn, g_sub, ck_a, cv_a, ck_b, cv_b, c_lf, lam_init):
    ns, n_pages = page_table.shape
    nrow = qa.shape[1]
    dec_seq = nrow // HB
    ncol = PAGE_SIZE * HB
    pt_flat = page_table.reshape(-1)

    def seq_map(b, p, pt):
        return (b, 0, 0)

    def page_map(b, p, pt):
        return (pt[b * n_pages + (n_pages - 1 - p)], 0, 0)

    seq_spec = pl.BlockSpec((1, nrow, DH), seq_map)
    page_spec = pl.BlockSpec((1, ncol, DH), page_map)
    return pl.pallas_call(
        functools.partial(_sattn_body, n_pages=n_pages, dec_seq=dec_seq, lam_init=lam_init),
        grid_spec=pltpu.PrefetchScalarGridSpec(
            num_scalar_prefetch=1,
            grid=(ns, n_pages),
            in_specs=[
                pl.BlockSpec(memory_space=pltpu.SMEM),
                seq_spec, seq_spec, seq_spec, seq_spec, seq_spec, seq_spec,
                pl.BlockSpec((1, 1, LANES), seq_map),
                pl.BlockSpec((1, DH), lambda b, p, pt: (0, 0)),
                page_spec, page_spec, page_spec, page_spec,
                pl.BlockSpec((1, 8, LANES), page_map),
            ],
            out_specs=[seq_spec, seq_spec],
            scratch_shapes=[
                pltpu.VMEM((2 * nrow, DH), BF16), pltpu.VMEM((nrow, DH), BF16),
                pltpu.VMEM((2 * nrow, 1), F32), pltpu.VMEM((2 * nrow, 1), F32), pltpu.VMEM((2 * nrow, DH), F32),
                pltpu.VMEM((nrow, 1), F32), pltpu.VMEM((nrow, 1), F32), pltpu.VMEM((nrow, DH), F32),
                pltpu.VMEM((1, LANES), F32),
            ],
        ),
        out_shape=[jax.ShapeDtypeStruct((ns, nrow, DH), F32), jax.ShapeDtypeStruct((ns, nrow, DH), F32)],
        compiler_params=pltpu.CompilerParams(
            dimension_semantics=("arbitrary", "arbitrary"), vmem_limit_bytes=VMEM_LIMIT),
        name="sattn",
    )(pt_flat, lam, qa, qb, kan, van, kbn, vbn, lfn, g_sub, ck_a, cv_a, ck_b, cv_b, c_lf)


def _outproj_body(x_ref, oa_ref, ob_ref, wa_ref, wb_ref, y_ref):
    y_ref[...] = (x_ref[...] + _dot(oa_ref[...].astype(BF16), wa_ref[...])
                  + _dot(ob_ref[...].astype(BF16), wb_ref[...]))


def _outproj(x, oa, ob, wa_bf, wb_bf):
    nt = x.shape[0]
    tm, tn = TM_PROJ, 1024
    ka, kb = oa.shape[1], ob.shape[1]
    return pl.pallas_call(
        _outproj_body,
        grid=(nt // tm, D_MODEL // tn),
        in_specs=[
            pl.BlockSpec((tm, tn), lambda i, j: (i, j)),
            pl.BlockSpec((tm, ka), lambda i, j: (i, 0)),
            pl.BlockSpec((tm, kb), lambda i, j: (i, 0)),
            pl.BlockSpec((ka, tn), lambda i, j: (0, j)),
            pl.BlockSpec((kb, tn), lambda i, j: (0, j)),
        ],
        out_specs=pl.BlockSpec((tm, tn), lambda i, j: (i, j)),
        out_shape=jax.ShapeDtypeStruct((nt, D_MODEL), F32),
        compiler_params=pltpu.CompilerParams(
            dimension_semantics=("arbitrary", "arbitrary"), vmem_limit_bytes=VMEM_LIMIT),
        name="outproj",
    )(x, oa, ob, wa_bf, wb_bf)


def _route_body(x_ref, g_ref, wr_ref, br_ref, h_ref, r_ref):
    h = _rms(x_ref[...], g_ref[...])
    h_ref[...] = h
    lg = jnp.dot(h, wr_ref[...], precision=lax.Precision.HIGHEST, preferred_element_type=F32) + br_ref[...]
    lane = lax.broadcasted_iota(jnp.int32, lg.shape, 1).astype(F32)
    big = float(LANES)

    def first_max(vals, valid):
        v = jnp.where(valid, vals, NEG_INF)
        top = jnp.max(v, axis=1, keepdims=True)
        idx = jnp.min(jnp.where(valid & (v == top), lane, big), axis=1, keepdims=True)
        return top, idx

    is_grp = lane < N_GROUPS
    gmax, grp = first_max(lg, is_grp)
    p_grp = 1.0 / jnp.sum(jnp.where(is_grp, jnp.exp(lg - gmax), 0.0), axis=1, keepdims=True)
    lo = N_GROUPS + EXPERTS_PER_GROUP * grp
    in_grp = (lane >= lo) & (lane < lo + EXPERTS_PER_GROUP)
    v0, i0 = first_max(lg, in_grp)
    v1, i1 = first_max(lg, in_grp & (lane != i0))
    e = jnp.exp(v1 - v0)
    w0 = p_grp / (1.0 + e)
    w1 = p_grp * e / (1.0 + e)
    r = jnp.where(lane == 0, i0 - N_GROUPS, 0.0)
    r = jnp.where(lane == 1, i1 - N_GROUPS, r)
    r = jnp.where(lane == 2, w0, r)
    r = jnp.where(lane == 3, w1, r)
    r_ref[...] = r


def _route(x, g, wr, br):
    nt = x.shape[0]
    tm = TM_ROUTE
    return pl.pallas_call(
        _route_body,
        grid=(nt // tm,),
        in_specs=[
            pl.BlockSpec((tm, D_MODEL), lambda i: (i, 0)),
            pl.BlockSpec((1, D_MODEL), lambda i: (0, 0)),
            pl.BlockSpec((D_MODEL, LANES), lambda i: (0, 0)),
            pl.BlockSpec((1, LANES), lambda i: (0, 0)),
        ],
        out_specs=[pl.BlockSpec((tm, D_MODEL), lambda i: (i, 0)), pl.BlockSpec((tm, LANES), lambda i: (i, 0))],
        out_shape=[jax.ShapeDtypeStruct((nt, D_MODEL), F32), jax.ShapeDtypeStruct((nt, LANES), F32)],
        compiler_params=pltpu.CompilerParams(dimension_semantics=("arbitrary",), vmem_limit_bytes=VMEM_LIMIT),
        name="route",
    )(x, g, wr, br)


def _row_copy(src_ref, dst_ref, sem, src_row, dst_row, n=1):
    return pltpu.make_async_copy(src_ref.at[pl.ds(src_row, n), :], dst_ref.at[pl.ds(dst_row, n), :], sem)


def _gather_body(idx_ref, n_ref, src_ref, dst_ref, sems):
    ch = GATHER_CHUNK
    n_chunks = n_ref[0] // ch

    def wait_chunk(c):
        _row_copy(src_ref, dst_ref, sems.at[c % 2], 0, 0, ch).wait()

    def chunk(c, carry):
        base = c * ch

        def issue(r, cc):
            _row_copy(src_ref, dst_ref, sems.at[c % 2], idx_ref[base + r], base + r).start()
            return cc

        lax.fori_loop(0, ch, issue, 0)

        @pl.when(c > 0)
        def _():
            wait_chunk(c - 1)

        return carry

    lax.fori_loop(0, n_chunks, chunk, 0)

    @pl.when(n_chunks > 0)
    def _():
        wait_chunk(n_chunks - 1)


def _gather_rows(src, idx, n_rows):
    s = idx.shape[0]
    return pl.pallas_call(
        _gather_body,
        grid_spec=pltpu.PrefetchScalarGridSpec(
            num_scalar_prefetch=2,
            grid=(1,),
            in_specs=[pl.BlockSpec(memory_space=pl.ANY)],
            out_specs=pl.BlockSpec(memory_space=pl.ANY),
            scratch_shapes=[pltpu.SemaphoreType.DMA((2,))],
        ),
        out_shape=jax.ShapeDtypeStruct((s, src.shape[1]), src.dtype),
        name="gather_rows",
    )(idx, n_rows, src)


def _expert_body(te_ref, nu_ref, hs_ref, ws_ref, wg_ref, wu_ref, wd_ref, y_ref, wg_s, wu_s, wd_s):
    t = pl.program_id(0)

    @pl.when(t < nu_ref[0])
    def _():
        e = te_ref[t]
        prev = te_ref[jnp.maximum(t - 1, 0)]

        @pl.when((t == 0) | (e != prev))
        def _():
            wg_s[...] = wg_ref[0].astype(BF16)
            wu_s[...] = wu_ref[0].astype(BF16)
            wd_s[...] = wd_ref[0].astype(BF16)

        hb = hs_ref[...].astype(BF16)
        hg = _dot(hb, wg_s[...])
        hu = _dot(hb, wu_s[...])
        act = hg * jax.nn.sigmoid(hg) * hu * ws_ref[...]
        y_ref[...] = _dot(act.astype(BF16), wd_s[...])

    @pl.when(t >= nu_ref[0])
    def _():
        y_ref[...] = jnp.zeros(y_ref.shape, F32)


def _experts(tile_expert, n_used, hs, wslot, w_gate, w_up, w_down):
    s = hs.shape[0]
    tm = TM_EXPERT
    n_tiles = s // tm

    def row_map(t, te, nu):
        return (jnp.minimum(t, nu[0] - 1), 0)

    def w_map(t, te, nu):
        return (te[t], 0, 0)

    return pl.pallas_call(
        _expert_body,
        grid_spec=pltpu.PrefetchScalarGridSpec(
            num_scalar_prefetch=2,
            grid=(n_tiles,),
            in_specs=[
                pl.BlockSpec((tm, D_MODEL), row_map),
                pl.BlockSpec((tm, 1), row_map),
                pl.BlockSpec((1, D_MODEL, D_EXPERT), w_map),
                pl.BlockSpec((1, D_MODEL, D_EXPERT), w_map),
                pl.BlockSpec((1, D_EXPERT, D_MODEL), w_map),
            ],
            out_specs=pl.BlockSpec((tm, D_MODEL), lambda t, te, nu: (t, 0)),
            scratch_shapes=[pltpu.VMEM((D_MODEL, D_EXPERT), BF16), pltpu.VMEM((D_MODEL, D_EXPERT), BF16),
                            pltpu.VMEM((D_EXPERT, D_MODEL), BF16)],
        ),
        out_shape=jax.ShapeDtypeStruct((s, D_MODEL), F32),
        compiler_params=pltpu.CompilerParams(dimension_semantics=("arbitrary",), vmem_limit_bytes=VMEM_LIMIT),
        name="experts",
    )(tile_expert, n_used, hs, wslot, w_gate, w_up, w_down)


def _combine_body(slot_ref, x_ref, y_ref, g_ref, o_ref, buf, sem, *, nt, final_norm):
    tm = TM_COMBINE
    base = pl.program_id(0) * tm

    def issue(r, c):
        for k in range(2):
            pltpu.make_async_copy(y_ref.at[pl.ds(slot_ref[k * nt + base + r], 1), :],
                                  buf.at[k, pl.ds(r, 1), :], sem).start()
        return c

    lax.fori_loop(0, tm, issue, 0)
    for k in range(2):
        pltpu.make_async_copy(y_ref.at[pl.ds(0, tm), :], buf.at[k], sem).wait()
    out = x_ref[...] + buf[0] + buf[1]
    if final_norm:
        out = _rms(out, g_ref[...])
    o_ref[...] = out


def _combine(slots, x, y, g, final_norm):
    nt = x.shape[0]
    tm = TM_COMBINE
    return pl.pallas_call(
        functools.partial(_combine_body, nt=nt, final_norm=final_norm),
        grid_spec=pltpu.PrefetchScalarGridSpec(
            num_scalar_prefetch=1,
            grid=(nt // tm,),
            in_specs=[
                pl.BlockSpec((tm, D_MODEL), lambda i, sl: (i, 0)),
                pl.BlockSpec(memory_space=pl.ANY),
                pl.BlockSpec((1, D_MODEL), lambda i, sl: (0, 0)),
            ],
            out_specs=pl.BlockSpec((tm, D_MODEL), lambda i, sl: (i, 0)),
            scratch_shapes=[pltpu.VMEM((2, tm, D_MODEL), F32), pltpu.SemaphoreType.DMA(())],
        ),
        out_shape=jax.ShapeDtypeStruct((nt, D_MODEL), F32),
        compiler_params=pltpu.CompilerParams(dimension_semantics=("arbitrary",), vmem_limit_bytes=VMEM_LIMIT),
        name="combine",
    )(slots, x, y, g)


def _dispatch_plan(route, valid):
    nt = route.shape[0]
    tm = TM_EXPERT
    n_valid = int(np.sum(valid))
    n_slots = -(-(2 * n_valid + N_EXPERTS * (tm - 1)) // tm) * tm
    n_tiles = n_slots // tm
    valid2 = jnp.asarray(np.concatenate([valid, valid]))
    eidx = jnp.concatenate([route[:, 0], route[:, 1]]).astype(jnp.int32)
    eidx = jnp.where(valid2, eidx, N_EXPERTS)
    onehot = (eidx[:, None] == jnp.arange(N_EXPERTS, dtype=jnp.int32)[None, :]).astype(jnp.int32)
    incl = jnp.cumsum(onehot, axis=0)
    counts = incl[-1]
    rank = jnp.sum((incl - 1) * onehot, axis=1)
    padded = ((counts + tm - 1) // tm) * tm
    ends = jnp.cumsum(padded)
    starts = ends - padded
    slot = jnp.where(valid2, jnp.take(starts, jnp.minimum(eidx, N_EXPERTS - 1)) + rank, 0).astype(jnp.int32)
    token = jnp.tile(jnp.arange(nt, dtype=jnp.int32), 2)
    dest = jnp.where(valid2, slot, n_slots)
    src_token = jnp.zeros((n_slots,), jnp.int32).at[dest].set(token, mode="drop")
    wts = jnp.concatenate([route[:, 2], route[:, 3]])
    wslot = jnp.zeros((n_slots,), F32).at[dest].set(wts, mode="drop")
    n_used = (ends[-1] // tm).astype(jnp.int32)
    tile_start = jnp.arange(n_tiles, dtype=jnp.int32) * tm
    tile_expert = jnp.sum((tile_start[:, None] >= ends[None, :]).astype(jnp.int32), axis=1)
    last_expert = jnp.take(tile_expert, jnp.maximum(n_used - 1, 0))
    tile_expert = jnp.where(jnp.arange(n_tiles) < n_used, tile_expert, last_expert)
    tile_expert = jnp.minimum(tile_expert, N_EXPERTS - 1).astype(jnp.int32)
    return slot, src_token, wslot.reshape(n_slots, 1), tile_expert, n_used.reshape(1)


def _moe(x, valid, g_ffn, wr, br, w_gate, w_up, w_down, g_final, final_norm):
    h, route = _route(x, g_ffn, wr, br)
    slot, src_token, wslot, tile_expert, n_used = _dispatch_plan(route, valid)
    hs = _gather_rows(h, src_token, jnp.full((1,), src_token.shape[0], jnp.int32))
    y = _experts(tile_expert, n_used, hs, wslot, w_gate, w_up, w_down)
    return _combine(slot, x, y, g_final, final_norm)


def _pool_windows(ext_rows, u, inv_cnt, w_ref, scale):
    outs = []
    for gi, w in enumerate(POOL_WINDOWS):
        c0 = gi * POOL_CH
        win = u[:, c0:c0 + POOL_CH]
        for j in range(1, w):
            win = win + ext_rows(j, c0)
        d = win * inv_cnt[gi] - u[:, c0:c0 + POOL_CH]
        outs.append(_dot(d.astype(BF16), w_ref[gi]))
    return jnp.concatenate(outs, axis=1) * scale


def _pool_prompt_body(xc_ref, xp_ref, g_ref, w_ref, sc_ref, y_ref, st_ref, ext, *, seq_len):
    i = pl.program_id(1)
    tp = TP_POOL
    pad = POOL_BUF + 1
    x = xc_ref[...]
    u = _rms(x, g_ref[...])
    tail = _rms(xp_ref[pl.ds(tp - pad, pad), :], g_ref[...])
    ext[pl.ds(0, pad), :] = jnp.where(i > 0, tail, 0.0)
    ext[pl.ds(pad, tp), :] = u
    pos = i * tp + lax.broadcasted_iota(jnp.int32, (tp, 1), 0)
    inv_cnt = [1.0 / jnp.minimum(w, pos + 1).astype(F32) for w in POOL_WINDOWS]

    def ext_rows(j, c0):
        return ext[pl.ds(pad - j, tp), c0:c0 + POOL_CH]

    y_ref[...] = x + _pool_windows(ext_rows, u, inv_cnt, w_ref, sc_ref[...])

    last = seq_len - 1

    @pl.when(i == last // tp)
    def _():
        st_ref[0] = ext[pl.ds(pad + last % tp + 1 - pad, pad), :]


def _pool_prompt(x, g, w_bf, scale, nb, lp, seq_len):
    tp = TP_POOL
    nblk = lp // tp
    pad = POOL_BUF + 1
    return pl.pallas_call(
        functools.partial(_pool_prompt_body, seq_len=seq_len),
        grid=(nb, nblk),
        in_specs=[
            pl.BlockSpec((tp, D_MODEL), lambda b, i: (b * nblk + i, 0)),
            pl.BlockSpec((tp, D_MODEL), lambda b, i: (b * nblk + jnp.maximum(i - 1, 0), 0)),
            pl.BlockSpec((1, D_MODEL), lambda b, i: (0, 0)),
            pl.BlockSpec((len(POOL_WINDOWS), POOL_CH, POOL_CH), lambda b, i: (0, 0, 0)),
            pl.BlockSpec((1, D_MODEL), lambda b, i: (0, 0)),
        ],
        out_specs=[
            pl.BlockSpec((tp, D_MODEL), lambda b, i: (b * nblk + i, 0)),
            pl.BlockSpec((1, pad, D_MODEL), lambda b, i: (b, 0, 0)),
        ],
        out_shape=[jax.ShapeDtypeStruct((nb * lp, D_MODEL), F32), jax.ShapeDtypeStruct((nb, pad, D_MODEL), F32)],
        scratch_shapes=[pltpu.VMEM((tp + pad, D_MODEL), F32)],
        compiler_params=pltpu.CompilerParams(
            dimension_semantics=("arbitrary", "arbitrary"), vmem_limit_bytes=VMEM_LIMIT),
        name="pool_prompt",
    )(x, x, g, w_bf, scale)


def _pool_sample_body(x_ref, st_ref, g_ref, w_ref, sc_ref, y_ref, so_ref, *, dec_seq):
    g = g_ref[...]
    xs = [x_ref[:, t, :] for t in range(dec_seq)]
    us = [_rms(x, g) for x in xs]

    def ext_row(e, c0):
        if e < POOL_BUF:
            return st_ref[:, e, c0:c0 + POOL_CH]
        return us[e - POOL_BUF][:, c0:c0 + POOL_CH]

    for t in range(dec_seq):
        inv_cnt = [1.0 / w for w in POOL_WINDOWS]
        y = _pool_windows(lambda j, c0: ext_row(POOL_BUF + t - j, c0), us[t], inv_cnt, w_ref, sc_ref[...])
        y_ref[:, t, :] = xs[t] + y
    for e in range(POOL_BUF):
        src = e + dec_seq
        so_ref[:, e, :] = st_ref[:, src, :] if src < POOL_BUF else us[src - POOL_BUF]


def _pool_sample(x3, state, g, w_bf, scale):
    ns, dec_seq, _ = x3.shape
    ts = 32
    return pl.pallas_call(
        functools.partial(_pool_sample_body, dec_seq=dec_seq),
        grid=(ns // ts,),
        in_specs=[
            pl.BlockSpec((ts, dec_seq, D_MODEL), lambda i: (i, 0, 0)),
            pl.BlockSpec((ts, POOL_BUF, D_MODEL), lambda i: (i, 0, 0)),
            pl.BlockSpec((1, D_MODEL), lambda i: (0, 0)),
            pl.BlockSpec((len(POOL_WINDOWS), POOL_CH, POOL_CH), lambda i: (0, 0, 0)),
            pl.BlockSpec((1, D_MODEL), lambda i: (0, 0)),
        ],
        out_specs=[
            pl.BlockSpec((ts, dec_seq, D_MODEL), lambda i: (i, 0, 0)),
            pl.BlockSpec((ts, POOL_BUF, D_MODEL), lambda i: (i, 0, 0)),
        ],
        out_shape=[jax.ShapeDtypeStruct(x3.shape, F32), jax.ShapeDtypeStruct(state.shape, F32)],
        compiler_params=pltpu.CompilerParams(dimension_semantics=("arbitrary",), vmem_limit_bytes=VMEM_LIMIT),
        name="pool_sample",
    )(x3, state, g, w_bf, scale)


def _rope_tables(pos):
    half = ROT_DIM // 2
    inv = ROPE_THETA ** (-jnp.arange(0, ROT_DIM, 2, dtype=F32) / ROT_DIM)
    ang = pos.astype(F32)[:, None] * inv[None, :]
    cos, sin = jnp.cos(ang), jnp.sin(ang)
    n = pos.shape[0]
    ones = jnp.ones((n, DQK_A - ROT_DIM), F32)
    zeros = jnp.zeros((n, DQK_A - ROT_DIM), F32)
    zh = jnp.zeros((n, half), F32)
    c = jnp.concatenate([cos, cos, ones], axis=1)
    s_lo = jnp.concatenate([zh, sin, zeros], axis=1)
    s_hi = jnp.concatenate([-sin, zh, zeros], axis=1)
    return jnp.stack([jnp.tile(t, (1, LANES // DQK_A)) for t in (c, s_lo, s_hi)])


def kernel(x_prompt, x_sample, cache_k_a, cache_v_a, cache_k_b, cache_v_b, cache_logf_b, state_pool, page_table,
           meta_tokens, g_mix, w_in, b_forget, lambda_qk, g_subln, w_out, w_pool, pool_scale, g_ffn,
           w_route_group, b_route_group, w_route_expert, b_route_expert, w_gate, w_up, w_down, g_final):
    nb, seq, _ = x_prompt.shape
    ns, dec_seq, _ = x_sample.shape
    depth = g_mix.shape[0]
    seq_len = seq + N_META
    lp = -(-seq_len // TQ) * TQ
    npr = nb * lp
    n_samp = ns * dec_seq
    nt = npr + n_samp
    past = page_table.shape[1] * PAGE_SIZE
    n_phys = cache_k_a.shape[1]
    assert nt % TM_PROJ == 0 and lp % TP_POOL == 0 and dec_seq * HB <= LANES

    xp = jnp.concatenate([jnp.broadcast_to(meta_tokens[None], (nb, N_META, D_MODEL)), x_prompt,
                          jnp.zeros((nb, lp - seq_len, D_MODEL), F32)], axis=1)
    x = jnp.concatenate([xp.reshape(npr, D_MODEL), x_sample.reshape(n_samp, D_MODEL)], axis=0)
    valid = np.concatenate([np.tile(np.arange(lp) < seq_len, nb), np.ones(n_samp, bool)])
    pos = jnp.concatenate([jnp.tile(jnp.arange(lp, dtype=jnp.int32), nb),
                           past + jnp.tile(jnp.arange(dec_seq, dtype=jnp.int32), ns)])
    rope = _rope_tables(pos)

    def prompt_rows(a):
        return a[:npr].reshape(nb, lp, -1)[:, :seq_len]

    def sample_rows(a):
        return a[npr:].reshape(ns, dec_seq * (a.shape[1] // DH), DH)

    outs = {k: [] for k in ("ka_p", "va_p", "kb_p", "vb_p", "lf_p", "pool_p",
                            "ka_s", "va_s", "kb_s", "vb_s", "lf_s", "pool_s")}
    y = None
    for l in range(depth):
        if l % 2 == 0:
            i = l // 2
            lam_init = 0.8 - 0.6 * math.exp(-0.3 * l)
            lq = lambda_qk[i]
            lam = (jnp.exp(jnp.sum(lq[0] * lq[1])) - jnp.exp(jnp.sum(lq[2] * lq[3])) + lam_init).reshape(1)
            w_main = w_in[i, :, :D_QKV].astype(BF16)
            w_f = jnp.pad(w_in[i, :, D_QKV:], ((0, 0), (0, LANES - HB))).astype(BF16)
            b_f = jnp.pad(b_forget[i], (0, LANES - HB)).reshape(1, LANES)
            g_sub = g_subln[i].reshape(1, DH)
            p, lf = _inproj(x, g_mix[l].reshape(1, D_MODEL), w_main, w_f, b_f, rope)

            cols = [p[:, c * HA * DH:(c + 1) * HA * DH] for c in range(6)]
            for name, c in (("ka", 1), ("va", 2), ("kb", 4), ("vb", 5)):
                outs[name + "_p"].append(prompt_rows(cols[c]).reshape(nb, seq_len, HA, DH))
                outs[name + "_s"].append(cols[c][npr:].reshape(ns, dec_seq, HA, DH))
            outs["lf_p"].append(prompt_rows(lf)[..., :HB])
            outs["lf_s"].append(lf[npr:, :HB].reshape(ns, dec_seq, HB))

            fc, fct = _fcum(lf[:npr], nb, lp)
            fk = fct[:, :HB].reshape(nb, HB, lp // TQ, TQ)
            oa_p = _flash_a(p, lam, g_sub, nb, lp, lam_init)
            ob_p = _flash_b(p, fc, fk, nb, lp)

            lfn = jnp.pad(lf[npr:, :HB].reshape(ns, 1, dec_seq * HB), ((0, 0), (0, 0), (0, LANES - dec_seq * HB)))
            oa_s, ob_s = _sattn(
                page_table, lam, sample_rows(cols[0]), sample_rows(cols[3]), sample_rows(cols[1]),
                sample_rows(cols[2]), sample_rows(cols[4]), sample_rows(cols[5]), lfn, g_sub,
                cache_k_a[i].reshape(n_phys, PAGE_SIZE * HA, DH), cache_v_a[i].reshape(n_phys, PAGE_SIZE * HA, DH),
                cache_k_b[i].reshape(n_phys, PAGE_SIZE * HB, DH), cache_v_b[i].reshape(n_phys, PAGE_SIZE * HB, DH),
                cache_logf_b[i].reshape(n_phys, 8, LANES), lam_init)
            oa = jnp.concatenate([oa_p, oa_s.reshape(n_samp, HA * DH)], axis=0)
            ob = jnp.concatenate([ob_p, ob_s.reshape(n_samp, HB * DH)], axis=0)
            wo = w_out[i].astype(BF16)
            x = _outproj(x, oa, ob, wo[:HA * DH], wo[HA * DH:])
        else:
            j = l // 2
            g = g_mix[l].reshape(1, D_MODEL)
            wp = w_pool[j].astype(BF16)
            sc = pool_scale[j].reshape(1, D_MODEL)
            yp, st_p = _pool_prompt(x[:npr], g, wp, sc, nb, lp, seq_len)
            ys, st_s = _pool_sample(x[npr:].reshape(ns, dec_seq, D_MODEL), state_pool[j], g, wp, sc)
            outs["pool_p"].append(st_p[:, 1:])
            outs["pool_s"].append(st_s)
            x = jnp.concatenate([yp, ys.reshape(n_samp, D_MODEL)], axis=0)

        wr = jnp.concatenate([w_route_group[l], w_route_expert[l].reshape(D_MODEL, N_EXPERTS),
                              jnp.zeros((D_MODEL, LANES - N_GROUPS - N_EXPERTS), F32)], axis=1)
        br = jnp.concatenate([b_route_group[l], b_route_expert[l].reshape(N_EXPERTS),
                              jnp.zeros((LANES - N_GROUPS - N_EXPERTS,), F32)]).reshape(1, LANES)
        last = l == depth - 1
        x = _moe(x, valid, g_ffn[l].reshape(1, D_MODEL), wr, br, w_gate[l], w_up[l], w_down[l],
                 g_final.reshape(1, D_MODEL), last)
    y = x
    y_prompt = y[:npr].reshape(nb, lp, D_MODEL)[:, N_META:seq_len]
    y_sample = y[npr:].reshape(ns, dec_seq, D_MODEL)
    st = lambda k: jnp.stack(outs[k])
    return (y_prompt, y_sample, st("ka_p"), st("va_p"), st("kb_p"), st("vb_p"), st("lf_p"), st("pool_p"),
            st("ka_s"), st("va_s"), st("kb_s"), st("vb_s"), st("lf_s"), st("pool_s"))
```

```python
import functools
import math

import jax
import jax.numpy as jnp
import numpy as np
from jax import lax
from jax.experimental import pallas as pl
from jax.experimental.pallas import tpu as pltpu

F32 = jnp.float32
BF16 = jnp.bfloat16

D_MODEL = 2048
N_META = 16
HA = 8
HB = 8
DH = 128
DQK_A = 64
ROT_DIM = 16
ROPE_THETA = 500000.0
POOL_WINDOWS = (2, 4, 8, 16)
POOL_CH = D_MODEL // len(POOL_WINDOWS)
POOL_BUF = max(POOL_WINDOWS) - 1
N_GROUPS = 4
EXPERTS_PER_GROUP = 8
N_EXPERTS = N_GROUPS * EXPERTS_PER_GROUP
D_EXPERT = D_MODEL // 4
EPS = 1e-6
NEG_INF = -1e30
PAGE_SIZE = 128
D_QKV = 6 * HA * DH

LANES = 128
SUBLANES = 8
VMEM_LIMIT = 56 * 1024 * 1024

TQ = 256
TM_PROJ = 512
TN_PROJ = 512
TM_ROUTE = 256
TM_EXPERT = 256
TM_COMBINE = 256
TP_POOL = 256
PAGES_PER_STEP = 8
DMA_UNROLL = 8


def _rms(x, g):
    return x * lax.rsqrt(jnp.mean(x * x, axis=-1, keepdims=True) + EPS) * g


def _dot(a, b):
    return jnp.dot(a, b, preferred_element_type=F32)


def _dot_nt(a, b):
    return lax.dot_general(a, b, (((1,), (1,)), ((), ())), preferred_element_type=F32)


def _inproj_body(x_ref, g_ref, w_ref, wf_ref, bf_ref, rope_ref, p_ref, lf_ref, h_scr, *, n_rope_tiles, tn):
    j = pl.program_id(1)

    @pl.when(j == 0)
    def _():
        hb = _rms(x_ref[...], g_ref[...]).astype(BF16)
        h_scr[...] = hb
        fl = _dot(hb, wf_ref[...]) + bf_ref[...]
        lf_ref[...] = jnp.minimum(fl, 0.0) - jnp.log1p(jnp.exp(-jnp.abs(fl)))

    acc = _dot(h_scr[...], w_ref[...])

    @pl.when(j < n_rope_tiles)
    def _():
        c = rope_ref[0]
        s_lo = rope_ref[1]
        s_hi = rope_ref[2]
        for cc in range(tn // LANES):
            a = acc[:, cc * LANES:(cc + 1) * LANES]
            p_ref[:, cc * LANES:(cc + 1) * LANES] = (
                a * c + pltpu.roll(a, ROT_DIM // 2, 1) * s_lo + pltpu.roll(a, LANES - ROT_DIM // 2, 1) * s_hi)

    @pl.when(j >= n_rope_tiles)
    def _():
        p_ref[...] = acc


def _inproj(x, g, w_bf, wf_bf, bf, rope):
    nt = x.shape[0]
    tm, tn = TM_PROJ, TN_PROJ
    n_rope_tiles = (2 * HA * DH) // tn
    return pl.pallas_call(
        functools.partial(_inproj_body, n_rope_tiles=n_rope_tiles, tn=tn),
        grid=(nt // tm, D_QKV // tn),
        in_specs=[
            pl.BlockSpec((tm, D_MODEL), lambda i, j: (i, 0)),
            pl.BlockSpec((1, D_MODEL), lambda i, j: (0, 0)),
            pl.BlockSpec((D_MODEL, tn), lambda i, j: (0, j)),
            pl.BlockSpec((D_MODEL, LANES), lambda i, j: (0, 0)),
            pl.BlockSpec((1, LANES), lambda i, j: (0, 0)),
            pl.BlockSpec((3, tm, LANES), lambda i, j: (0, i, 0)),
        ],
        out_specs=[
            pl.BlockSpec((tm, tn), lambda i, j: (i, j)),
            pl.BlockSpec((tm, LANES), lambda i, j: (i, 0)),
        ],
        out_shape=[jax.ShapeDtypeStruct((nt, D_QKV), F32), jax.ShapeDtypeStruct((nt, LANES), F32)],
        scratch_shapes=[pltpu.VMEM((tm, D_MODEL), BF16)],
        compiler_params=pltpu.CompilerParams(
            dimension_semantics=("arbitrary", "arbitrary"), vmem_limit_bytes=VMEM_LIMIT),
        name="inproj",
    )(x, g, w_bf, wf_bf, bf, rope)


def _fcum_body(lf_ref, fc_ref, fct_ref, *, lp):
    blk = TQ
    row = lax.broadcasted_iota(jnp.int32, (blk, blk), 0)
    col = lax.broadcasted_iota(jnp.int32, (blk, blk), 1)
    tri = jnp.where(row >= col, 1.0, 0.0).astype(BF16)
    carry = jnp.zeros((1, LANES), F32)
    for b in range(lp // blk):
        x = lf_ref[pl.ds(b * blk, blk), :]
        hi = x.astype(BF16)
        r1 = x - hi.astype(F32)
        mid = r1.astype(BF16)
        lo = (r1 - mid.astype(F32)).astype(BF16)
        c = _dot(tri, hi) + _dot(tri, mid) + _dot(tri, lo) + carry
        fc_ref[pl.ds(b * blk, blk), :] = c
        carry = c[blk - 1:blk, :]
    fct_ref[0] = fc_ref[...].T


def _fcum(lf, nb, lp):
    return pl.pallas_call(
        functools.partial(_fcum_body, lp=lp),
        grid=(nb,),
        in_specs=[pl.BlockSpec((lp, LANES), lambda b: (b, 0))],
        out_specs=[pl.BlockSpec((lp, LANES), lambda b: (b, 0)),
                   pl.BlockSpec((1, LANES, lp), lambda b: (b, 0, 0))],
        out_shape=[jax.ShapeDtypeStruct((nb * lp, LANES), F32), jax.ShapeDtypeStruct((nb, LANES, lp), F32)],
        name="fcum",
    )(lf)


def _split_maps(q):
    lane = lax.broadcasted_iota(jnp.int32, q.shape, 1)
    return jnp.concatenate([jnp.where(lane < DQK_A, q, 0.0), jnp.where(lane >= DQK_A, q, 0.0)], axis=0)


def _subln(o, g, lam_init):
    return _rms(o, g) * (1.0 - lam_init)


def _causal_rows(qs, kb, vb, i, bias=None):
    n = (i + 1) * TQ
    s = _dot_nt(qs, kb[:n])
    if bias is not None:
        s = s + bias(n)
    row = lax.broadcasted_iota(jnp.int32, (qs.shape[0], TQ), 0) % TQ
    col = lax.broadcasted_iota(jnp.int32, (qs.shape[0], TQ), 1)
    diag = jnp.where(col <= row, s[:, n - TQ:], NEG_INF)
    s = diag if i == 0 else jnp.concatenate([s[:, :n - TQ], diag], axis=1)
    p = jnp.exp(s - jnp.max(s, axis=1, keepdims=True))
    return _dot(p.astype(BF16), vb[:n]) / jnp.sum(p, axis=1, keepdims=True)


def _flash_a_body(lam_ref, q_ref, k_ref, v_ref, g_ref, o_ref, *, lam_init, nq):
    kb = k_ref[...].astype(BF16)
    vb = v_ref[...].astype(BF16)
    for i in range(nq):
        qs = _split_maps(q_ref[pl.ds(i * TQ, TQ), :] * (DQK_A ** -0.5)).astype(BF16)
        o = _causal_rows(qs, kb, vb, i)
        o = o[:TQ] - lam_ref[0] * o[TQ:]
        o_ref[pl.ds(i * TQ, TQ), :] = _subln(o, g_ref[...], lam_init)


def _flash_b_body(q_ref, k_ref, v_ref, fq_ref, fk_ref, o_ref, *, nq):
    h = pl.program_id(1)
    kb = k_ref[...].astype(BF16)
    vb = v_ref[...].astype(BF16)
    fk = fk_ref[0, 0]
    lane = lax.broadcasted_iota(jnp.int32, (TQ, LANES), 1)
    for i in range(nq):
        qs = (q_ref[pl.ds(i * TQ, TQ), :] * (DH ** -0.5)).astype(BF16)
        fq_col = jnp.sum(jnp.where(lane == h, fq_ref[pl.ds(i * TQ, TQ), :], 0.0), axis=1, keepdims=True)
        o_ref[pl.ds(i * TQ, TQ), :] = _causal_rows(qs, kb, vb, i, bias=lambda n: fq_col - fk[:, :n])


def _flash_a(p, lam, g_sub, nb, lp, lam_init):
    nq = lp // TQ
    qcol, kcol, vcol = 0, HA, 2 * HA
    seq = lambda c: pl.BlockSpec((lp, DH), lambda b, h: (b, c + h))
    return pl.pallas_call(
        functools.partial(_flash_a_body, lam_init=lam_init, nq=nq),
        grid=(nb, HA),
        in_specs=[pl.BlockSpec(memory_space=pltpu.SMEM), seq(qcol), seq(kcol), seq(vcol),
                  pl.BlockSpec((1, DH), lambda b, h: (0, 0))],
        out_specs=pl.BlockSpec((lp, DH), lambda b, h: (b, h)),
        out_shape=jax.ShapeDtypeStruct((nb * lp, HA * DH), F32),
        compiler_params=pltpu.CompilerParams(
            dimension_semantics=("arbitrary", "arbitrary"), vmem_limit_bytes=VMEM_LIMIT),
        name="flash_a",
    )(lam, p, p, p, g_sub)


def _flash_b(p, fc, fk, nb, lp):
    nq = lp // TQ
    qcol, kcol, vcol = 3 * HA, 4 * HA, 5 * HA
    seq = lambda c: pl.BlockSpec((lp, DH), lambda b, h: (b, c + h))
    return pl.pallas_call(
        functools.partial(_flash_b_body, nq=nq),
        grid=(nb, HB),
        in_specs=[seq(qcol), seq(kcol), seq(vcol),
                  pl.BlockSpec((lp, LANES), lambda b, h: (b, 0)),
                  pl.BlockSpec((1, 1, 1, lp), lambda b, h: (b, h, 0, 0))],
        out_specs=pl.BlockSpec((lp, DH), lambda b, h: (b, h)),
        out_shape=jax.ShapeDtypeStruct((nb * lp, HB * DH), F32),
        compiler_params=pltpu.CompilerParams(
            dimension_semantics=("arbitrary", "arbitrary"), vmem_limit_bytes=VMEM_LIMIT),
        name="flash_b",
    )(p, p, p, fc, fk)


def _online_update(s, pv, m_ref, l_ref, acc_ref):
    m_prev = m_ref[...]
    m_new = jnp.maximum(m_prev, jnp.max(s, axis=1, keepdims=True))
    alpha = jnp.exp(m_prev - m_new)
    p = jnp.exp(s - m_new)
    l_ref[...] = alpha * l_ref[...] + jnp.sum(p, axis=1, keepdims=True)
    acc_ref[...] = alpha * acc_ref[...] + pv(p.astype(BF16))
    m_ref[...] = m_new


def _page_suffix_bias(x, carry):
    lane = lax.broadcasted_iota(jnp.int32, x.shape, 1)
    row = lax.broadcasted_iota(jnp.int32, x.shape, 0)
    y = x
    t = x
    for s in (8, 16, 32, 64):
        y = y + jnp.where(lane + s < LANES, pltpu.roll(y, LANES - s, 1), 0.0)
        t = t + pltpu.roll(t, s, 1)
    z = t
    for s in (1, 2, 4):
        z = z + jnp.where(row + s < SUBLANES, pltpu.roll(z, SUBLANES - s, 0), 0.0)
    g = (y - x) + (z - t) + carry
    return g, carry + z[0:1, :]


def _sattn_body(*refs, n_steps, dec_seq, lam_init):
    pps = PAGES_PER_STEP
    (_, lam_ref, qa_ref, qb_ref, kan_ref, van_ref, kbn_ref, vbn_ref, lfn_ref, g_ref), refs = refs[:10], refs[10:]
    ka_refs, va_refs, kb_refs, vb_refs, lf_refs = (refs[j * pps:(j + 1) * pps] for j in range(5))
    oa_ref, ob_ref, qa_s, qb_s, ma, la, acca, mb, lb, accb, carry = refs[5 * pps:]
    pstep = pl.program_id(1)
    nrow = dec_seq * HB
    ncol = PAGE_SIZE * HB

    @pl.when(pstep == 0)
    def _():
        qa_s[...] = _split_maps(qa_ref[0] * (DQK_A ** -0.5)).astype(BF16)
        qb_s[...] = (qb_ref[0] * (DH ** -0.5)).astype(BF16)
        for m_ref, l_ref, acc_ref in ((ma, la, acca), (mb, lb, accb)):
            m_ref[...] = jnp.full(m_ref.shape, NEG_INF, F32)
            l_ref[...] = jnp.zeros(l_ref.shape, F32)
            acc_ref[...] = jnp.zeros(acc_ref.shape, F32)
        carry[...] = jnp.zeros(carry.shape, F32)

    def head_match(nr):
        r = lax.broadcasted_iota(jnp.int32, (nr, ncol), 0)
        c = lax.broadcasted_iota(jnp.int32, (nr, ncol), 1)
        return (r % HB) == (c % HB)

    def paged(q_s, k_refs, v_refs, biases, m_ref, l_ref, acc_ref):
        hm = head_match(q_s.shape[0])
        cols = []
        for j in range(pps):
            s = _dot_nt(q_s[...], k_refs[j][0].astype(BF16))
            if biases is not None:
                s = s + biases[j]
            cols.append(jnp.where(hm, s, NEG_INF))

        def pv(p):
            out = _dot(p[:, :ncol], v_refs[0][0].astype(BF16))
            for j in range(1, pps):
                out = out + _dot(p[:, j * ncol:(j + 1) * ncol], v_refs[j][0].astype(BF16))
            return out

        _online_update(jnp.concatenate(cols, axis=1), pv, m_ref, l_ref, acc_ref)

    paged(qa_s, ka_refs, va_refs, None, ma, la, acca)

    biases = []
    c = carry[...]
    for j in range(pps):
        g, c = _page_suffix_bias(lf_refs[j][0], c)
        biases.append(jnp.concatenate([g[r:r + 1, :] for r in range(SUBLANES)], axis=1))
    carry[...] = c
    paged(qb_s, kb_refs, vb_refs, biases, mb, lb, accb)

    @pl.when(pstep == n_steps - 1)
    def _():
        def new_mask(shape):
            r = lax.broadcasted_iota(jnp.int32, shape, 0)
            c = lax.broadcasted_iota(jnp.int32, shape, 1)
            return ((r % HB) == (c % HB)) & ((c // HB) <= ((r // HB) % dec_seq))

        s = _dot_nt(qa_s[...], kan_ref[0].astype(BF16))
        s = jnp.where(new_mask(s.shape), s, NEG_INF)
        _online_update(s, lambda p: _dot(p, van_ref[0].astype(BF16)), ma, la, acca)

        x = lfn_ref[0]
        cs = x
        for t in range(1, dec_seq):
            cs = cs + pltpu.roll(x, t * HB, 1)
        s = _dot_nt(qb_s[...], kbn_ref[0].astype(BF16)) - cs[:, :nrow]
        s = jnp.where(new_mask(s.shape), s, NEG_INF)
        _online_update(s, lambda p: _dot(p, vbn_ref[0].astype(BF16)), mb, lb, accb)

        o = acca[...] / la[...]
        o = o[:nrow] - lam_ref[0] * o[nrow:]
        oa_ref[0] = _subln(o, g_ref[...], lam_init)
        ob_ref[0] = accb[...] / lb[...]


def _sattn(page_table, lam, qa, qb, kan, van, kbn, vbn, lfn, g_sub, ck_a, cv_a, ck_b, cv_b, c_lf, lam_init):
    ns, n_pages = page_table.shape
    pps = PAGES_PER_STEP
    n_steps = n_pages // pps
    nrow = qa.shape[1]
    dec_seq = nrow // HB
    ncol = PAGE_SIZE * HB
    pt_flat = page_table.reshape(-1)

    def seq_map(b, p, pt):
        return (b, 0, 0)

    def page_map(j):
        return lambda b, p, pt: (pt[b * n_pages + (n_pages - 1 - (p * pps + j))], 0, 0)

    seq_spec = pl.BlockSpec((1, nrow, DH), seq_map)
    page_specs = [pl.BlockSpec((1, ncol, DH), page_map(j)) for j in range(pps)]
    lf_specs = [pl.BlockSpec((1, SUBLANES, LANES), page_map(j)) for j in range(pps)]
    return pl.pallas_call(
        functools.partial(_sattn_body, n_steps=n_steps, dec_seq=dec_seq, lam_init=lam_init),
        grid_spec=pltpu.PrefetchScalarGridSpec(
            num_scalar_prefetch=1,
            grid=(ns, n_steps),
            in_specs=[
                pl.BlockSpec(memory_space=pltpu.SMEM),
                seq_spec, seq_spec, seq_spec, seq_spec, seq_spec, seq_spec,
                pl.BlockSpec((1, 1, LANES), seq_map),
                pl.BlockSpec((1, DH), lambda b, p, pt: (0, 0)),
            ] + page_specs * 4 + lf_specs,
            out_specs=[seq_spec, seq_spec],
            scratch_shapes=[
                pltpu.VMEM((2 * nrow, DH), BF16), pltpu.VMEM((nrow, DH), BF16),
                pltpu.VMEM((2 * nrow, 1), F32), pltpu.VMEM((2 * nrow, 1), F32), pltpu.VMEM((2 * nrow, DH), F32),
                pltpu.VMEM((nrow, 1), F32), pltpu.VMEM((nrow, 1), F32), pltpu.VMEM((nrow, DH), F32),
                pltpu.VMEM((1, LANES), F32),
            ],
        ),
        out_shape=[jax.ShapeDtypeStruct((ns, nrow, DH), F32), jax.ShapeDtypeStruct((ns, nrow, DH), F32)],
        compiler_params=pltpu.CompilerParams(
            dimension_semantics=("arbitrary", "arbitrary"), vmem_limit_bytes=VMEM_LIMIT),
        name="sattn",
    )(pt_flat, lam, qa, qb, kan, van, kbn, vbn, lfn, g_sub,
      *([ck_a] * pps), *([cv_a] * pps), *([ck_b] * pps), *([cv_b] * pps), *([c_lf] * pps))


def _outproj_body(x_ref, oap_ref, obp_ref, oas_ref, obs_ref, wa_ref, wb_ref, y_ref, *, n_prompt_tiles):
    is_sample = pl.program_id(0) >= n_prompt_tiles
    oa = jnp.where(is_sample, oas_ref[...], oap_ref[...]).astype(BF16)
    ob = jnp.where(is_sample, obs_ref[...], obp_ref[...]).astype(BF16)
    y_ref[...] = x_ref[...] + _dot(oa, wa_ref[...]) + _dot(ob, wb_ref[...])


def _outproj(x, oa_p, ob_p, oa_s, ob_s, wa_bf, wb_bf):
    nt = x.shape[0]
    tm, tn = TM_PROJ, 1024
    ka, kb = oa_p.shape[1], ob_p.shape[1]
    npt = oa_p.shape[0] // tm
    nst = oa_s.shape[0] // tm
    assert npt * tm == oa_p.shape[0] and nst * tm == oa_s.shape[0] and (npt + nst) * tm == nt
    p_map = lambda i, j: (jnp.minimum(i, npt - 1), 0)
    s_map = lambda i, j: (jnp.maximum(i - npt, 0), 0)
    return pl.pallas_call(
        functools.partial(_outproj_body, n_prompt_tiles=npt),
        grid=(nt // tm, D_MODEL // tn),
        in_specs=[
            pl.BlockSpec((tm, tn), lambda i, j: (i, j)),
            pl.BlockSpec((tm, ka), p_map),
            pl.BlockSpec((tm, kb), p_map),
            pl.BlockSpec((tm, ka), s_map),
            pl.BlockSpec((tm, kb), s_map),
            pl.BlockSpec((ka, tn), lambda i, j: (0, j)),
            pl.BlockSpec((kb, tn), lambda i, j: (0, j)),
        ],
        out_specs=pl.BlockSpec((tm, tn), lambda i, j: (i, j)),
        out_shape=jax.ShapeDtypeStruct((nt, D_MODEL), F32),
        compiler_params=pltpu.CompilerParams(
            dimension_semantics=("arbitrary", "arbitrary"), vmem_limit_bytes=VMEM_LIMIT),
        name="outproj",
    )(x, oa_p, ob_p, oa_s, ob_s, wa_bf, wb_bf)


def _route_body(x_ref, g_ref, wr_ref, br_ref, h_ref, r_ref):
    h = _rms(x_ref[...], g_ref[...])
    h_ref[...] = h
    lg = jnp.dot(h, wr_ref[...], precision=lax.Precision.HIGHEST, preferred_element_type=F32) + br_ref[...]
    lane = lax.broadcasted_iota(jnp.int32, lg.shape, 1).astype(F32)
    big = float(LANES)

    def first_max(vals, valid):
        v = jnp.where(valid, vals, NEG_INF)
        top = jnp.max(v, axis=1, keepdims=True)
        idx = jnp.min(jnp.where(valid & (v == top), lane, big), axis=1, keepdims=True)
        return top, idx

    is_grp = lane < N_GROUPS
    gmax, grp = first_max(lg, is_grp)
    p_grp = 1.0 / jnp.sum(jnp.where(is_grp, jnp.exp(lg - gmax), 0.0), axis=1, keepdims=True)
    lo = N_GROUPS + EXPERTS_PER_GROUP * grp
    in_grp = (lane >= lo) & (lane < lo + EXPERTS_PER_GROUP)
    v0, i0 = first_max(lg, in_grp)
    v1, i1 = first_max(lg, in_grp & (lane != i0))
    e = jnp.exp(v1 - v0)
    w0 = p_grp / (1.0 + e)
    w1 = p_grp * e / (1.0 + e)
    r = jnp.where(lane == 0, i0 - N_GROUPS, 0.0)
    r = jnp.where(lane == 1, i1 - N_GROUPS, r)
    r = jnp.where(lane == 2, w0, r)
    r = jnp.where(lane == 3, w1, r)
    r_ref[...] = r


def _route(x, g, wr, br):
    nt = x.shape[0]
    tm = TM_ROUTE
    return pl.pallas_call(
        _route_body,
        grid=(nt // tm,),
        in_specs=[
            pl.BlockSpec((tm, D_MODEL), lambda i: (i, 0)),
            pl.BlockSpec((1, D_MODEL), lambda i: (0, 0)),
            pl.BlockSpec((D_MODEL, LANES), lambda i: (0, 0)),
            pl.BlockSpec((1, LANES), lambda i: (0, 0)),
        ],
        out_specs=[pl.BlockSpec((tm, D_MODEL), lambda i: (i, 0)), pl.BlockSpec((tm, LANES), lambda i: (i, 0))],
        out_shape=[jax.ShapeDtypeStruct((nt, D_MODEL), F32), jax.ShapeDtypeStruct((nt, LANES), F32)],
        compiler_params=pltpu.CompilerParams(dimension_semantics=("arbitrary",), vmem_limit_bytes=VMEM_LIMIT),
        name="route",
    )(x, g, wr, br)


def _start_row_gather(idx_ref, idx_base, src_ref, dst_ref, sem, n_rows):
    def issue(c, carry):
        for u in range(DMA_UNROLL):
            r = c * DMA_UNROLL + u
            pltpu.make_async_copy(src_ref.at[pl.ds(idx_ref[idx_base + r], 1), :],
                                  dst_ref.at[pl.ds(r, 1), :], sem).start()
        return carry

    lax.fori_loop(0, n_rows // DMA_UNROLL, issue, 0)


def _wait_row_gather(src_ref, dst_ref, sem, n_rows):
    pltpu.make_async_copy(src_ref.at[pl.ds(0, n_rows), :], dst_ref, sem).wait()


def _expert_body(te_ref, nu_ref, src_ref, h_ref, wg_ref, wu_ref, wd_ref, y_ref, hbuf, sems, wg_s, wu_s, wd_s):
    tm = TM_EXPERT
    t = pl.program_id(0)
    n_used = nu_ref[0]

    def start(tile):
        slot = tile % 2
        _start_row_gather(src_ref, tile * tm, h_ref, hbuf.at[slot], sems.at[slot], tm)

    @pl.when(t == 0)
    def _():
        start(0)

    @pl.when(t + 1 < n_used)
    def _():
        start(t + 1)

    @pl.when(t < n_used)
    def _():
        e = te_ref[t]
        prev = te_ref[jnp.maximum(t - 1, 0)]

        @pl.when((t == 0) | (e != prev))
        def _():
            wg_s[...] = wg_ref[0, 0].astype(BF16)
            wu_s[...] = wu_ref[0, 0].astype(BF16)
            wd_s[...] = wd_ref[0, 0].astype(BF16)

        slot = t % 2
        _wait_row_gather(h_ref, hbuf.at[slot], sems.at[slot], tm)
        hb = hbuf[slot].astype(BF16)
        hg = _dot(hb, wg_s[...])
        hu = _dot(hb, wu_s[...])
        act = hg * jax.nn.sigmoid(hg) * hu
        y_ref[...] = _dot(act.astype(BF16), wd_s[...])

    @pl.when(t >= n_used)
    def _():
        y_ref[...] = jnp.zeros(y_ref.shape, F32)


def _experts(tile_expert, n_used, src_token, h, w_gate, w_up, w_down, layer):
    tm = TM_EXPERT
    n_slots = src_token.shape[0]

    def w_map(t, te, nu, src):
        return (layer, te[t], 0, 0)

    return pl.pallas_call(
        _expert_body,
        grid_spec=pltpu.PrefetchScalarGridSpec(
            num_scalar_prefetch=3,
            grid=(n_slots // tm,),
            in_specs=[
                pl.BlockSpec(memory_space=pl.ANY),
                pl.BlockSpec((1, 1, D_MODEL, D_EXPERT), w_map),
                pl.BlockSpec((1, 1, D_MODEL, D_EXPERT), w_map),
                pl.BlockSpec((1, 1, D_EXPERT, D_MODEL), w_map),
            ],
            out_specs=pl.BlockSpec((tm, D_MODEL), lambda t, te, nu, src: (t, 0)),
            scratch_shapes=[pltpu.VMEM((2, tm, D_MODEL), F32), pltpu.SemaphoreType.DMA((2,)),
                            pltpu.VMEM((D_MODEL, D_EXPERT), BF16), pltpu.VMEM((D_MODEL, D_EXPERT), BF16),
                            pltpu.VMEM((D_EXPERT, D_MODEL), BF16)],
        ),
        out_shape=jax.ShapeDtypeStruct((n_slots, D_MODEL), F32),
        compiler_params=pltpu.CompilerParams(dimension_semantics=("arbitrary",), vmem_limit_bytes=VMEM_LIMIT),
        name="experts",
    )(tile_expert, n_used, src_token, h, w_gate, w_up, w_down)


def _combine_body(slot_ref, x_ref, r_ref, y_ref, g_ref, o_ref, buf, sem, *, nt, final_norm):
    tm = TM_COMBINE
    base = pl.program_id(0) * tm
    for k in range(2):
        _start_row_gather(slot_ref, k * nt + base, y_ref, buf.at[k], sem, tm)
    for k in range(2):
        _wait_row_gather(y_ref, buf.at[k], sem, tm)
    r = r_ref[...]
    out = x_ref[...] + r[:, 2:3] * buf[0] + r[:, 3:4] * buf[1]
    if final_norm:
        out = _rms(out, g_ref[...])
    o_ref[...] = out


def _combine(slots, x, route, y, g, final_norm):
    nt = x.shape[0]
    tm = TM_COMBINE
    return pl.pallas_call(
        functools.partial(_combine_body, nt=nt, final_norm=final_norm),
        grid_spec=pltpu.PrefetchScalarGridSpec(
            num_scalar_prefetch=1,
            grid=(nt // tm,),
            in_specs=[
                pl.BlockSpec((tm, D_MODEL), lambda i, sl: (i, 0)),
                pl.BlockSpec((tm, LANES), lambda i, sl: (i, 0)),
                pl.BlockSpec(memory_space=pl.ANY),
                pl.BlockSpec((1, D_MODEL), lambda i, sl: (0, 0)),
            ],
            out_specs=pl.BlockSpec((tm, D_MODEL), lambda i, sl: (i, 0)),
            scratch_shapes=[pltpu.VMEM((2, tm, D_MODEL), F32), pltpu.SemaphoreType.DMA(())],
        ),
        out_shape=jax.ShapeDtypeStruct((nt, D_MODEL), F32),
        compiler_params=pltpu.CompilerParams(dimension_semantics=("arbitrary",), vmem_limit_bytes=VMEM_LIMIT),
        name="combine",
    )(slots, x, route, y, g)


def _dispatch_plan(route, valid):
    nt = route.shape[0]
    tm = TM_EXPERT
    n_valid = int(np.sum(valid))
    n_slots = -(-(2 * n_valid + N_EXPERTS * (tm - 1)) // tm) * tm
    n_tiles = n_slots // tm
    valid2 = jnp.asarray(np.concatenate([valid, valid]))
    eidx = jnp.concatenate([route[:, 0], route[:, 1]]).astype(jnp.int32)
    eidx = jnp.where(valid2, eidx, N_EXPERTS)
    onehot = (eidx[:, None] == jnp.arange(N_EXPERTS, dtype=jnp.int32)[None, :]).astype(jnp.int32)
    incl = jnp.cumsum(onehot, axis=0)
    counts = incl[-1]
    rank = jnp.sum((incl - 1) * onehot, axis=1)
    padded = ((counts + tm - 1) // tm) * tm
    ends = jnp.cumsum(padded)
    starts = ends - padded
    slot = jnp.where(valid2, jnp.take(starts, jnp.minimum(eidx, N_EXPERTS - 1)) + rank, 0).astype(jnp.int32)
    token = jnp.tile(jnp.arange(nt, dtype=jnp.int32), 2)
    dest = jnp.where(valid2, slot, n_slots)
    src_token = jnp.zeros((n_slots,), jnp.int32).at[dest].set(token, mode="drop")
    n_used = (ends[-1] // tm).astype(jnp.int32)
    tile_start = jnp.arange(n_tiles, dtype=jnp.int32) * tm
    tile_expert = jnp.sum((tile_start[:, None] >= ends[None, :]).astype(jnp.int32), axis=1)
    last_expert = jnp.take(tile_expert, jnp.maximum(n_used - 1, 0))
    tile_expert = jnp.where(jnp.arange(n_tiles) < n_used, tile_expert, last_expert)
    tile_expert = jnp.minimum(tile_expert, N_EXPERTS - 1).astype(jnp.int32)
    return slot, src_token, tile_expert, n_used.reshape(1)


def _moe(x, valid, g_ffn, wr, br, w_gate, w_up, w_down, layer, g_final, final_norm):
    h, route = _route(x, g_ffn, wr, br)
    slot, src_token, tile_expert, n_used = _dispatch_plan(route, valid)
    y = _experts(tile_expert, n_used, src_token, h, w_gate, w_up, w_down, layer)
    return _combine(slot, x, route, y, g_final, final_norm)


def _pool_windows(ext_rows, u, inv_cnt, w_ref, scale):
    outs = []
    for gi, w in enumerate(POOL_WINDOWS):
        c0 = gi * POOL_CH
        win = u[:, c0:c0 + POOL_CH]
        for j in range(1, w):
            win = win + ext_rows(j, c0)
        d = win * inv_cnt[gi] - u[:, c0:c0 + POOL_CH]
        outs.append(_dot(d.astype(BF16), w_ref[gi]))
    return jnp.concatenate(outs, axis=1) * scale


def _pool_prompt_body(xc_ref, xp_ref, g_ref, w_ref, sc_ref, y_ref, st_ref, ext, *, seq_len, nblk, n_prompt_blocks):
    step = pl.program_id(0)
    i = step % nblk
    tp = TP_POOL
    pad = POOL_BUF + 1
    last = seq_len - 1

    @pl.when(step < n_prompt_blocks)
    def _():
        x = xc_ref[...]
        u = _rms(x, g_ref[...])
        tail = _rms(xp_ref[pl.ds(tp - pad, pad), :], g_ref[...])
        ext[pl.ds(0, pad), :] = jnp.where(i > 0, tail, 0.0)
        ext[pl.ds(pad, tp), :] = u
        pos = i * tp + lax.broadcasted_iota(jnp.int32, (tp, 1), 0)
        inv_cnt = [1.0 / jnp.minimum(w, pos + 1).astype(F32) for w in POOL_WINDOWS]

        def ext_rows(j, c0):
            return ext[pl.ds(pad - j, tp), c0:c0 + POOL_CH]

        y_ref[...] = x + _pool_windows(ext_rows, u, inv_cnt, w_ref, sc_ref[...])

        @pl.when(i == last // tp)
        def _():
            st_ref[0] = ext[pl.ds(last % tp + 1, pad), :]

    @pl.when(step >= n_prompt_blocks)
    def _():
        y_ref[...] = xc_ref[...]


def _pool_prompt(x, g, w_bf, scale, nb, lp, seq_len):
    tp = TP_POOL
    nt = x.shape[0]
    nblk = lp // tp
    npb = nb * nblk
    pad = POOL_BUF + 1
    assert nt % tp == 0

    def prev_map(s):
        return (jnp.where((s % nblk == 0) | (s >= npb), s, s - 1), 0)

    return pl.pallas_call(
        functools.partial(_pool_prompt_body, seq_len=seq_len, nblk=nblk, n_prompt_blocks=npb),
        grid=(nt // tp,),
        in_specs=[
            pl.BlockSpec((tp, D_MODEL), lambda s: (s, 0)),
            pl.BlockSpec((tp, D_MODEL), prev_map),
            pl.BlockSpec((1, D_MODEL), lambda s: (0, 0)),
            pl.BlockSpec((len(POOL_WINDOWS), POOL_CH, POOL_CH), lambda s: (0, 0, 0)),
            pl.BlockSpec((1, D_MODEL), lambda s: (0, 0)),
        ],
        out_specs=[
            pl.BlockSpec((tp, D_MODEL), lambda s: (s, 0)),
            pl.BlockSpec((1, pad, D_MODEL), lambda s: (jnp.minimum(s // nblk, nb - 1), 0, 0)),
        ],
        out_shape=[jax.ShapeDtypeStruct((nt, D_MODEL), F32), jax.ShapeDtypeStruct((nb, pad, D_MODEL), F32)],
        scratch_shapes=[pltpu.VMEM((tp + pad, D_MODEL), F32)],
        compiler_params=pltpu.CompilerParams(dimension_semantics=("arbitrary",), vmem_limit_bytes=VMEM_LIMIT),
        name="pool_prompt",
    )(x, x, g, w_bf, scale)


def _pool_sample_body(x_ref, st_ref, g_ref, w_ref, sc_ref, y_ref, so_ref, *, dec_seq):
    g = g_ref[...]
    xs = [x_ref[:, t, :] for t in range(dec_seq)]
    us = [_rms(x, g) for x in xs]

    def ext_row(e, c0):
        if e < POOL_BUF:
            return st_ref[:, e, c0:c0 + POOL_CH]
        return us[e - POOL_BUF][:, c0:c0 + POOL_CH]

    for t in range(dec_seq):
        inv_cnt = [1.0 / w for w in POOL_WINDOWS]
        y = _pool_windows(lambda j, c0: ext_row(POOL_BUF + t - j, c0), us[t], inv_cnt, w_ref, sc_ref[...])
        y_ref[:, t, :] = xs[t] + y
    for e in range(POOL_BUF):
        src = e + dec_seq
        so_ref[:, e, :] = st_ref[:, src, :] if src < POOL_BUF else us[src - POOL_BUF]


def _pool_sample(x3, state, g, w_bf, scale):
    ns, dec_seq, _ = x3.shape
    ts = 32
    return pl.pallas_call(
        functools.partial(_pool_sample_body, dec_seq=dec_seq),
        grid=(ns // ts,),
        in_specs=[
            pl.BlockSpec((ts, dec_seq, D_MODEL), lambda i: (i, 0, 0)),
            pl.BlockSpec((ts, POOL_BUF, D_MODEL), lambda i: (i, 0, 0)),
            pl.BlockSpec((1, D_MODEL), lambda i: (0, 0)),
            pl.BlockSpec((len(POOL_WINDOWS), POOL_CH, POOL_CH), lambda i: (0, 0, 0)),
            pl.BlockSpec((1, D_MODEL), lambda i: (0, 0)),
        ],
        out_specs=[
            pl.BlockSpec((ts, dec_seq, D_MODEL), lambda i: (i, 0, 0)),
            pl.BlockSpec((ts, POOL_BUF, D_MODEL), lambda i: (i, 0, 0)),
        ],
        out_shape=[jax.ShapeDtypeStruct(x3.shape, F32), jax.ShapeDtypeStruct(state.shape, F32)],
        compiler_params=pltpu.CompilerParams(dimension_semantics=("arbitrary",), vmem_limit_bytes=VMEM_LIMIT),
        name="pool_sample",
    )(x3, state, g, w_bf, scale)


def _rope_tables(pos):
    half = ROT_DIM // 2
    inv = ROPE_THETA ** (-jnp.arange(0, ROT_DIM, 2, dtype=F32) / ROT_DIM)
    ang = pos.astype(F32)[:, None] * inv[None, :]
    cos, sin = jnp.cos(ang), jnp.sin(ang)
    n = pos.shape[0]
    ones = jnp.ones((n, DQK_A - ROT_DIM), F32)
    zeros = jnp.zeros((n, DQK_A - ROT_DIM), F32)
    zh = jnp.zeros((n, half), F32)
    c = jnp.concatenate([cos, cos, ones], axis=1)
    s_lo = jnp.concatenate([zh, sin, zeros], axis=1)
    s_hi = jnp.concatenate([-sin, zh, zeros], axis=1)
    return jnp.stack([jnp.tile(t, (1, LANES // DQK_A)) for t in (c, s_lo, s_hi)])


def kernel(x_prompt, x_sample, cache_k_a, cache_v_a, cache_k_b, cache_v_b, cache_logf_b, state_pool, page_table,
           meta_tokens, g_mix, w_in, b_forget, lambda_qk, g_subln, w_out, w_pool, pool_scale, g_ffn,
           w_route_group, b_route_group, w_route_expert, b_route_expert, w_gate, w_up, w_down, g_final):
    nb, seq, _ = x_prompt.shape
    ns, dec_seq, _ = x_sample.shape
    depth = g_mix.shape[0]
    seq_len = seq + N_META
    lp = -(-seq_len // TQ) * TQ
    npr = nb * lp
    n_samp = ns * dec_seq
    nt = npr + n_samp
    past = page_table.shape[1] * PAGE_SIZE
    n_phys = cache_k_a.shape[1]
    assert nt % TM_PROJ == 0 and lp % TP_POOL == 0 and dec_seq * HB <= LANES
    assert page_table.shape[1] % PAGES_PER_STEP == 0 and PAGE_SIZE * HB == SUBLANES * LANES

    pieces = []
    for b in range(nb):
        pieces += [meta_tokens, x_prompt[b], jnp.zeros((lp - seq_len, D_MODEL), F32)]
    x = jnp.concatenate(pieces + [x_sample.reshape(n_samp, D_MODEL)], axis=0)
    valid = np.concatenate([np.tile(np.arange(lp) < seq_len, nb), np.ones(n_samp, bool)])
    pos = jnp.concatenate([jnp.tile(jnp.arange(lp, dtype=jnp.int32), nb),
                           past + jnp.tile(jnp.arange(dec_seq, dtype=jnp.int32), ns)])
    rope = _rope_tables(pos)

    outs = {k: [] for k in ("ka_p", "va_p", "kb_p", "vb_p", "lf_p", "pool_p",
                            "ka_s", "va_s", "kb_s", "vb_s", "lf_s", "pool_s")}
    for l in range(depth):
        if l % 2 == 0:
            i = l // 2
            lam_init = 0.8 - 0.6 * math.exp(-0.3 * l)
            lq = lambda_qk[i]
            lam = (jnp.exp(jnp.sum(lq[0] * lq[1])) - jnp.exp(jnp.sum(lq[2] * lq[3])) + lam_init).reshape(1)
            w_main = w_in[i, :, :D_QKV].astype(BF16)
            w_f = jnp.pad(w_in[i, :, D_QKV:], ((0, 0), (0, LANES - HB))).astype(BF16)
            b_f = jnp.pad(b_forget[i], (0, LANES - HB)).reshape(1, LANES)
            g_sub = g_subln[i].reshape(1, DH)
            p, lf = _inproj(x, g_mix[l].reshape(1, D_MODEL), w_main, w_f, b_f, rope)

            p_prompt = p[:npr].reshape(nb, lp, 6, HA, DH)
            p_sample = p[npr:].reshape(ns, dec_seq, 6, HA, DH)
            for name, c in (("ka", 1), ("va", 2), ("kb", 4), ("vb", 5)):
                outs[name + "_p"].append(p_prompt[:, :seq_len, c])
                outs[name + "_s"].append(p_sample[:, :, c])
            outs["lf_p"].append(lf[:npr].reshape(nb, lp, LANES)[:, :seq_len, :HB])
            outs["lf_s"].append(lf[npr:, :HB].reshape(ns, dec_seq, HB))

            fc, fct = _fcum(lf[:npr], nb, lp)
            fk = fct[:, :HB].reshape(nb, HB, 1, lp)
            oa_p = _flash_a(p, lam, g_sub, nb, lp, lam_init)
            ob_p = _flash_b(p, fc, fk, nb, lp)

            samp = [p_sample[:, :, c].reshape(ns, dec_seq * HA, DH) for c in range(6)]
            lfn = jnp.pad(lf[npr:, :HB].reshape(ns, 1, dec_seq * HB), ((0, 0), (0, 0), (0, LANES - dec_seq * HB)))
            oa_s, ob_s = _sattn(
                page_table, lam, samp[0], samp[3], samp[1], samp[2], samp[4], samp[5], lfn, g_sub,
                cache_k_a[i].reshape(n_phys, PAGE_SIZE * HA, DH), cache_v_a[i].reshape(n_phys, PAGE_SIZE * HA, DH),
                cache_k_b[i].reshape(n_phys, PAGE_SIZE * HB, DH), cache_v_b[i].reshape(n_phys, PAGE_SIZE * HB, DH),
                cache_logf_b[i].reshape(n_phys, SUBLANES, LANES), lam_init)
            wo = w_out[i].astype(BF16)
            x = _outproj(x, oa_p, ob_p, oa_s.reshape(n_samp, HA * DH), ob_s.reshape(n_samp, HB * DH),
                         wo[:HA * DH], wo[HA * DH:])
        else:
            j = l // 2
            g = g_mix[l].reshape(1, D_MODEL)
            wp = w_pool[j].astype(BF16)
            sc = pool_scale[j].reshape(1, D_MODEL)
            ys, st_s = _pool_sample(x[npr:].reshape(ns, dec_seq, D_MODEL), state_pool[j], g, wp, sc)
            y, st_p = _pool_prompt(x, g, wp, sc, nb, lp, seq_len)
            outs["pool_p"].append(st_p[:, 1:])
            outs["pool_s"].append(st_s)
            x = lax.dynamic_update_slice(y, ys.reshape(n_samp, D_MODEL), (npr, 0))

        wr = jnp.concatenate([w_route_group[l], w_route_expert[l].reshape(D_MODEL, N_EXPERTS),
                              jnp.zeros((D_MODEL, LANES - N_GROUPS - N_EXPERTS), F32)], axis=1)
        br = jnp.concatenate([b_route_group[l], b_route_expert[l].reshape(N_EXPERTS),
                              jnp.zeros((LANES - N_GROUPS - N_EXPERTS,), F32)]).reshape(1, LANES)
        x = _moe(x, valid, g_ffn[l].reshape(1, D_MODEL), wr, br, w_gate, w_up, w_down, l,
                 g_final.reshape(1, D_MODEL), l == depth - 1)
    y_prompt = x[:npr].reshape(nb, lp, D_MODEL)[:, N_META:seq_len]
    y_sample = x[npr:].reshape(ns, dec_seq, D_MODEL)
    st = lambda k: jnp.stack(outs[k])
    return (y_prompt, y_sample, st("ka_p"), st("va_p"), st("kb_p"), st("vb_p"), st("lf_p"), st("pool_p"),
            st("ka_s"), st("va_s"), st("kb_s"), st("vb_s"), st("lf_s"), st("pool_s"))
```

```python
import functools
import math

import jax
import jax.numpy as jnp
import numpy as np
from jax import lax
from jax.experimental import pallas as pl
from jax.experimental.pallas import tpu as pltpu

F32 = jnp.float32
BF16 = jnp.bfloat16

D_MODEL = 2048
N_META = 16
HA = 8
HB = 8
DH = 128
DQK_A = 64
ROT_DIM = 16
ROPE_THETA = 500000.0
POOL_WINDOWS = (2, 4, 8, 16)
POOL_CH = D_MODEL // len(POOL_WINDOWS)
POOL_BUF = max(POOL_WINDOWS) - 1
N_GROUPS = 4
EXPERTS_PER_GROUP = 8
N_EXPERTS = N_GROUPS * EXPERTS_PER_GROUP
D_EXPERT = D_MODEL // 4
EPS = 1e-6
NEG_INF = -1e30
PAGE_SIZE = 128
D_QKV = 6 * HA * DH

LANES = 128
SUBLANES = 8
VMEM_LIMIT = 56 * 1024 * 1024

TQ = 256
TM_PROJ = 512
TN_PROJ = 1024
TM_ROUTE = 256
TM_EXPERT = 256
TM_COMBINE = 256
TP_POOL = 256
PAGES_PER_STEP = 8
DMA_UNROLL = 8


def _rms(x, g):
    return x * lax.rsqrt(jnp.mean(x * x, axis=-1, keepdims=True) + EPS) * g


def _dot(a, b):
    return jnp.dot(a, b, preferred_element_type=F32)


def _dot_nt(a, b):
    return lax.dot_general(a, b, (((1,), (1,)), ((), ())), preferred_element_type=F32)


def _inproj_body(x_ref, g_ref, w_ref, wf_ref, bf_ref, rope_ref, p_ref, lf_ref, h_scr, *, n_rope_tiles, tn):
    j = pl.program_id(1)

    @pl.when(j == 0)
    def _():
        hb = _rms(x_ref[...], g_ref[...]).astype(BF16)
        h_scr[...] = hb
        fl = _dot(hb, wf_ref[...]) + bf_ref[...]
        lf_ref[...] = jnp.minimum(fl, 0.0) - jnp.log1p(jnp.exp(-jnp.abs(fl)))

    acc = _dot(h_scr[...], w_ref[...])

    @pl.when(j < n_rope_tiles)
    def _():
        c = rope_ref[0]
        s_lo = rope_ref[1]
        s_hi = rope_ref[2]
        for cc in range(tn // LANES):
            a = acc[:, cc * LANES:(cc + 1) * LANES]
            p_ref[:, cc * LANES:(cc + 1) * LANES] = (
                a * c + pltpu.roll(a, ROT_DIM // 2, 1) * s_lo + pltpu.roll(a, LANES - ROT_DIM // 2, 1) * s_hi)

    @pl.when(j >= n_rope_tiles)
    def _():
        p_ref[...] = acc


def _inproj(x, g, w_bf, wf_bf, bf, rope):
    nt = x.shape[0]
    tm, tn = TM_PROJ, TN_PROJ
    n_rope_tiles = (2 * HA * DH) // tn
    return pl.pallas_call(
        functools.partial(_inproj_body, n_rope_tiles=n_rope_tiles, tn=tn),
        grid=(nt // tm, D_QKV // tn),
        in_specs=[
            pl.BlockSpec((tm, D_MODEL), lambda i, j: (i, 0)),
            pl.BlockSpec((1, D_MODEL), lambda i, j: (0, 0)),
            pl.BlockSpec((D_MODEL, tn), lambda i, j: (0, j)),
            pl.BlockSpec((D_MODEL, LANES), lambda i, j: (0, 0)),
            pl.BlockSpec((1, LANES), lambda i, j: (0, 0)),
            pl.BlockSpec((3, tm, LANES), lambda i, j: (0, i, 0)),
        ],
        out_specs=[
            pl.BlockSpec((tm, tn), lambda i, j: (i, j)),
            pl.BlockSpec((tm, LANES), lambda i, j: (i, 0)),
        ],
        out_shape=[jax.ShapeDtypeStruct((nt, D_QKV), F32), jax.ShapeDtypeStruct((nt, LANES), F32)],
        scratch_shapes=[pltpu.VMEM((tm, D_MODEL), BF16)],
        compiler_params=pltpu.CompilerParams(
            dimension_semantics=("arbitrary", "arbitrary"), vmem_limit_bytes=VMEM_LIMIT),
        name="inproj",
    )(x, g, w_bf, wf_bf, bf, rope)


def _fcum_body(lf_ref, fc_ref, fct_ref, *, lp):
    blk = TQ
    row = lax.broadcasted_iota(jnp.int32, (blk, blk), 0)
    col = lax.broadcasted_iota(jnp.int32, (blk, blk), 1)
    tri = jnp.where(row >= col, 1.0, 0.0).astype(BF16)
    carry = jnp.zeros((1, LANES), F32)
    for b in range(lp // blk):
        x = lf_ref[pl.ds(b * blk, blk), :]
        hi = x.astype(BF16)
        r1 = x - hi.astype(F32)
        mid = r1.astype(BF16)
        lo = (r1 - mid.astype(F32)).astype(BF16)
        c = _dot(tri, hi) + _dot(tri, mid) + _dot(tri, lo) + carry
        fc_ref[pl.ds(b * blk, blk), :] = c
        carry = c[blk - 1:blk, :]
    fct_ref[0] = fc_ref[...].T


def _fcum(lf, nb, lp):
    return pl.pallas_call(
        functools.partial(_fcum_body, lp=lp),
        grid=(nb,),
        in_specs=[pl.BlockSpec((lp, LANES), lambda b: (b, 0))],
        out_specs=[pl.BlockSpec((lp, LANES), lambda b: (b, 0)),
                   pl.BlockSpec((1, LANES, lp), lambda b: (b, 0, 0))],
        out_shape=[jax.ShapeDtypeStruct((nb * lp, LANES), F32), jax.ShapeDtypeStruct((nb, LANES, lp), F32)],
        name="fcum",
    )(lf)


def _split_maps(q):
    lane = lax.broadcasted_iota(jnp.int32, q.shape, 1)
    return jnp.concatenate([jnp.where(lane < DQK_A, q, 0.0), jnp.where(lane >= DQK_A, q, 0.0)], axis=0)


def _subln(o, g, lam_init):
    return _rms(o, g) * (1.0 - lam_init)


def _causal_rows(qs, kb, vb, i, bias=None):
    n = (i + 1) * TQ
    s = _dot_nt(qs, kb[:n])
    if bias is not None:
        s = s + bias(n)
    row = lax.broadcasted_iota(jnp.int32, (qs.shape[0], TQ), 0) % TQ
    col = lax.broadcasted_iota(jnp.int32, (qs.shape[0], TQ), 1)
    diag = jnp.where(col <= row, s[:, n - TQ:], NEG_INF)
    s = diag if i == 0 else jnp.concatenate([s[:, :n - TQ], diag], axis=1)
    p = jnp.exp(s - jnp.max(s, axis=1, keepdims=True))
    return _dot(p.astype(BF16), vb[:n]) / jnp.sum(p, axis=1, keepdims=True)


def _flash_a_body(lam_ref, q_ref, k_ref, v_ref, g_ref, o_ref, *, lam_init, nq):
    kb = k_ref[...].astype(BF16)
    vb = v_ref[...].astype(BF16)
    for i in range(nq):
        qs = _split_maps(q_ref[pl.ds(i * TQ, TQ), :] * (DQK_A ** -0.5)).astype(BF16)
        o = _causal_rows(qs, kb, vb, i)
        o = o[:TQ] - lam_ref[0] * o[TQ:]
        o_ref[pl.ds(i * TQ, TQ), :] = _subln(o, g_ref[...], lam_init)


def _flash_b_body(q_ref, k_ref, v_ref, fq_ref, fk_ref, o_ref, *, nq):
    h = pl.program_id(1)
    kb = k_ref[...].astype(BF16)
    vb = v_ref[...].astype(BF16)
    fk = fk_ref[0, 0]
    lane = lax.broadcasted_iota(jnp.int32, (TQ, LANES), 1)
    for i in range(nq):
        qs = (q_ref[pl.ds(i * TQ, TQ), :] * (DH ** -0.5)).astype(BF16)
        fq_col = jnp.sum(jnp.where(lane == h, fq_ref[pl.ds(i * TQ, TQ), :], 0.0), axis=1, keepdims=True)
        o_ref[pl.ds(i * TQ, TQ), :] = _causal_rows(qs, kb, vb, i, bias=lambda n: fq_col - fk[:, :n])


def _flash_a(p, lam, g_sub, nb, lp, lam_init):
    nq = lp // TQ
    qcol, kcol, vcol = 0, HA, 2 * HA
    seq = lambda c: pl.BlockSpec((lp, DH), lambda b, h: (b, c + h))
    return pl.pallas_call(
        functools.partial(_flash_a_body, lam_init=lam_init, nq=nq),
        grid=(nb, HA),
        in_specs=[pl.BlockSpec(memory_space=pltpu.SMEM), seq(qcol), seq(kcol), seq(vcol),
                  pl.BlockSpec((1, DH), lambda b, h: (0, 0))],
        out_specs=pl.BlockSpec((lp, DH), lambda b, h: (b, h)),
        out_shape=jax.ShapeDtypeStruct((nb * lp, HA * DH), F32),
        compiler_params=pltpu.CompilerParams(
            dimension_semantics=("arbitrary", "arbitrary"), vmem_limit_bytes=VMEM_LIMIT),
        name="flash_a",
    )(lam, p, p, p, g_sub)


def _flash_b(p, fc, fk, nb, lp):
    nq = lp // TQ
    qcol, kcol, vcol = 3 * HA, 4 * HA, 5 * HA
    seq = lambda c: pl.BlockSpec((lp, DH), lambda b, h: (b, c + h))
    return pl.pallas_call(
        functools.partial(_flash_b_body, nq=nq),
        grid=(nb, HB),
        in_specs=[seq(qcol), seq(kcol), seq(vcol),
                  pl.BlockSpec((lp, LANES), lambda b, h: (b, 0)),
                  pl.BlockSpec((1, 1, 1, lp), lambda b, h: (b, h, 0, 0))],
        out_specs=pl.BlockSpec((lp, DH), lambda b, h: (b, h)),
        out_shape=jax.ShapeDtypeStruct((nb * lp, HB * DH), F32),
        compiler_params=pltpu.CompilerParams(
            dimension_semantics=("arbitrary", "arbitrary"), vmem_limit_bytes=VMEM_LIMIT),
        name="flash_b",
    )(p, p, p, fc, fk)


def _online_update(s, pv, m_ref, l_ref, acc_ref):
    m_prev = m_ref[...]
    m_new = jnp.maximum(m_prev, jnp.max(s, axis=1, keepdims=True))
    alpha = jnp.exp(m_prev - m_new)
    p = jnp.exp(s - m_new)
    l_ref[...] = alpha * l_ref[...] + jnp.sum(p, axis=1, keepdims=True)
    acc_ref[...] = alpha * acc_ref[...] + pv(p.astype(BF16))
    m_ref[...] = m_new


def _page_suffix_bias(x, carry):
    lane = lax.broadcasted_iota(jnp.int32, x.shape, 1)
    row = lax.broadcasted_iota(jnp.int32, x.shape, 0)
    y = x
    t = x
    for s in (8, 16, 32, 64):
        y = y + jnp.where(lane + s < LANES, pltpu.roll(y, LANES - s, 1), 0.0)
        t = t + pltpu.roll(t, s, 1)
    z = t
    for s in (1, 2, 4):
        z = z + jnp.where(row + s < SUBLANES, pltpu.roll(z, SUBLANES - s, 0), 0.0)
    g = (y - x) + (z - t) + carry
    return g, carry + z[0:1, :]


def _sattn_body(*refs, n_steps, dec_seq, lam_init):
    pps = PAGES_PER_STEP
    (_, lam_ref, qa_ref, qb_ref, kan_ref, van_ref, kbn_ref, vbn_ref, lfn_ref, g_ref), refs = refs[:10], refs[10:]
    ka_refs, va_refs, kb_refs, vb_refs, lf_refs = (refs[j * pps:(j + 1) * pps] for j in range(5))
    oa_ref, ob_ref, qa_s, qb_s, ma, la, acca, mb, lb, accb, carry = refs[5 * pps:]
    pstep = pl.program_id(1)
    nrow = dec_seq * HB
    ncol = PAGE_SIZE * HB

    @pl.when(pstep == 0)
    def _():
        qa_s[...] = _split_maps(qa_ref[0] * (DQK_A ** -0.5)).astype(BF16)
        qb_s[...] = (qb_ref[0] * (DH ** -0.5)).astype(BF16)
        for m_ref, l_ref, acc_ref in ((ma, la, acca), (mb, lb, accb)):
            m_ref[...] = jnp.full(m_ref.shape, NEG_INF, F32)
            l_ref[...] = jnp.zeros(l_ref.shape, F32)
            acc_ref[...] = jnp.zeros(acc_ref.shape, F32)
        carry[...] = jnp.zeros(carry.shape, F32)

    def head_match(nr):
        r = lax.broadcasted_iota(jnp.int32, (nr, ncol), 0)
        c = lax.broadcasted_iota(jnp.int32, (nr, ncol), 1)
        return (r % HB) == (c % HB)

    def paged(q_s, k_refs, v_refs, biases, m_ref, l_ref, acc_ref):
        hm = head_match(q_s.shape[0])
        cols = []
        for j in range(pps):
            s = _dot_nt(q_s[...], k_refs[j][0].astype(BF16))
            if biases is not None:
                s = s + biases[j]
            cols.append(jnp.where(hm, s, NEG_INF))

        def pv(p):
            out = _dot(p[:, :ncol], v_refs[0][0].astype(BF16))
            for j in range(1, pps):
                out = out + _dot(p[:, j * ncol:(j + 1) * ncol], v_refs[j][0].astype(BF16))
            return out

        _online_update(jnp.concatenate(cols, axis=1), pv, m_ref, l_ref, acc_ref)

    paged(qa_s, ka_refs, va_refs, None, ma, la, acca)

    biases = []
    c = carry[...]
    for j in range(pps):
        g, c = _page_suffix_bias(lf_refs[j][0], c)
        biases.append(jnp.concatenate([g[r:r + 1, :] for r in range(SUBLANES)], axis=1))
    carry[...] = c
    paged(qb_s, kb_refs, vb_refs, biases, mb, lb, accb)

    @pl.when(pstep == n_steps - 1)
    def _():
        def new_mask(shape):
            r = lax.broadcasted_iota(jnp.int32, shape, 0)
            c = lax.broadcasted_iota(jnp.int32, shape, 1)
            return ((r % HB) == (c % HB)) & ((c // HB) <= ((r // HB) % dec_seq))

        s = _dot_nt(qa_s[...], kan_ref[0].astype(BF16))
        s = jnp.where(new_mask(s.shape), s, NEG_INF)
        _online_update(s, lambda p: _dot(p, van_ref[0].astype(BF16)), ma, la, acca)

        x = lfn_ref[0]
        cs = x
        for t in range(1, dec_seq):
            cs = cs + pltpu.roll(x, t * HB, 1)
        s = _dot_nt(qb_s[...], kbn_ref[0].astype(BF16)) - cs[:, :nrow]
        s = jnp.where(new_mask(s.shape), s, NEG_INF)
        _online_update(s, lambda p: _dot(p, vbn_ref[0].astype(BF16)), mb, lb, accb)

        o = acca[...] / la[...]
        o = o[:nrow] - lam_ref[0] * o[nrow:]
        oa_ref[0] = _subln(o, g_ref[...], lam_init)
        ob_ref[0] = accb[...] / lb[...]


def _sattn(page_table, lam, qa, qb, kan, van, kbn, vbn, lfn, g_sub, ck_a, cv_a, ck_b, cv_b, c_lf, lam_init):
    ns, n_pages = page_table.shape
    pps = PAGES_PER_STEP
    n_steps = n_pages // pps
    nrow = qa.shape[1]
    dec_seq = nrow // HB
    ncol = PAGE_SIZE * HB
    pt_flat = page_table.reshape(-1)

    def seq_map(b, p, pt):
        return (b, 0, 0)

    def page_map(j):
        return lambda b, p, pt: (pt[b * n_pages + (n_pages - 1 - (p * pps + j))], 0, 0)

    seq_spec = pl.BlockSpec((1, nrow, DH), seq_map)
    page_specs = [pl.BlockSpec((1, ncol, DH), page_map(j)) for j in range(pps)]
    lf_specs = [pl.BlockSpec((1, SUBLANES, LANES), page_map(j)) for j in range(pps)]
    return pl.pallas_call(
        functools.partial(_sattn_body, n_steps=n_steps, dec_seq=dec_seq, lam_init=lam_init),
        grid_spec=pltpu.PrefetchScalarGridSpec(
            num_scalar_prefetch=1,
            grid=(ns, n_steps),
            in_specs=[
                pl.BlockSpec(memory_space=pltpu.SMEM),
                seq_spec, seq_spec, seq_spec, seq_spec, seq_spec, seq_spec,
                pl.BlockSpec((1, 1, LANES), seq_map),
                pl.BlockSpec((1, DH), lambda b, p, pt: (0, 0)),
            ] + page_specs * 4 + lf_specs,
            out_specs=[seq_spec, seq_spec],
            scratch_shapes=[
                pltpu.VMEM((2 * nrow, DH), BF16), pltpu.VMEM((nrow, DH), BF16),
                pltpu.VMEM((2 * nrow, 1), F32), pltpu.VMEM((2 * nrow, 1), F32), pltpu.VMEM((2 * nrow, DH), F32),
                pltpu.VMEM((nrow, 1), F32), pltpu.VMEM((nrow, 1), F32), pltpu.VMEM((nrow, DH), F32),
                pltpu.VMEM((1, LANES), F32),
            ],
        ),
        out_shape=[jax.ShapeDtypeStruct((ns, nrow, DH), F32), jax.ShapeDtypeStruct((ns, nrow, DH), F32)],
        compiler_params=pltpu.CompilerParams(
            dimension_semantics=("arbitrary", "arbitrary"), vmem_limit_bytes=VMEM_LIMIT),
        name="sattn",
    )(pt_flat, lam, qa, qb, kan, van, kbn, vbn, lfn, g_sub,
      *([ck_a] * pps), *([cv_a] * pps), *([ck_b] * pps), *([cv_b] * pps), *([c_lf] * pps))


def _outproj_body(x_ref, oap_ref, obp_ref, oas_ref, obs_ref, wa_ref, wb_ref, y_ref, *, n_prompt_tiles):
    is_sample = pl.program_id(0) >= n_prompt_tiles
    oa = jnp.where(is_sample, oas_ref[...], oap_ref[...]).astype(BF16)
    ob = jnp.where(is_sample, obs_ref[...], obp_ref[...]).astype(BF16)
    y_ref[...] = x_ref[...] + _dot(oa, wa_ref[...]) + _dot(ob, wb_ref[...])


def _outproj(x, oa_p, ob_p, oa_s, ob_s, wa_bf, wb_bf):
    nt = x.shape[0]
    tm, tn = TM_PROJ, 1024
    ka, kb = oa_p.shape[1], ob_p.shape[1]
    npt = oa_p.shape[0] // tm
    nst = oa_s.shape[0] // tm
    assert npt * tm == oa_p.shape[0] and nst * tm == oa_s.shape[0] and (npt + nst) * tm == nt
    p_map = lambda i, j: (jnp.minimum(i, npt - 1), 0)
    s_map = lambda i, j: (jnp.maximum(i - npt, 0), 0)
    return pl.pallas_call(
        functools.partial(_outproj_body, n_prompt_tiles=npt),
        grid=(nt // tm, D_MODEL // tn),
        in_specs=[
            pl.BlockSpec((tm, tn), lambda i, j: (i, j)),
            pl.BlockSpec((tm, ka), p_map),
            pl.BlockSpec((tm, kb), p_map),
            pl.BlockSpec((tm, ka), s_map),
            pl.BlockSpec((tm, kb), s_map),
            pl.BlockSpec((ka, tn), lambda i, j: (0, j)),
            pl.BlockSpec((kb, tn), lambda i, j: (0, j)),
        ],
        out_specs=pl.BlockSpec((tm, tn), lambda i, j: (i, j)),
        out_shape=jax.ShapeDtypeStruct((nt, D_MODEL), F32),
        compiler_params=pltpu.CompilerParams(
            dimension_semantics=("arbitrary", "arbitrary"), vmem_limit_bytes=VMEM_LIMIT),
        name="outproj",
    )(x, oa_p, ob_p, oa_s, ob_s, wa_bf, wb_bf)


def _route_body(x_ref, g_ref, wr_ref, br_ref, h_ref, r_ref):
    h = _rms(x_ref[...], g_ref[...])
    h_ref[...] = h
    lg = jnp.dot(h, wr_ref[...], precision=lax.Precision.HIGHEST, preferred_element_type=F32) + br_ref[...]
    lane = lax.broadcasted_iota(jnp.int32, lg.shape, 1).astype(F32)
    big = float(LANES)

    def first_max(vals, valid):
        v = jnp.where(valid, vals, NEG_INF)
        top = jnp.max(v, axis=1, keepdims=True)
        idx = jnp.min(jnp.where(valid & (v == top), lane, big), axis=1, keepdims=True)
        return top, idx

    is_grp = lane < N_GROUPS
    gmax, grp = first_max(lg, is_grp)
    p_grp = 1.0 / jnp.sum(jnp.where(is_grp, jnp.exp(lg - gmax), 0.0), axis=1, keepdims=True)
    lo = N_GROUPS + EXPERTS_PER_GROUP * grp
    in_grp = (lane >= lo) & (lane < lo + EXPERTS_PER_GROUP)
    v0, i0 = first_max(lg, in_grp)
    v1, i1 = first_max(lg, in_grp & (lane != i0))
    e = jnp.exp(v1 - v0)
    w0 = p_grp / (1.0 + e)
    w1 = p_grp * e / (1.0 + e)
    r = jnp.where(lane == 0, i0 - N_GROUPS, 0.0)
    r = jnp.where(lane == 1, i1 - N_GROUPS, r)
    r = jnp.where(lane == 2, w0, r)
    r = jnp.where(lane == 3, w1, r)
    r_ref[...] = r


def _route(x, g, wr, br):
    nt = x.shape[0]
    tm = TM_ROUTE
    return pl.pallas_call(
        _route_body,
        grid=(nt // tm,),
        in_specs=[
            pl.BlockSpec((tm, D_MODEL), lambda i: (i, 0)),
            pl.BlockSpec((1, D_MODEL), lambda i: (0, 0)),
            pl.BlockSpec((D_MODEL, LANES), lambda i: (0, 0)),
            pl.BlockSpec((1, LANES), lambda i: (0, 0)),
        ],
        out_specs=[pl.BlockSpec((tm, D_MODEL), lambda i: (i, 0)), pl.BlockSpec((tm, LANES), lambda i: (i, 0))],
        out_shape=[jax.ShapeDtypeStruct((nt, D_MODEL), F32), jax.ShapeDtypeStruct((nt, LANES), F32)],
        compiler_params=pltpu.CompilerParams(dimension_semantics=("arbitrary",), vmem_limit_bytes=VMEM_LIMIT),
        name="route",
    )(x, g, wr, br)


def _start_row_gather(idx_ref, idx_base, src_ref, dst_ref, sem, n_rows, straight_line=False):
    def copy(r):
        pltpu.make_async_copy(src_ref.at[pl.ds(idx_ref[idx_base + r], 1), :],
                              dst_ref.at[pl.ds(r, 1), :], sem).start()

    if straight_line:
        for r in range(n_rows):
            copy(r)
        return

    def issue(c, carry):
        for u in range(DMA_UNROLL):
            copy(c * DMA_UNROLL + u)
        return carry

    lax.fori_loop(0, n_rows // DMA_UNROLL, issue, 0)


def _wait_row_gather(src_ref, dst_ref, sem, n_rows):
    pltpu.make_async_copy(src_ref.at[pl.ds(0, n_rows), :], dst_ref, sem).wait()


def _expert_body(te_ref, ne_ref, nu_ref, src_ref, h_ref, wg_ref, wu_ref, wd_ref, y_ref, hbuf0, hbuf1, sems,
                 wg_f, wu_f, wd_f, wsem, wg_s, wu_s, wd_s, *, n_tiles, layer):
    tm = TM_EXPERT
    t = pl.program_id(0)
    n_used = nu_ref[0]
    bufs = (hbuf0, hbuf1)
    weights = ((wg_ref, wg_f, wg_s), (wu_ref, wu_f, wu_s), (wd_ref, wd_f, wd_s))

    def weight_copies(e):
        return [pltpu.make_async_copy(w_ref.at[layer, e], w_f, wsem) for w_ref, w_f, _ in weights]

    @pl.when(t == 0)
    def _():
        _start_row_gather(src_ref, 0, h_ref, hbuf0, sems.at[0], tm)
        for c in weight_copies(te_ref[0]):
            c.start()

    @pl.when(t < n_used)
    def _():
        e = te_ref[t]
        prev = te_ref[jnp.maximum(t - 1, 0)]

        @pl.when((t == 0) | (e != prev))
        def _():
            for c in weight_copies(e):
                c.wait()
            for _, w_f, w_s in weights:
                w_s[...] = w_f[...].astype(BF16)
            nxt_e = ne_ref[t]

            @pl.when(nxt_e < N_EXPERTS)
            def _():
                for c in weight_copies(nxt_e):
                    c.start()

    nxt = jnp.minimum(t + 1, n_tiles - 1)
    for par in range(2):
        @pl.when((t < n_used) & (t % 2 == par))
        def _():
            _wait_row_gather(h_ref, bufs[par], sems.at[par], tm)
            hb = bufs[par][...].astype(BF16)
            _start_row_gather(src_ref, nxt * tm, h_ref, bufs[1 - par], sems.at[1 - par], tm, straight_line=True)
            hg = _dot(hb, wg_s[...])
            hu = _dot(hb, wu_s[...])
            act = hg * jax.nn.sigmoid(hg) * hu
            y_ref[...] = _dot(act.astype(BF16), wd_s[...])

        @pl.when((t == n_used) & (t % 2 == par))
        def _():
            _wait_row_gather(h_ref, bufs[par], sems.at[par], tm)

        @pl.when((t == n_tiles - 1) & (t < n_used) & (t % 2 == par))
        def _():
            _wait_row_gather(h_ref, bufs[1 - par], sems.at[1 - par], tm)

    @pl.when(t >= n_used)
    def _():
        y_ref[...] = jnp.zeros(y_ref.shape, F32)


def _experts(tile_expert, next_expert, n_used, src_token, h, w_gate, w_up, w_down, layer):
    tm = TM_EXPERT
    n_slots = src_token.shape[0]
    any_spec = pl.BlockSpec(memory_space=pl.ANY)
    return pl.pallas_call(
        functools.partial(_expert_body, n_tiles=n_slots // tm, layer=layer),
        grid_spec=pltpu.PrefetchScalarGridSpec(
            num_scalar_prefetch=4,
            grid=(n_slots // tm,),
            in_specs=[any_spec, any_spec, any_spec, any_spec],
            out_specs=pl.BlockSpec((tm, D_MODEL), lambda t, te, ne, nu, src: (t, 0)),
            scratch_shapes=[pltpu.VMEM((tm, D_MODEL), F32), pltpu.VMEM((tm, D_MODEL), F32),
                            pltpu.SemaphoreType.DMA((2,)),
                            pltpu.VMEM((D_MODEL, D_EXPERT), F32), pltpu.VMEM((D_MODEL, D_EXPERT), F32),
                            pltpu.VMEM((D_EXPERT, D_MODEL), F32), pltpu.SemaphoreType.DMA(()),
                            pltpu.VMEM((D_MODEL, D_EXPERT), BF16), pltpu.VMEM((D_MODEL, D_EXPERT), BF16),
                            pltpu.VMEM((D_EXPERT, D_MODEL), BF16)],
        ),
        out_shape=jax.ShapeDtypeStruct((n_slots, D_MODEL), F32),
        compiler_params=pltpu.CompilerParams(dimension_semantics=("arbitrary",), vmem_limit_bytes=VMEM_LIMIT),
        name="experts",
    )(tile_expert, next_expert, n_used, src_token, h, w_gate, w_up, w_down)


def _combine_body(slot_ref, x_ref, r_ref, y_ref, g_ref, *rest, nt, n_steps, row_start, n_first, final_norm):
    tm = TM_COMBINE
    t = pl.program_id(0)
    o_refs, (b00, b01, b10, b11, sems) = rest[:-5], rest[-5:]
    bufs = ((b00, b01), (b10, b11))

    def start(step, par, straight_line):
        for k in range(2):
            _start_row_gather(slot_ref, k * nt + row_start(step), y_ref, bufs[par][k], sems.at[par], tm,
                              straight_line)

    def wait(par):
        for k in range(2):
            _wait_row_gather(y_ref, bufs[par][k], sems.at[par], tm)

    @pl.when(t == 0)
    def _():
        start(0, 0, False)

    nxt = jnp.minimum(t + 1, n_steps - 1)
    for par in range(2):
        @pl.when(t % 2 == par)
        def _():
            wait(par)
            start(nxt, 1 - par, True)
            r = r_ref[...]
            out = x_ref[...] + r[:, 2:3] * bufs[par][0][...] + r[:, 3:4] * bufs[par][1][...]
            if final_norm:
                out = _rms(out, g_ref[...])
            if len(o_refs) == 1:
                o_refs[0][...] = out
            else:
                @pl.when(t < n_first)
                def _():
                    o_refs[0][...] = out

                @pl.when(t >= n_first)
                def _():
                    o_refs[1][...] = out

        @pl.when((t == n_steps - 1) & (t % 2 == par))
        def _():
            wait(1 - par)


def _combine(slots, x, route, y, g, out_rows=None):
    nt = x.shape[0]
    tm = TM_COMBINE
    if out_rows is None:
        n_steps = n_first = nt // tm
        row_start = lambda t: t * tm
        out_specs = [pl.BlockSpec((tm, D_MODEL), lambda i, sl: (i, 0))]
        out_shape = [jax.ShapeDtypeStruct((nt, D_MODEL), F32)]
    else:
        nb, lp, first, count, npr = out_rows
        per = count // tm
        n_first = nb * per
        n_second = (nt - npr) // tm
        assert per * tm == count and n_second * tm == nt - npr and first % SUBLANES == 0
        n_steps = n_first + n_second
        row_start = lambda t: jnp.where(t < n_first, (t // per) * lp + first + (t % per) * tm,
                                        npr + (t - n_first) * tm)
        out_specs = [pl.BlockSpec((tm, D_MODEL), lambda i, sl: (jnp.minimum(i, n_first - 1), 0)),
                     pl.BlockSpec((tm, D_MODEL), lambda i, sl: (jnp.maximum(i - n_first, 0), 0))]
        out_shape = [jax.ShapeDtypeStruct((nb * count, D_MODEL), F32),
                     jax.ShapeDtypeStruct((nt - npr, D_MODEL), F32)]
    res = pl.pallas_call(
        functools.partial(_combine_body, nt=nt, n_steps=n_steps, row_start=row_start, n_first=n_first,
                          final_norm=out_rows is not None),
        grid_spec=pltpu.PrefetchScalarGridSpec(
            num_scalar_prefetch=1,
            grid=(n_steps,),
            in_specs=[
                pl.BlockSpec((pl.Element(tm), pl.Element(D_MODEL)),
                             lambda i, sl: (pl.multiple_of(row_start(i), SUBLANES), 0)),
                pl.BlockSpec((pl.Element(tm), pl.Element(LANES)),
                             lambda i, sl: (pl.multiple_of(row_start(i), SUBLANES), 0)),
                pl.BlockSpec(memory_space=pl.ANY),
                pl.BlockSpec((1, D_MODEL), lambda i, sl: (0, 0)),
            ],
            out_specs=out_specs,
            scratch_shapes=[pltpu.VMEM((tm, D_MODEL), F32)] * 4 + [pltpu.SemaphoreType.DMA((2,))],
        ),
        out_shape=out_shape,
        compiler_params=pltpu.CompilerParams(dimension_semantics=("arbitrary",), vmem_limit_bytes=VMEM_LIMIT),
        name="combine",
    )(slots, x, route, y, g)
    return res[0] if out_rows is None else res


def _dispatch_plan(route, valid):
    nt = route.shape[0]
    tm = TM_EXPERT
    n_valid = int(np.sum(valid))
    n_slots = -(-(2 * n_valid + N_EXPERTS * (tm - 1)) // tm) * tm
    n_tiles = n_slots // tm
    valid2 = jnp.asarray(np.concatenate([valid, valid]))
    eidx = jnp.concatenate([route[:, 0], route[:, 1]]).astype(jnp.int32)
    eidx = jnp.where(valid2, eidx, N_EXPERTS)
    onehot = (eidx[:, None] == jnp.arange(N_EXPERTS, dtype=jnp.int32)[None, :]).astype(jnp.int32)
    incl = jnp.cumsum(onehot, axis=0)
    counts = incl[-1]
    rank = jnp.sum((incl - 1) * onehot, axis=1)
    padded = ((counts + tm - 1) // tm) * tm
    ends = jnp.cumsum(padded)
    starts = ends - padded
    slot = jnp.where(valid2, jnp.take(starts, jnp.minimum(eidx, N_EXPERTS - 1)) + rank, 0).astype(jnp.int32)
    token = jnp.tile(jnp.arange(nt, dtype=jnp.int32), 2)
    dest = jnp.where(valid2, slot, n_slots)
    src_token = jnp.zeros((n_slots,), jnp.int32).at[dest].set(token, mode="drop")
    n_used = (ends[-1] // tm).astype(jnp.int32)
    tile_start = jnp.arange(n_tiles, dtype=jnp.int32) * tm
    tile_expert = jnp.sum((tile_start[:, None] >= ends[None, :]).astype(jnp.int32), axis=1)
    last_expert = jnp.take(tile_expert, jnp.maximum(n_used - 1, 0))
    tile_expert = jnp.where(jnp.arange(n_tiles) < n_used, tile_expert, last_expert)
    tile_expert = jnp.minimum(tile_expert, N_EXPERTS - 1).astype(jnp.int32)
    ids = jnp.arange(N_EXPERTS, dtype=jnp.int32)
    later = (ids[None, :] > ids[:, None]) & (counts[None, :] > 0)
    next_nonempty = jnp.min(jnp.where(later, ids[None, :], N_EXPERTS), axis=1)
    next_expert = jnp.take(next_nonempty, tile_expert).astype(jnp.int32)
    return slot, src_token, tile_expert, next_expert, n_used.reshape(1)


def _moe(x, valid, g_ffn, wr, br, w_gate, w_up, w_down, layer, g_final, out_rows):
    h, route = _route(x, g_ffn, wr, br)
    slot, src_token, tile_expert, next_expert, n_used = _dispatch_plan(route, valid)
    y = _experts(tile_expert, next_expert, n_used, src_token, h, w_gate, w_up, w_down, layer)
    return _combine(slot, x, route, y, g_final, out_rows)


def _pool_windows(ext_rows, u, inv_cnt, w_ref, scale):
    outs = []
    for gi, w in enumerate(POOL_WINDOWS):
        c0 = gi * POOL_CH
        win = u[:, c0:c0 + POOL_CH]
        for j in range(1, w):
            win = win + ext_rows(j, c0)
        d = win * inv_cnt[gi] - u[:, c0:c0 + POOL_CH]
        outs.append(_dot(d.astype(BF16), w_ref[gi]))
    return jnp.concatenate(outs, axis=1) * scale


def _pool_prompt_body(xc_ref, xp_ref, g_ref, w_ref, sc_ref, y_ref, st_ref, ext, *, seq_len, nblk, n_prompt_blocks):
    step = pl.program_id(0)
    i = step % nblk
    tp = TP_POOL
    pad = POOL_BUF + 1
    last = seq_len - 1

    @pl.when(step < n_prompt_blocks)
    def _():
        x = xc_ref[...]
        u = _rms(x, g_ref[...])
        tail = _rms(xp_ref[pl.ds(tp - pad, pad), :], g_ref[...])
        ext[pl.ds(0, pad), :] = jnp.where(i > 0, tail, 0.0)
        ext[pl.ds(pad, tp), :] = u
        pos = i * tp + lax.broadcasted_iota(jnp.int32, (tp, 1), 0)
        inv_cnt = [1.0 / jnp.minimum(w, pos + 1).astype(F32) for w in POOL_WINDOWS]

        def ext_rows(j, c0):
            return ext[pl.ds(pad - j, tp), c0:c0 + POOL_CH]

        y_ref[...] = x + _pool_windows(ext_rows, u, inv_cnt, w_ref, sc_ref[...])

        @pl.when(i == last // tp)
        def _():
            st_ref[0] = ext[pl.ds(last % tp + 1, pad), :]

    @pl.when(step >= n_prompt_blocks)
    def _():
        y_ref[...] = xc_ref[...]


def _pool_prompt(x, g, w_bf, scale, nb, lp, seq_len):
    tp = TP_POOL
    nt = x.shape[0]
    nblk = lp // tp
    npb = nb * nblk
    pad = POOL_BUF + 1
    assert nt % tp == 0

    def prev_map(s):
        return (jnp.where((s % nblk == 0) | (s >= npb), s, s - 1), 0)

    return pl.pallas_call(
        functools.partial(_pool_prompt_body, seq_len=seq_len, nblk=nblk, n_prompt_blocks=npb),
        grid=(nt // tp,),
        in_specs=[
            pl.BlockSpec((tp, D_MODEL), lambda s: (s, 0)),
            pl.BlockSpec((tp, D_MODEL), prev_map),
            pl.BlockSpec((1, D_MODEL), lambda s: (0, 0)),
            pl.BlockSpec((len(POOL_WINDOWS), POOL_CH, POOL_CH), lambda s: (0, 0, 0)),
            pl.BlockSpec((1, D_MODEL), lambda s: (0, 0)),
        ],
        out_specs=[
            pl.BlockSpec((tp, D_MODEL), lambda s: (s, 0)),
            pl.BlockSpec((1, pad, D_MODEL), lambda s: (jnp.minimum(s // nblk, nb - 1), 0, 0)),
        ],
        out_shape=[jax.ShapeDtypeStruct((nt, D_MODEL), F32), jax.ShapeDtypeStruct((nb, pad, D_MODEL), F32)],
        scratch_shapes=[pltpu.VMEM((tp + pad, D_MODEL), F32)],
        compiler_params=pltpu.CompilerParams(dimension_semantics=("arbitrary",), vmem_limit_bytes=VMEM_LIMIT),
        name="pool_prompt",
    )(x, x, g, w_bf, scale)


def _pool_sample_body(x_ref, st_ref, g_ref, w_ref, sc_ref, y_ref, so_ref, *, dec_seq):
    g = g_ref[...]
    xs = [x_ref[:, t, :] for t in range(dec_seq)]
    us = [_rms(x, g) for x in xs]

    def ext_row(e, c0):
        if e < POOL_BUF:
            return st_ref[:, e, c0:c0 + POOL_CH]
        return us[e - POOL_BUF][:, c0:c0 + POOL_CH]

    for t in range(dec_seq):
        inv_cnt = [1.0 / w for w in POOL_WINDOWS]
        y = _pool_windows(lambda j, c0: ext_row(POOL_BUF + t - j, c0), us[t], inv_cnt, w_ref, sc_ref[...])
        y_ref[:, t, :] = xs[t] + y
    for e in range(POOL_BUF):
        src = e + dec_seq
        so_ref[:, e, :] = st_ref[:, src, :] if src < POOL_BUF else us[src - POOL_BUF]


def _pool_sample(x3, state, g, w_bf, scale):
    ns, dec_seq, _ = x3.shape
    ts = 32
    return pl.pallas_call(
        functools.partial(_pool_sample_body, dec_seq=dec_seq),
        grid=(ns // ts,),
        in_specs=[
            pl.BlockSpec((ts, dec_seq, D_MODEL), lambda i: (i, 0, 0)),
            pl.BlockSpec((ts, POOL_BUF, D_MODEL), lambda i: (i, 0, 0)),
            pl.BlockSpec((1, D_MODEL), lambda i: (0, 0)),
            pl.BlockSpec((len(POOL_WINDOWS), POOL_CH, POOL_CH), lambda i: (0, 0, 0)),
            pl.BlockSpec((1, D_MODEL), lambda i: (0, 0)),
        ],
        out_specs=[
            pl.BlockSpec((ts, dec_seq, D_MODEL), lambda i: (i, 0, 0)),
            pl.BlockSpec((ts, POOL_BUF, D_MODEL), lambda i: (i, 0, 0)),
        ],
        out_shape=[jax.ShapeDtypeStruct(x3.shape, F32), jax.ShapeDtypeStruct(state.shape, F32)],
        compiler_params=pltpu.CompilerParams(dimension_semantics=("arbitrary",), vmem_limit_bytes=VMEM_LIMIT),
        name="pool_sample",
    )(x3, state, g, w_bf, scale)


def _rope_tables(pos):
    half = ROT_DIM // 2
    inv = ROPE_THETA ** (-jnp.arange(0, ROT_DIM, 2, dtype=F32) / ROT_DIM)
    ang = pos.astype(F32)[:, None] * inv[None, :]
    cos, sin = jnp.cos(ang), jnp.sin(ang)
    n = pos.shape[0]
    ones = jnp.ones((n, DQK_A - ROT_DIM), F32)
    zeros = jnp.zeros((n, DQK_A - ROT_DIM), F32)
    zh = jnp.zeros((n, half), F32)
    c = jnp.concatenate([cos, cos, ones], axis=1)
    s_lo = jnp.concatenate([zh, sin, zeros], axis=1)
    s_hi = jnp.concatenate([-sin, zh, zeros], axis=1)
    return jnp.stack([jnp.tile(t, (1, LANES // DQK_A)) for t in (c, s_lo, s_hi)])


def kernel(x_prompt, x_sample, cache_k_a, cache_v_a, cache_k_b, cache_v_b, cache_logf_b, state_pool, page_table,
           meta_tokens, g_mix, w_in, b_forget, lambda_qk, g_subln, w_out, w_pool, pool_scale, g_ffn,
           w_route_group, b_route_group, w_route_expert, b_route_expert, w_gate, w_up, w_down, g_final):
    nb, seq, _ = x_prompt.shape
    ns, dec_seq, _ = x_sample.shape
    depth = g_mix.shape[0]
    seq_len = seq + N_META
    lp = -(-seq_len // TQ) * TQ
    npr = nb * lp
    n_samp = ns * dec_seq
    nt = npr + n_samp
    past = page_table.shape[1] * PAGE_SIZE
    n_phys = cache_k_a.shape[1]
    assert nt % TM_PROJ == 0 and lp % TP_POOL == 0 and dec_seq * HB <= LANES
    assert page_table.shape[1] % PAGES_PER_STEP == 0 and PAGE_SIZE * HB == SUBLANES * LANES

    pieces = []
    for b in range(nb):
        pieces += [meta_tokens, x_prompt[b], jnp.zeros((lp - seq_len, D_MODEL), F32)]
    x = jnp.concatenate(pieces + [x_sample.reshape(n_samp, D_MODEL)], axis=0)
    valid = np.concatenate([np.tile(np.arange(lp) < seq_len, nb), np.ones(n_samp, bool)])
    pos = jnp.concatenate([jnp.tile(jnp.arange(lp, dtype=jnp.int32), nb),
                           past + jnp.tile(jnp.arange(dec_seq, dtype=jnp.int32), ns)])
    rope = _rope_tables(pos)

    outs = {k: [] for k in ("ka_p", "va_p", "kb_p", "vb_p", "lf_p", "pool_p",
                            "ka_s", "va_s", "kb_s", "vb_s", "lf_s", "pool_s")}
    for l in range(depth):
        if l % 2 == 0:
            i = l // 2
            lam_init = 0.8 - 0.6 * math.exp(-0.3 * l)
            lq = lambda_qk[i]
            lam = (jnp.exp(jnp.sum(lq[0] * lq[1])) - jnp.exp(jnp.sum(lq[2] * lq[3])) + lam_init).reshape(1)
            w_main = w_in[i, :, :D_QKV].astype(BF16)
            w_f = jnp.pad(w_in[i, :, D_QKV:], ((0, 0), (0, LANES - HB))).astype(BF16)
            b_f = jnp.pad(b_forget[i], (0, LANES - HB)).reshape(1, LANES)
            g_sub = g_subln[i].reshape(1, DH)
            p, lf = _inproj(x, g_mix[l].reshape(1, D_MODEL), w_main, w_f, b_f, rope)

            p_prompt = p[:npr].reshape(nb, lp, D_QKV)
            p_sample = p[npr:].reshape(ns, dec_seq, 6, HA, DH).transpose(2, 0, 1, 3, 4)
            for name, c in (("ka", 1), ("va", 2), ("kb", 4), ("vb", 5)):
                outs[name + "_p"].append(
                    p_prompt[:, :seq_len, c * HA * DH:(c + 1) * HA * DH].reshape(nb, seq_len, HA, DH))
                outs[name + "_s"].append(p_sample[c])
            outs["lf_p"].append(lf[:npr].reshape(nb, lp, LANES)[:, :seq_len, :HB])
            outs["lf_s"].append(lf[npr:, :HB].reshape(ns, dec_seq, HB))

            fc, fct = _fcum(lf[:npr], nb, lp)
            fk = fct[:, :HB].reshape(nb, HB, 1, lp)
            oa_p = _flash_a(p, lam, g_sub, nb, lp, lam_init)
            ob_p = _flash_b(p, fc, fk, nb, lp)

            samp = [p_sample[c].reshape(ns, dec_seq * HA, DH) for c in range(6)]
            lfn = jnp.pad(lf[npr:, :HB].reshape(ns, 1, dec_seq * HB), ((0, 0), (0, 0), (0, LANES - dec_seq * HB)))
            oa_s, ob_s = _sattn(
                page_table, lam, samp[0], samp[3], samp[1], samp[2], samp[4], samp[5], lfn, g_sub,
                cache_k_a[i].reshape(n_phys, PAGE_SIZE * HA, DH), cache_v_a[i].reshape(n_phys, PAGE_SIZE * HA, DH),
                cache_k_b[i].reshape(n_phys, PAGE_SIZE * HB, DH), cache_v_b[i].reshape(n_phys, PAGE_SIZE * HB, DH),
                cache_logf_b[i].reshape(n_phys, SUBLANES, LANES), lam_init)
            wo = w_out[i].astype(BF16)
            x = _outproj(x, oa_p, ob_p, oa_s.reshape(n_samp, HA * DH), ob_s.reshape(n_samp, HB * DH),
                         wo[:HA * DH], wo[HA * DH:])
        else:
            j = l // 2
            g = g_mix[l].reshape(1, D_MODEL)
            wp = w_pool[j].astype(BF16)
            sc = pool_scale[j].reshape(1, D_MODEL)
            ys, st_s = _pool_sample(x[npr:].reshape(ns, dec_seq, D_MODEL), state_pool[j], g, wp, sc)
            y, st_p = _pool_prompt(x, g, wp, sc, nb, lp, seq_len)
            outs["pool_p"].append(st_p[:, 1:])
            outs["pool_s"].append(st_s)
            x = lax.dynamic_update_slice(y, ys.reshape(n_samp, D_MODEL), (npr, 0))

        wr = jnp.concatenate([w_route_group[l], w_route_expert[l].reshape(D_MODEL, N_EXPERTS),
                              jnp.zeros((D_MODEL, LANES - N_GROUPS - N_EXPERTS), F32)], axis=1)
        br = jnp.concatenate([b_route_group[l], b_route_expert[l].reshape(N_EXPERTS),
                              jnp.zeros((LANES - N_GROUPS - N_EXPERTS,), F32)]).reshape(1, LANES)
        out_rows = (nb, lp, N_META, seq, npr) if l == depth - 1 else None
        x = _moe(x, valid, g_ffn[l].reshape(1, D_MODEL), wr, br, w_gate, w_up, w_down, l,
                 g_final.reshape(1, D_MODEL), out_rows)
    y_prompt = x[0].reshape(nb, seq, D_MODEL)
    y_sample = x[1].reshape(ns, dec_seq, D_MODEL)
    st = lambda k: jnp.stack(outs[k])
    return (y_prompt, y_sample, st("ka_p"), st("va_p"), st("kb_p"), st("vb_p"), st("lf_p"), st("pool_p"),
            st("ka_s"), st("va_s"), st("kb_s"), st("vb_s"), st("lf_s"), st("pool_s"))
```

```python
import functools
import math

import jax
import jax.numpy as jnp
import numpy as np
from jax import lax
from jax.experimental import pallas as pl
from jax.experimental.pallas import tpu as pltpu

F32 = jnp.float32
BF16 = jnp.bfloat16

D_MODEL = 2048
N_META = 16
HA = 8
HB = 8
DH = 128
DQK_A = 64
ROT_DIM = 16
ROPE_THETA = 500000.0
POOL_WINDOWS = (2, 4, 8, 16)
POOL_CH = D_MODEL // len(POOL_WINDOWS)
POOL_BUF = max(POOL_WINDOWS) - 1
N_GROUPS = 4
EXPERTS_PER_GROUP = 8
N_EXPERTS = N_GROUPS * EXPERTS_PER_GROUP
D_EXPERT = D_MODEL // 4
EPS = 1e-6
NEG_INF = -1e30
PAGE_SIZE = 128
D_QKV = 6 * HA * DH

LANES = 128
SUBLANES = 8
VMEM_LIMIT = 56 * 1024 * 1024

TQ = 256
TM_PROJ = 512
TN_PROJ = 1024
TM_ROUTE = 256
TM_EXPERT = 256
TM_COMBINE = 256
TP_POOL = 256
PAGES_PER_STEP = 8
DMA_UNROLL = 8


def _rms(x, g):
    return x * lax.rsqrt(jnp.mean(x * x, axis=-1, keepdims=True) + EPS) * g


def _dot(a, b):
    return jnp.dot(a, b, preferred_element_type=F32)


def _dot_nt(a, b):
    return lax.dot_general(a, b, (((1,), (1,)), ((), ())), preferred_element_type=F32)


def _inproj_body(x_ref, g_ref, w_ref, wf_ref, bf_ref, rope_ref, p_ref, lf_ref, h_scr, *, n_rope_tiles, tn):
    j = pl.program_id(1)

    @pl.when(j == 0)
    def _():
        hb = _rms(x_ref[...], g_ref[...]).astype(BF16)
        h_scr[...] = hb
        fl = _dot(hb, wf_ref[...]) + bf_ref[...]
        lf_ref[...] = jnp.minimum(fl, 0.0) - jnp.log1p(jnp.exp(-jnp.abs(fl)))

    acc = _dot(h_scr[...], w_ref[...])

    @pl.when(j < n_rope_tiles)
    def _():
        c = rope_ref[0]
        s_lo = rope_ref[1]
        s_hi = rope_ref[2]
        for cc in range(tn // LANES):
            a = acc[:, cc * LANES:(cc + 1) * LANES]
            p_ref[:, cc * LANES:(cc + 1) * LANES] = (
                a * c + pltpu.roll(a, ROT_DIM // 2, 1) * s_lo + pltpu.roll(a, LANES - ROT_DIM // 2, 1) * s_hi)

    @pl.when(j >= n_rope_tiles)
    def _():
        p_ref[...] = acc


def _inproj(x, g, w_bf, wf_bf, bf, rope):
    nt = x.shape[0]
    tm, tn = TM_PROJ, TN_PROJ
    n_rope_tiles = (2 * HA * DH) // tn
    return pl.pallas_call(
        functools.partial(_inproj_body, n_rope_tiles=n_rope_tiles, tn=tn),
        grid=(nt // tm, D_QKV // tn),
        in_specs=[
            pl.BlockSpec((tm, D_MODEL), lambda i, j: (i, 0)),
            pl.BlockSpec((1, D_MODEL), lambda i, j: (0, 0)),
            pl.BlockSpec((D_MODEL, tn), lambda i, j: (0, j)),
            pl.BlockSpec((D_MODEL, LANES), lambda i, j: (0, 0)),
            pl.BlockSpec((1, LANES), lambda i, j: (0, 0)),
            pl.BlockSpec((3, tm, LANES), lambda i, j: (0, i, 0)),
        ],
        out_specs=[
            pl.BlockSpec((tm, tn), lambda i, j: (i, j)),
            pl.BlockSpec((tm, LANES), lambda i, j: (i, 0)),
        ],
        out_shape=[jax.ShapeDtypeStruct((nt, D_QKV), F32), jax.ShapeDtypeStruct((nt, LANES), F32)],
        scratch_shapes=[pltpu.VMEM((tm, D_MODEL), BF16)],
        compiler_params=pltpu.CompilerParams(
            dimension_semantics=("arbitrary", "arbitrary"), vmem_limit_bytes=VMEM_LIMIT),
        name="inproj",
    )(x, g, w_bf, wf_bf, bf, rope)


def _fcum_body(lf_ref, fc_ref, fct_ref, *, lp):
    blk = TQ
    row = lax.broadcasted_iota(jnp.int32, (blk, blk), 0)
    col = lax.broadcasted_iota(jnp.int32, (blk, blk), 1)
    tri = jnp.where(row >= col, 1.0, 0.0).astype(BF16)
    carry = jnp.zeros((1, LANES), F32)
    for b in range(lp // blk):
        x = lf_ref[pl.ds(b * blk, blk), :]
        hi = x.astype(BF16)
        r1 = x - hi.astype(F32)
        mid = r1.astype(BF16)
        lo = (r1 - mid.astype(F32)).astype(BF16)
        c = _dot(tri, hi) + _dot(tri, mid) + _dot(tri, lo) + carry
        fc_ref[pl.ds(b * blk, blk), :] = c
        carry = c[blk - 1:blk, :]
    fct_ref[0] = fc_ref[...].T


def _fcum(lf, nb, lp):
    return pl.pallas_call(
        functools.partial(_fcum_body, lp=lp),
        grid=(nb,),
        in_specs=[pl.BlockSpec((lp, LANES), lambda b: (b, 0))],
        out_specs=[pl.BlockSpec((lp, LANES), lambda b: (b, 0)),
                   pl.BlockSpec((1, LANES, lp), lambda b: (b, 0, 0))],
        out_shape=[jax.ShapeDtypeStruct((nb * lp, LANES), F32), jax.ShapeDtypeStruct((nb, LANES, lp), F32)],
        name="fcum",
    )(lf)


def _split_maps(q):
    lane = lax.broadcasted_iota(jnp.int32, q.shape, 1)
    return jnp.concatenate([jnp.where(lane < DQK_A, q, 0.0), jnp.where(lane >= DQK_A, q, 0.0)], axis=0)


def _subln(o, g, lam_init):
    return _rms(o, g) * (1.0 - lam_init)


def _causal_rows(qs, kb, vb, i, bias=None):
    n = (i + 1) * TQ
    s = _dot_nt(qs, kb[:n])
    if bias is not None:
        s = s + bias(n)
    row = lax.broadcasted_iota(jnp.int32, (qs.shape[0], TQ), 0) % TQ
    col = lax.broadcasted_iota(jnp.int32, (qs.shape[0], TQ), 1)
    diag = jnp.where(col <= row, s[:, n - TQ:], NEG_INF)
    s = diag if i == 0 else jnp.concatenate([s[:, :n - TQ], diag], axis=1)
    p = jnp.exp(s - jnp.max(s, axis=1, keepdims=True))
    return _dot(p.astype(BF16), vb[:n]) / jnp.sum(p, axis=1, keepdims=True)


def _flash_a_body(lam_ref, q_ref, k_ref, v_ref, g_ref, o_ref, *, lam_init, nq):
    kb = k_ref[...].astype(BF16)
    vb = v_ref[...].astype(BF16)
    for i in range(nq):
        qs = _split_maps(q_ref[pl.ds(i * TQ, TQ), :] * (DQK_A ** -0.5)).astype(BF16)
        o = _causal_rows(qs, kb, vb, i)
        o = o[:TQ] - lam_ref[0] * o[TQ:]
        o_ref[pl.ds(i * TQ, TQ), :] = _subln(o, g_ref[...], lam_init)


def _flash_b_body(q_ref, k_ref, v_ref, fq_ref, fk_ref, o_ref, *, nq):
    h = pl.program_id(1)
    kb = k_ref[...].astype(BF16)
    vb = v_ref[...].astype(BF16)
    fk = fk_ref[0, 0]
    lane = lax.broadcasted_iota(jnp.int32, (TQ, LANES), 1)
    for i in range(nq):
        qs = (q_ref[pl.ds(i * TQ, TQ), :] * (DH ** -0.5)).astype(BF16)
        fq_col = jnp.sum(jnp.where(lane == h, fq_ref[pl.ds(i * TQ, TQ), :], 0.0), axis=1, keepdims=True)
        o_ref[pl.ds(i * TQ, TQ), :] = _causal_rows(qs, kb, vb, i, bias=lambda n: fq_col - fk[:, :n])


def _flash_a(p, lam, g_sub, nb, lp, lam_init):
    nq = lp // TQ
    qcol, kcol, vcol = 0, HA, 2 * HA
    seq = lambda c: pl.BlockSpec((lp, DH), lambda b, h: (b, c + h))
    return pl.pallas_call(
        functools.partial(_flash_a_body, lam_init=lam_init, nq=nq),
        grid=(nb, HA),
        in_specs=[pl.BlockSpec(memory_space=pltpu.SMEM), seq(qcol), seq(kcol), seq(vcol),
                  pl.BlockSpec((1, DH), lambda b, h: (0, 0))],
        out_specs=pl.BlockSpec((lp, DH), lambda b, h: (b, h)),
        out_shape=jax.ShapeDtypeStruct((nb * lp, HA * DH), F32),
        compiler_params=pltpu.CompilerParams(
            dimension_semantics=("arbitrary", "arbitrary"), vmem_limit_bytes=VMEM_LIMIT),
        name="flash_a",
    )(lam, p, p, p, g_sub)


def _flash_b(p, fc, fk, nb, lp):
    nq = lp // TQ
    qcol, kcol, vcol = 3 * HA, 4 * HA, 5 * HA
    seq = lambda c: pl.BlockSpec((lp, DH), lambda b, h: (b, c + h))
    return pl.pallas_call(
        functools.partial(_flash_b_body, nq=nq),
        grid=(nb, HB),
        in_specs=[seq(qcol), seq(kcol), seq(vcol),
                  pl.BlockSpec((lp, LANES), lambda b, h: (b, 0)),
                  pl.BlockSpec((1, 1, 1, lp), lambda b, h: (b, h, 0, 0))],
        out_specs=pl.BlockSpec((lp, DH), lambda b, h: (b, h)),
        out_shape=jax.ShapeDtypeStruct((nb * lp, HB * DH), F32),
        compiler_params=pltpu.CompilerParams(
            dimension_semantics=("arbitrary", "arbitrary"), vmem_limit_bytes=VMEM_LIMIT),
        name="flash_b",
    )(p, p, p, fc, fk)


def _online_update(s, pv, m_ref, l_ref, acc_ref):
    m_prev = m_ref[...]
    m_new = jnp.maximum(m_prev, jnp.max(s, axis=1, keepdims=True))
    alpha = jnp.exp(m_prev - m_new)
    p = jnp.exp(s - m_new)
    l_ref[...] = alpha * l_ref[...] + jnp.sum(p, axis=1, keepdims=True)
    acc_ref[...] = alpha * acc_ref[...] + pv(p.astype(BF16))
    m_ref[...] = m_new


def _page_suffix_bias(x, carry):
    lane = lax.broadcasted_iota(jnp.int32, x.shape, 1)
    row = lax.broadcasted_iota(jnp.int32, x.shape, 0)
    y = x
    t = x
    for s in (8, 16, 32, 64):
        y = y + jnp.where(lane + s < LANES, pltpu.roll(y, LANES - s, 1), 0.0)
        t = t + pltpu.roll(t, s, 1)
    z = t
    for s in (1, 2, 4):
        z = z + jnp.where(row + s < SUBLANES, pltpu.roll(z, SUBLANES - s, 0), 0.0)
    g = (y - x) + (z - t) + carry
    return g, carry + z[0:1, :]


def _sattn_body(*refs, n_steps, dec_seq, lam_init):
    pps = PAGES_PER_STEP
    (_, lam_ref, qa_ref, qb_ref, kan_ref, van_ref, kbn_ref, vbn_ref, lfn_ref, g_ref), refs = refs[:10], refs[10:]
    ka_refs, va_refs, kb_refs, vb_refs, lf_refs = (refs[j * pps:(j + 1) * pps] for j in range(5))
    oa_ref, ob_ref, qa_s, qb_s, ma, la, acca, mb, lb, accb, carry = refs[5 * pps:]
    pstep = pl.program_id(1)
    nrow = dec_seq * HB
    ncol = PAGE_SIZE * HB

    @pl.when(pstep == 0)
    def _():
        qa_s[...] = _split_maps(qa_ref[0] * (DQK_A ** -0.5)).astype(BF16)
        qb_s[...] = (qb_ref[0] * (DH ** -0.5)).astype(BF16)
        for m_ref, l_ref, acc_ref in ((ma, la, acca), (mb, lb, accb)):
            m_ref[...] = jnp.full(m_ref.shape, NEG_INF, F32)
            l_ref[...] = jnp.zeros(l_ref.shape, F32)
            acc_ref[...] = jnp.zeros(acc_ref.shape, F32)
        carry[...] = jnp.zeros(carry.shape, F32)

    def head_match(nr):
        r = lax.broadcasted_iota(jnp.int32, (nr, ncol), 0)
        c = lax.broadcasted_iota(jnp.int32, (nr, ncol), 1)
        return (r % HB) == (c % HB)

    def paged(q_s, k_refs, v_refs, biases, m_ref, l_ref, acc_ref):
        hm = head_match(q_s.shape[0])
        cols = []
        for j in range(pps):
            s = _dot_nt(q_s[...], k_refs[j][0].astype(BF16))
            if biases is not None:
                s = s + biases[j]
            cols.append(jnp.where(hm, s, NEG_INF))

        def pv(p):
            out = _dot(p[:, :ncol], v_refs[0][0].astype(BF16))
            for j in range(1, pps):
                out = out + _dot(p[:, j * ncol:(j + 1) * ncol], v_refs[j][0].astype(BF16))
            return out

        _online_update(jnp.concatenate(cols, axis=1), pv, m_ref, l_ref, acc_ref)

    paged(qa_s, ka_refs, va_refs, None, ma, la, acca)

    biases = []
    c = carry[...]
    for j in range(pps):
        g, c = _page_suffix_bias(lf_refs[j][0], c)
        biases.append(jnp.concatenate([g[r:r + 1, :] for r in range(SUBLANES)], axis=1))
    carry[...] = c
    paged(qb_s, kb_refs, vb_refs, biases, mb, lb, accb)

    @pl.when(pstep == n_steps - 1)
    def _():
        def new_mask(shape):
            r = lax.broadcasted_iota(jnp.int32, shape, 0)
            c = lax.broadcasted_iota(jnp.int32, shape, 1)
            return ((r % HB) == (c % HB)) & ((c // HB) <= ((r // HB) % dec_seq))

        s = _dot_nt(qa_s[...], kan_ref[0].astype(BF16))
        s = jnp.where(new_mask(s.shape), s, NEG_INF)
        _online_update(s, lambda p: _dot(p, van_ref[0].astype(BF16)), ma, la, acca)

        x = lfn_ref[0]
        cs = x
        for t in range(1, dec_seq):
            cs = cs + pltpu.roll(x, t * HB, 1)
        s = _dot_nt(qb_s[...], kbn_ref[0].astype(BF16)) - cs[:, :nrow]
        s = jnp.where(new_mask(s.shape), s, NEG_INF)
        _online_update(s, lambda p: _dot(p, vbn_ref[0].astype(BF16)), mb, lb, accb)

        o = acca[...] / la[...]
        o = o[:nrow] - lam_ref[0] * o[nrow:]
        oa_ref[0] = _subln(o, g_ref[...], lam_init)
        ob_ref[0] = accb[...] / lb[...]


def _sattn(page_table, lam, qa, qb, kan, van, kbn, vbn, lfn, g_sub, ck_a, cv_a, ck_b, cv_b, c_lf, lam_init):
    ns, n_pages = page_table.shape
    pps = PAGES_PER_STEP
    n_steps = n_pages // pps
    nrow = qa.shape[1]
    dec_seq = nrow // HB
    ncol = PAGE_SIZE * HB
    pt_flat = page_table.reshape(-1)

    def seq_map(b, p, pt):
        return (b, 0, 0)

    def page_map(j):
        return lambda b, p, pt: (pt[b * n_pages + (n_pages - 1 - (p * pps + j))], 0, 0)

    seq_spec = pl.BlockSpec((1, nrow, DH), seq_map)
    page_specs = [pl.BlockSpec((1, ncol, DH), page_map(j)) for j in range(pps)]
    lf_specs = [pl.BlockSpec((1, SUBLANES, LANES), page_map(j)) for j in range(pps)]
    return pl.pallas_call(
        functools.partial(_sattn_body, n_steps=n_steps, dec_seq=dec_seq, lam_init=lam_init),
        grid_spec=pltpu.PrefetchScalarGridSpec(
            num_scalar_prefetch=1,
            grid=(ns, n_steps),
            in_specs=[
                pl.BlockSpec(memory_space=pltpu.SMEM),
                seq_spec, seq_spec, seq_spec, seq_spec, seq_spec, seq_spec,
                pl.BlockSpec((1, 1, LANES), seq_map),
                pl.BlockSpec((1, DH), lambda b, p, pt: (0, 0)),
            ] + page_specs * 4 + lf_specs,
            out_specs=[seq_spec, seq_spec],
            scratch_shapes=[
                pltpu.VMEM((2 * nrow, DH), BF16), pltpu.VMEM((nrow, DH), BF16),
                pltpu.VMEM((2 * nrow, 1), F32), pltpu.VMEM((2 * nrow, 1), F32), pltpu.VMEM((2 * nrow, DH), F32),
                pltpu.VMEM((nrow, 1), F32), pltpu.VMEM((nrow, 1), F32), pltpu.VMEM((nrow, DH), F32),
                pltpu.VMEM((1, LANES), F32),
            ],
        ),
        out_shape=[jax.ShapeDtypeStruct((ns, nrow, DH), F32), jax.ShapeDtypeStruct((ns, nrow, DH), F32)],
        compiler_params=pltpu.CompilerParams(
            dimension_semantics=("arbitrary", "arbitrary"), vmem_limit_bytes=VMEM_LIMIT),
        name="sattn",
    )(pt_flat, lam, qa, qb, kan, van, kbn, vbn, lfn, g_sub,
      *([ck_a] * pps), *([cv_a] * pps), *([ck_b] * pps), *([cv_b] * pps), *([c_lf] * pps))


def _outproj_body(x_ref, oap_ref, obp_ref, oas_ref, obs_ref, wa_ref, wb_ref, y_ref, *, n_prompt_tiles):
    is_sample = pl.program_id(0) >= n_prompt_tiles
    oa = jnp.where(is_sample, oas_ref[...], oap_ref[...]).astype(BF16)
    ob = jnp.where(is_sample, obs_ref[...], obp_ref[...]).astype(BF16)
    y_ref[...] = x_ref[...] + _dot(oa, wa_ref[...]) + _dot(ob, wb_ref[...])


def _outproj(x, oa_p, ob_p, oa_s, ob_s, wa_bf, wb_bf):
    nt = x.shape[0]
    tm, tn = TM_PROJ, 1024
    ka, kb = oa_p.shape[1], ob_p.shape[1]
    npt = oa_p.shape[0] // tm
    nst = oa_s.shape[0] // tm
    assert npt * tm == oa_p.shape[0] and nst * tm == oa_s.shape[0] and (npt + nst) * tm == nt
    p_map = lambda i, j: (jnp.minimum(i, npt - 1), 0)
    s_map = lambda i, j: (jnp.maximum(i - npt, 0), 0)
    return pl.pallas_call(
        functools.partial(_outproj_body, n_prompt_tiles=npt),
        grid=(nt // tm, D_MODEL // tn),
        in_specs=[
            pl.BlockSpec((tm, tn), lambda i, j: (i, j)),
            pl.BlockSpec((tm, ka), p_map),
            pl.BlockSpec((tm, kb), p_map),
            pl.BlockSpec((tm, ka), s_map),
            pl.BlockSpec((tm, kb), s_map),
            pl.BlockSpec((ka, tn), lambda i, j: (0, j)),
            pl.BlockSpec((kb, tn), lambda i, j: (0, j)),
        ],
        out_specs=pl.BlockSpec((tm, tn), lambda i, j: (i, j)),
        out_shape=jax.ShapeDtypeStruct((nt, D_MODEL), F32),
        compiler_params=pltpu.CompilerParams(
            dimension_semantics=("arbitrary", "arbitrary"), vmem_limit_bytes=VMEM_LIMIT),
        name="outproj",
    )(x, oa_p, ob_p, oa_s, ob_s, wa_bf, wb_bf)


def _route_body(x_ref, g_ref, wr_ref, br_ref, h_ref, r_ref):
    h = _rms(x_ref[...], g_ref[...])
    h_ref[...] = h
    hi = h.astype(BF16)
    lo = (h - hi.astype(F32)).astype(BF16)
    wr = wr_ref[...]
    r_hi = _dot(hi, wr)
    lg = r_hi[:, :LANES] + (r_hi[:, LANES:] + _dot(lo, wr[:, :LANES])) + br_ref[...]
    lane = lax.broadcasted_iota(jnp.int32, lg.shape, 1).astype(F32)
    big = float(LANES)

    def first_max(vals, valid):
        v = jnp.where(valid, vals, NEG_INF)
        top = jnp.max(v, axis=1, keepdims=True)
        idx = jnp.min(jnp.where(valid & (v == top), lane, big), axis=1, keepdims=True)
        return top, idx

    is_grp = lane < N_GROUPS
    gmax, grp = first_max(lg, is_grp)
    p_grp = 1.0 / jnp.sum(jnp.where(is_grp, jnp.exp(lg - gmax), 0.0), axis=1, keepdims=True)
    lo = N_GROUPS + EXPERTS_PER_GROUP * grp
    in_grp = (lane >= lo) & (lane < lo + EXPERTS_PER_GROUP)
    v0, i0 = first_max(lg, in_grp)
    v1, i1 = first_max(lg, in_grp & (lane != i0))
    e = jnp.exp(v1 - v0)
    w0 = p_grp / (1.0 + e)
    w1 = p_grp * e / (1.0 + e)
    r = jnp.where(lane == 0, i0 - N_GROUPS, 0.0)
    r = jnp.where(lane == 1, i1 - N_GROUPS, r)
    r = jnp.where(lane == 2, w0, r)
    r = jnp.where(lane == 3, w1, r)
    r_ref[...] = r


def _route(x, g, wr, br):
    nt = x.shape[0]
    tm = TM_ROUTE
    return pl.pallas_call(
        _route_body,
        grid=(nt // tm,),
        in_specs=[
            pl.BlockSpec((tm, D_MODEL), lambda i: (i, 0)),
            pl.BlockSpec((1, D_MODEL), lambda i: (0, 0)),
            pl.BlockSpec((D_MODEL, 2 * LANES), lambda i: (0, 0)),
            pl.BlockSpec((1, LANES), lambda i: (0, 0)),
        ],
        out_specs=[pl.BlockSpec((tm, D_MODEL), lambda i: (i, 0)), pl.BlockSpec((tm, LANES), lambda i: (i, 0))],
        out_shape=[jax.ShapeDtypeStruct((nt, D_MODEL), F32), jax.ShapeDtypeStruct((nt, LANES), F32)],
        compiler_params=pltpu.CompilerParams(dimension_semantics=("arbitrary",), vmem_limit_bytes=VMEM_LIMIT),
        name="route",
    )(x, g, wr, br)


def _start_row_gather(idx_ref, idx_base, src_ref, dst_ref, sem, n_rows, straight_line=False):
    def copy(r):
        pltpu.make_async_copy(src_ref.at[pl.ds(idx_ref[idx_base + r], 1), :],
                              dst_ref.at[pl.ds(r, 1), :], sem).start()

    if straight_line:
        for r in range(n_rows):
            copy(r)
        return

    def issue(c, carry):
        for u in range(DMA_UNROLL):
            copy(c * DMA_UNROLL + u)
        return carry

    lax.fori_loop(0, n_rows // DMA_UNROLL, issue, 0)


def _wait_row_gather(src_ref, dst_ref, sem, n_rows):
    pltpu.make_async_copy(src_ref.at[pl.ds(0, n_rows), :], dst_ref, sem).wait()


def _expert_body(te_ref, ne_ref, nu_ref, src_ref, h_ref, wg_ref, wu_ref, wd_ref, y_ref, hbuf0, hbuf1, sems,
                 wg_f, wu_f, wd_f, wsem, wg_s, wu_s, wd_s, *, n_tiles, layer):
    tm = TM_EXPERT
    t = pl.program_id(0)
    n_used = nu_ref[0]
    bufs = (hbuf0, hbuf1)
    weights = ((wg_ref, wg_f, wg_s), (wu_ref, wu_f, wu_s), (wd_ref, wd_f, wd_s))

    def weight_copies(e):
        return [pltpu.make_async_copy(w_ref.at[layer, e], w_f, wsem) for w_ref, w_f, _ in weights]

    @pl.when(t == 0)
    def _():
        _start_row_gather(src_ref, 0, h_ref, hbuf0, sems.at[0], tm)
        for c in weight_copies(te_ref[0]):
            c.start()

    @pl.when(t < n_used)
    def _():
        e = te_ref[t]
        prev = te_ref[jnp.maximum(t - 1, 0)]

        @pl.when((t == 0) | (e != prev))
        def _():
            for c in weight_copies(e):
                c.wait()
            for _, w_f, w_s in weights:
                w_s[...] = w_f[...].astype(BF16)
            nxt_e = ne_ref[t]

            @pl.when(nxt_e < N_EXPERTS)
            def _():
                for c in weight_copies(nxt_e):
                    c.start()

    nxt = jnp.minimum(t + 1, n_tiles - 1)
    for par in range(2):
        @pl.when((t < n_used) & (t % 2 == par))
        def _():
            _wait_row_gather(h_ref, bufs[par], sems.at[par], tm)
            hb = bufs[par][...].astype(BF16)
            _start_row_gather(src_ref, nxt * tm, h_ref, bufs[1 - par], sems.at[1 - par], tm, straight_line=True)
            hg = _dot(hb, wg_s[...])
            hu = _dot(hb, wu_s[...])
            act = hg * jax.nn.sigmoid(hg) * hu
            y_ref[...] = _dot(act.astype(BF16), wd_s[...])

        @pl.when((t == n_used) & (t % 2 == par))
        def _():
            _wait_row_gather(h_ref, bufs[par], sems.at[par], tm)

        @pl.when((t == n_tiles - 1) & (t < n_used) & (t % 2 == par))
        def _():
            _wait_row_gather(h_ref, bufs[1 - par], sems.at[1 - par], tm)

    @pl.when(t >= n_used)
    def _():
        y_ref[...] = jnp.zeros(y_ref.shape, F32)


def _experts(tile_expert, next_expert, n_used, src_token, h, w_gate, w_up, w_down, layer):
    tm = TM_EXPERT
    n_slots = src_token.shape[0]
    any_spec = pl.BlockSpec(memory_space=pl.ANY)
    return pl.pallas_call(
        functools.partial(_expert_body, n_tiles=n_slots // tm, layer=layer),
        grid_spec=pltpu.PrefetchScalarGridSpec(
            num_scalar_prefetch=4,
            grid=(n_slots // tm,),
            in_specs=[any_spec, any_spec, any_spec, any_spec],
            out_specs=pl.BlockSpec((tm, D_MODEL), lambda t, te, ne, nu, src: (t, 0)),
            scratch_shapes=[pltpu.VMEM((tm, D_MODEL), F32), pltpu.VMEM((tm, D_MODEL), F32),
                            pltpu.SemaphoreType.DMA((2,)),
                            pltpu.VMEM((D_MODEL, D_EXPERT), F32), pltpu.VMEM((D_MODEL, D_EXPERT), F32),
                            pltpu.VMEM((D_EXPERT, D_MODEL), F32), pltpu.SemaphoreType.DMA(()),
                            pltpu.VMEM((D_MODEL, D_EXPERT), BF16), pltpu.VMEM((D_MODEL, D_EXPERT), BF16),
                            pltpu.VMEM((D_EXPERT, D_MODEL), BF16)],
        ),
        out_shape=jax.ShapeDtypeStruct((n_slots, D_MODEL), F32),
        compiler_params=pltpu.CompilerParams(dimension_semantics=("arbitrary",), vmem_limit_bytes=VMEM_LIMIT),
        name="experts",
    )(tile_expert, next_expert, n_used, src_token, h, w_gate, w_up, w_down)


def _combine_body(slot_ref, x_ref, r_ref, y_ref, g_ref, *rest, nt, n_steps, row_start, n_first, final_norm):
    tm = TM_COMBINE
    t = pl.program_id(0)
    o_refs, (b00, b01, b10, b11, sems) = rest[:-5], rest[-5:]
    bufs = ((b00, b01), (b10, b11))

    def start(step, par, straight_line):
        for k in range(2):
            _start_row_gather(slot_ref, k * nt + row_start(step), y_ref, bufs[par][k], sems.at[par], tm,
                              straight_line)

    def wait(par):
        for k in range(2):
            _wait_row_gather(y_ref, bufs[par][k], sems.at[par], tm)

    @pl.when(t == 0)
    def _():
        start(0, 0, False)

    nxt = jnp.minimum(t + 1, n_steps - 1)
    for par in range(2):
        @pl.when(t % 2 == par)
        def _():
            wait(par)
            start(nxt, 1 - par, True)
            r = r_ref[...]
            out = x_ref[...] + r[:, 2:3] * bufs[par][0][...] + r[:, 3:4] * bufs[par][1][...]
            if final_norm:
                out = _rms(out, g_ref[...])
            if len(o_refs) == 1:
                o_refs[0][...] = out
            else:
                @pl.when(t < n_first)
                def _():
                    o_refs[0][...] = out

                @pl.when(t >= n_first)
                def _():
                    o_refs[1][...] = out

        @pl.when((t == n_steps - 1) & (t % 2 == par))
        def _():
            wait(1 - par)


def _combine(slots, x, route, y, g, out_rows=None):
    nt = x.shape[0]
    tm = TM_COMBINE
    if out_rows is None:
        n_steps = n_first = nt // tm
        row_start = lambda t: t * tm
        out_specs = [pl.BlockSpec((tm, D_MODEL), lambda i, sl: (i, 0))]
        out_shape = [jax.ShapeDtypeStruct((nt, D_MODEL), F32)]
    else:
        nb, lp, first, count, npr = out_rows
        per = count // tm
        n_first = nb * per
        n_second = (nt - npr) // tm
        assert per * tm == count and n_second * tm == nt - npr and first % SUBLANES == 0
        n_steps = n_first + n_second
        row_start = lambda t: jnp.where(t < n_first, (t // per) * lp + first + (t % per) * tm,
                                        npr + (t - n_first) * tm)
        out_specs = [pl.BlockSpec((tm, D_MODEL), lambda i, sl: (jnp.minimum(i, n_first - 1), 0)),
                     pl.BlockSpec((tm, D_MODEL), lambda i, sl: (jnp.maximum(i - n_first, 0), 0))]
        out_shape = [jax.ShapeDtypeStruct((nb * count, D_MODEL), F32),
                     jax.ShapeDtypeStruct((nt - npr, D_MODEL), F32)]
    res = pl.pallas_call(
        functools.partial(_combine_body, nt=nt, n_steps=n_steps, row_start=row_start, n_first=n_first,
                          final_norm=out_rows is not None),
        grid_spec=pltpu.PrefetchScalarGridSpec(
            num_scalar_prefetch=1,
            grid=(n_steps,),
            in_specs=[
                pl.BlockSpec((pl.Element(tm), pl.Element(D_MODEL)),
                             lambda i, sl: (pl.multiple_of(row_start(i), SUBLANES), 0)),
                pl.BlockSpec((pl.Element(tm), pl.Element(LANES)),
                             lambda i, sl: (pl.multiple_of(row_start(i), SUBLANES), 0)),
                pl.BlockSpec(memory_space=pl.ANY),
                pl.BlockSpec((1, D_MODEL), lambda i, sl: (0, 0)),
            ],
            out_specs=out_specs,
            scratch_shapes=[pltpu.VMEM((tm, D_MODEL), F32)] * 4 + [pltpu.SemaphoreType.DMA((2,))],
        ),
        out_shape=out_shape,
        compiler_params=pltpu.CompilerParams(dimension_semantics=("arbitrary",), vmem_limit_bytes=VMEM_LIMIT),
        name="combine",
    )(slots, x, route, y, g)
    return res[0] if out_rows is None else res


def _dispatch_plan(route, valid):
    nt = route.shape[0]
    tm = TM_EXPERT
    n_valid = int(np.sum(valid))
    n_slots = -(-(2 * n_valid + N_EXPERTS * (tm - 1)) // tm) * tm
    n_tiles = n_slots // tm
    valid2 = jnp.asarray(np.concatenate([valid, valid]))
    eidx = jnp.concatenate([route[:, 0], route[:, 1]]).astype(jnp.int32)
    eidx = jnp.where(valid2, eidx, N_EXPERTS)
    onehot = (eidx[:, None] == jnp.arange(N_EXPERTS, dtype=jnp.int32)[None, :]).astype(jnp.int32)
    incl = jnp.cumsum(onehot, axis=0)
    counts = incl[-1]
    rank = jnp.sum((incl - 1) * onehot, axis=1)
    padded = ((counts + tm - 1) // tm) * tm
    ends = jnp.cumsum(padded)
    starts = ends - padded
    pair = jnp.arange(2 * nt, dtype=jnp.int32)
    slot = jnp.where(valid2, jnp.take(starts, jnp.minimum(eidx, N_EXPERTS - 1)) + rank, pair % n_slots)
    slot = slot.astype(jnp.int32)
    token = pair % nt
    dest = jnp.where(valid2, slot, n_slots)
    src_token = (jnp.arange(n_slots, dtype=jnp.int32) % nt).at[dest].set(token, mode="drop")
    n_used = (ends[-1] // tm).astype(jnp.int32)
    tile_start = jnp.arange(n_tiles, dtype=jnp.int32) * tm
    tile_expert = jnp.sum((tile_start[:, None] >= ends[None, :]).astype(jnp.int32), axis=1)
    last_expert = jnp.take(tile_expert, jnp.maximum(n_used - 1, 0))
    tile_expert = jnp.where(jnp.arange(n_tiles) < n_used, tile_expert, last_expert)
    tile_expert = jnp.minimum(tile_expert, N_EXPERTS - 1).astype(jnp.int32)
    ids = jnp.arange(N_EXPERTS, dtype=jnp.int32)
    later = (ids[None, :] > ids[:, None]) & (counts[None, :] > 0)
    next_nonempty = jnp.min(jnp.where(later, ids[None, :], N_EXPERTS), axis=1)
    next_expert = jnp.take(next_nonempty, tile_expert).astype(jnp.int32)
    return slot, src_token, tile_expert, next_expert, n_used.reshape(1)


def _moe(x, valid, g_ffn, wr, br, w_gate, w_up, w_down, layer, g_final, out_rows):
    h, route = _route(x, g_ffn, wr, br)
    slot, src_token, tile_expert, next_expert, n_used = _dispatch_plan(route, valid)
    y = _experts(tile_expert, next_expert, n_used, src_token, h, w_gate, w_up, w_down, layer)
    return _combine(slot, x, route, y, g_final, out_rows)


def _pool_windows(ext_rows, u, inv_cnt, w_ref, scale):
    outs = []
    for gi, w in enumerate(POOL_WINDOWS):
        c0 = gi * POOL_CH
        win = u[:, c0:c0 + POOL_CH]
        for j in range(1, w):
            win = win + ext_rows(j, c0)
        d = win * inv_cnt[gi] - u[:, c0:c0 + POOL_CH]
        outs.append(_dot(d.astype(BF16), w_ref[gi]))
    return jnp.concatenate(outs, axis=1) * scale


def _pool_prompt_body(xc_ref, xp_ref, g_ref, w_ref, sc_ref, y_ref, st_ref, ext, *, seq_len, nblk, n_prompt_blocks):
    step = pl.program_id(0)
    i = step % nblk
    tp = TP_POOL
    pad = POOL_BUF + 1
    last = seq_len - 1

    @pl.when(step < n_prompt_blocks)
    def _():
        x = xc_ref[...]
        u = _rms(x, g_ref[...])
        tail = _rms(xp_ref[pl.ds(tp - pad, pad), :], g_ref[...])
        ext[pl.ds(0, pad), :] = jnp.where(i > 0, tail, 0.0)
        ext[pl.ds(pad, tp), :] = u
        pos = i * tp + lax.broadcasted_iota(jnp.int32, (tp, 1), 0)
        inv_cnt = [1.0 / jnp.minimum(w, pos + 1).astype(F32) for w in POOL_WINDOWS]

        def ext_rows(j, c0):
            return ext[pl.ds(pad - j, tp), c0:c0 + POOL_CH]

        y_ref[...] = x + _pool_windows(ext_rows, u, inv_cnt, w_ref, sc_ref[...])

        @pl.when(i == last // tp)
        def _():
            st_ref[0] = ext[pl.ds(last % tp + 1, pad), :]

    @pl.when(step >= n_prompt_blocks)
    def _():
        y_ref[...] = xc_ref[...]


def _pool_prompt(x, g, w_bf, scale, nb, lp, seq_len):
    tp = TP_POOL
    nt = x.shape[0]
    nblk = lp // tp
    npb = nb * nblk
    pad = POOL_BUF + 1
    assert nt % tp == 0

    def prev_map(s):
        return (jnp.where((s % nblk == 0) | (s >= npb), s, s - 1), 0)

    return pl.pallas_call(
        functools.partial(_pool_prompt_body, seq_len=seq_len, nblk=nblk, n_prompt_blocks=npb),
        grid=(nt // tp,),
        in_specs=[
            pl.BlockSpec((tp, D_MODEL), lambda s: (s, 0)),
            pl.BlockSpec((tp, D_MODEL), prev_map),
            pl.BlockSpec((1, D_MODEL), lambda s: (0, 0)),
            pl.BlockSpec((len(POOL_WINDOWS), POOL_CH, POOL_CH), lambda s: (0, 0, 0)),
            pl.BlockSpec((1, D_MODEL), lambda s: (0, 0)),
        ],
        out_specs=[
            pl.BlockSpec((tp, D_MODEL), lambda s: (s, 0)),
            pl.BlockSpec((1, pad, D_MODEL), lambda s: (jnp.minimum(s // nblk, nb - 1), 0, 0)),
        ],
        out_shape=[jax.ShapeDtypeStruct((nt, D_MODEL), F32), jax.ShapeDtypeStruct((nb, pad, D_MODEL), F32)],
        scratch_shapes=[pltpu.VMEM((tp + pad, D_MODEL), F32)],
        compiler_params=pltpu.CompilerParams(dimension_semantics=("arbitrary",), vmem_limit_bytes=VMEM_LIMIT),
        name="pool_prompt",
    )(x, x, g, w_bf, scale)


def _pool_sample_body(x_ref, st_ref, g_ref, w_ref, sc_ref, y_ref, so_ref, *, dec_seq):
    g = g_ref[...]
    xs = [x_ref[:, t, :] for t in range(dec_seq)]
    us = [_rms(x, g) for x in xs]

    def ext_row(e, c0):
        if e < POOL_BUF:
            return st_ref[:, e, c0:c0 + POOL_CH]
        return us[e - POOL_BUF][:, c0:c0 + POOL_CH]

    for t in range(dec_seq):
        inv_cnt = [1.0 / w for w in POOL_WINDOWS]
        y = _pool_windows(lambda j, c0: ext_row(POOL_BUF + t - j, c0), us[t], inv_cnt, w_ref, sc_ref[...])
        y_ref[:, t, :] = xs[t] + y
    for e in range(POOL_BUF):
        src = e + dec_seq
        so_ref[:, e, :] = st_ref[:, src, :] if src < POOL_BUF else us[src - POOL_BUF]


def _pool_sample(x3, state, g, w_bf, scale):
    ns, dec_seq, _ = x3.shape
    ts = 32
    return pl.pallas_call(
        functools.partial(_pool_sample_body, dec_seq=dec_seq),
        grid=(ns // ts,),
        in_specs=[
            pl.BlockSpec((ts, dec_seq, D_MODEL), lambda i: (i, 0, 0)),
            pl.BlockSpec((ts, POOL_BUF, D_MODEL), lambda i: (i, 0, 0)),
            pl.BlockSpec((1, D_MODEL), lambda i: (0, 0)),
            pl.BlockSpec((len(POOL_WINDOWS), POOL_CH, POOL_CH), lambda i: (0, 0, 0)),
            pl.BlockSpec((1, D_MODEL), lambda i: (0, 0)),
        ],
        out_specs=[
            pl.BlockSpec((ts, dec_seq, D_MODEL), lambda i: (i, 0, 0)),
            pl.BlockSpec((ts, POOL_BUF, D_MODEL), lambda i: (i, 0, 0)),
        ],
        out_shape=[jax.ShapeDtypeStruct(x3.shape, F32), jax.ShapeDtypeStruct(state.shape, F32)],
        compiler_params=pltpu.CompilerParams(dimension_semantics=("arbitrary",), vmem_limit_bytes=VMEM_LIMIT),
        name="pool_sample",
    )(x3, state, g, w_bf, scale)


def _rope_tables(pos):
    half = ROT_DIM // 2
    inv = ROPE_THETA ** (-jnp.arange(0, ROT_DIM, 2, dtype=F32) / ROT_DIM)
    ang = pos.astype(F32)[:, None] * inv[None, :]
    cos, sin = jnp.cos(ang), jnp.sin(ang)
    n = pos.shape[0]
    ones = jnp.ones((n, DQK_A - ROT_DIM), F32)
    zeros = jnp.zeros((n, DQK_A - ROT_DIM), F32)
    zh = jnp.zeros((n, half), F32)
    c = jnp.concatenate([cos, cos, ones], axis=1)
    s_lo = jnp.concatenate([zh, sin, zeros], axis=1)
    s_hi = jnp.concatenate([-sin, zh, zeros], axis=1)
    return jnp.stack([jnp.tile(t, (1, LANES // DQK_A)) for t in (c, s_lo, s_hi)])


def kernel(x_prompt, x_sample, cache_k_a, cache_v_a, cache_k_b, cache_v_b, cache_logf_b, state_pool, page_table,
           meta_tokens, g_mix, w_in, b_forget, lambda_qk, g_subln, w_out, w_pool, pool_scale, g_ffn,
           w_route_group, b_route_group, w_route_expert, b_route_expert, w_gate, w_up, w_down, g_final):
    nb, seq, _ = x_prompt.shape
    ns, dec_seq, _ = x_sample.shape
    depth = g_mix.shape[0]
    seq_len = seq + N_META
    lp = -(-seq_len // TQ) * TQ
    npr = nb * lp
    n_samp = ns * dec_seq
    nt = npr + n_samp
    past = page_table.shape[1] * PAGE_SIZE
    n_phys = cache_k_a.shape[1]
    assert nt % TM_PROJ == 0 and lp % TP_POOL == 0 and dec_seq * HB <= LANES
    assert page_table.shape[1] % PAGES_PER_STEP == 0 and PAGE_SIZE * HB == SUBLANES * LANES

    pieces = []
    for b in range(nb):
        pieces += [meta_tokens, x_prompt[b], jnp.zeros((lp - seq_len, D_MODEL), F32)]
    x = jnp.concatenate(pieces + [x_sample.reshape(n_samp, D_MODEL)], axis=0)
    valid = np.concatenate([np.tile(np.arange(lp) < seq_len, nb), np.ones(n_samp, bool)])
    pos = jnp.concatenate([jnp.tile(jnp.arange(lp, dtype=jnp.int32), nb),
                           past + jnp.tile(jnp.arange(dec_seq, dtype=jnp.int32), ns)])
    rope = _rope_tables(pos)

    outs = {k: [] for k in ("ka_p", "va_p", "kb_p", "vb_p", "lf_p", "pool_p",
                            "ka_s", "va_s", "kb_s", "vb_s", "lf_s", "pool_s")}
    for l in range(depth):
        if l % 2 == 0:
            i = l // 2
            lam_init = 0.8 - 0.6 * math.exp(-0.3 * l)
            lq = lambda_qk[i]
            lam = (jnp.exp(jnp.sum(lq[0] * lq[1])) - jnp.exp(jnp.sum(lq[2] * lq[3])) + lam_init).reshape(1)
            w_main = w_in[i, :, :D_QKV].astype(BF16)
            w_f = jnp.pad(w_in[i, :, D_QKV:], ((0, 0), (0, LANES - HB))).astype(BF16)
            b_f = jnp.pad(b_forget[i], (0, LANES - HB)).reshape(1, LANES)
            g_sub = g_subln[i].reshape(1, DH)
            p, lf = _inproj(x, g_mix[l].reshape(1, D_MODEL), w_main, w_f, b_f, rope)

            p_sample = p[npr:].reshape(ns, dec_seq, 6, HA, DH).transpose(2, 0, 1, 3, 4)
            for name, c in (("ka", 1), ("va", 2), ("kb", 4), ("vb", 5)):
                rows = [p[b * lp:b * lp + seq_len, c * HA * DH:(c + 1) * HA * DH] for b in range(nb)]
                outs[name + "_p"].append(jnp.stack(rows).reshape(nb, seq_len, HA, DH))
                outs[name + "_s"].append(p_sample[c])
            outs["lf_p"].append(lf[:npr].reshape(nb, lp, LANES)[:, :seq_len, :HB])
            outs["lf_s"].append(lf[npr:, :HB].reshape(ns, dec_seq, HB))

            fc, fct = _fcum(lf[:npr], nb, lp)
            fk = fct[:, :HB].reshape(nb, HB, 1, lp)
            oa_p = _flash_a(p, lam, g_sub, nb, lp, lam_init)
            ob_p = _flash_b(p, fc, fk, nb, lp)

            samp = [p_sample[c].reshape(ns, dec_seq * HA, DH) for c in range(6)]
            lfn = jnp.pad(lf[npr:, :HB].reshape(ns, 1, dec_seq * HB), ((0, 0), (0, 0), (0, LANES - dec_seq * HB)))
            oa_s, ob_s = _sattn(
                page_table, lam, samp[0], samp[3], samp[1], samp[2], samp[4], samp[5], lfn, g_sub,
                cache_k_a[i].reshape(n_phys, PAGE_SIZE * HA, DH), cache_v_a[i].reshape(n_phys, PAGE_SIZE * HA, DH),
                cache_k_b[i].reshape(n_phys, PAGE_SIZE * HB, DH), cache_v_b[i].reshape(n_phys, PAGE_SIZE * HB, DH),
                cache_logf_b[i].reshape(n_phys, SUBLANES, LANES), lam_init)
            wo = w_out[i].astype(BF16)
            x = _outproj(x, oa_p, ob_p, oa_s.reshape(n_samp, HA * DH), ob_s.reshape(n_samp, HB * DH),
                         wo[:HA * DH], wo[HA * DH:])
        else:
            j = l // 2
            g = g_mix[l].reshape(1, D_MODEL)
            wp = w_pool[j].astype(BF16)
            sc = pool_scale[j].reshape(1, D_MODEL)
            ys, st_s = _pool_sample(x[npr:].reshape(ns, dec_seq, D_MODEL), state_pool[j], g, wp, sc)
            y, st_p = _pool_prompt(x, g, wp, sc, nb, lp, seq_len)
            outs["pool_p"].append(st_p[:, 1:])
            outs["pool_s"].append(st_s)
            x = lax.dynamic_update_slice(y, ys.reshape(n_samp, D_MODEL), (npr, 0))

        wr = jnp.concatenate([w_route_group[l], w_route_expert[l].reshape(D_MODEL, N_EXPERTS),
                              jnp.zeros((D_MODEL, LANES - N_GROUPS - N_EXPERTS), F32)], axis=1)
        wr_hi = wr.astype(BF16)
        wr = jnp.concatenate([wr_hi, (wr - wr_hi.astype(F32)).astype(BF16)], axis=1)
        br =jnp.concatenate([b_route_group[l], b_route_expert[l].reshape(N_EXPERTS),
                              jnp.zeros((LANES - N_GROUPS - N_EXPERTS,), F32)]).reshape(1, LANES)
        out_rows = (nb, lp, N_META, seq, npr) if l == depth - 1 else None
        x = _moe(x, valid, g_ffn[l].reshape(1, D_MODEL), wr, br, w_gate, w_up, w_down, l,
                 g_final.reshape(1, D_MODEL), out_rows)
    y_prompt = x[0].reshape(nb, seq, D_MODEL)
    y_sample = x[1].reshape(ns, dec_seq, D_MODEL)
    st = lambda k: jnp.stack(outs[k])
    return (y_prompt, y_sample, st("ka_p"), st("va_p"), st("kb_p"), st("vb_p"), st("lf_p"), st("pool_p"),
            st("ka_s"), st("va_s"), st("kb_s"), st("vb_s"), st("lf_s"), st("pool_s"))
```

```python
import functools
import math

import jax
import jax.numpy as jnp
import numpy as np
from jax import lax
from jax.experimental import pallas as pl
from jax.experimental.pallas import tpu as pltpu

F32 = jnp.float32
BF16 = jnp.bfloat16

D_MODEL = 2048
N_META = 16
HA = 8
HB = 8
DH = 128
DQK_A = 64
ROT_DIM = 16
ROPE_THETA = 500000.0
POOL_WINDOWS = (2, 4, 8, 16)
POOL_CH = D_MODEL // len(POOL_WINDOWS)
POOL_BUF = max(POOL_WINDOWS) - 1
N_GROUPS = 4
EXPERTS_PER_GROUP = 8
N_EXPERTS = N_GROUPS * EXPERTS_PER_GROUP
D_EXPERT = D_MODEL // 4
EPS = 1e-6
NEG_INF = -1e30
PAGE_SIZE = 128
D_QKV = 6 * HA * DH

LANES = 128
SUBLANES = 8
VMEM_LIMIT = 56 * 1024 * 1024

TQ = 256
TM_PROJ = 512
TN_PROJ = 1024
TM_ROUTE = 256
TM_EXPERT = 256
TM_COMBINE = 256
TP_POOL = 256
PAGES_PER_STEP = 8
DMA_UNROLL = 8


def _rms(x, g):
    return x * lax.rsqrt(jnp.mean(x * x, axis=-1, keepdims=True) + EPS) * g


def _dot(a, b):
    return jnp.dot(a, b, preferred_element_type=F32)


def _dot_nt(a, b):
    return lax.dot_general(a, b, (((1,), (1,)), ((), ())), preferred_element_type=F32)


def _inproj_body(x_ref, g_ref, w_ref, wf_ref, bf_ref, rope_ref, p_ref, lf_ref, h_scr, *, n_rope_tiles, tn):
    j = pl.program_id(1)

    @pl.when(j == 0)
    def _():
        hb = _rms(x_ref[...], g_ref[...]).astype(BF16)
        h_scr[...] = hb
        fl = _dot(hb, wf_ref[...]) + bf_ref[...]
        lf_ref[...] = jnp.minimum(fl, 0.0) - jnp.log1p(jnp.exp(-jnp.abs(fl)))

    @pl.when(j < n_rope_tiles)
    def _():
        c = rope_ref[0]
        s_lo = rope_ref[1]
        s_hi = rope_ref[2]
        wide = 2 * LANES
        for cw in range(tn // wide):
            acc = _dot(h_scr[...], w_ref[:, cw * wide:(cw + 1) * wide])
            for cc in range(2):
                a = acc[:, cc * LANES:(cc + 1) * LANES]
                lo = cw * wide + cc * LANES
                p_ref[:, lo:lo + LANES] = (
                    a * c + pltpu.roll(a, ROT_DIM // 2, 1) * s_lo + pltpu.roll(a, LANES - ROT_DIM // 2, 1) * s_hi)

    @pl.when(j >= n_rope_tiles)
    def _():
        p_ref[...] = _dot(h_scr[...], w_ref[...])


def _inproj(x, g, w_bf, wf_bf, bf, rope):
    nt = x.shape[0]
    tm, tn = TM_PROJ, TN_PROJ
    n_rope_tiles = (2 * HA * DH) // tn
    return pl.pallas_call(
        functools.partial(_inproj_body, n_rope_tiles=n_rope_tiles, tn=tn),
        grid=(nt // tm, D_QKV // tn),
        in_specs=[
            pl.BlockSpec((tm, D_MODEL), lambda i, j: (i, 0)),
            pl.BlockSpec((1, D_MODEL), lambda i, j: (0, 0)),
            pl.BlockSpec((D_MODEL, tn), lambda i, j: (0, j)),
            pl.BlockSpec((D_MODEL, LANES), lambda i, j: (0, 0)),
            pl.BlockSpec((1, LANES), lambda i, j: (0, 0)),
            pl.BlockSpec((3, tm, LANES), lambda i, j: (0, i, 0)),
        ],
        out_specs=[
            pl.BlockSpec((tm, tn), lambda i, j: (i, j)),
            pl.BlockSpec((tm, LANES), lambda i, j: (i, 0)),
        ],
        out_shape=[jax.ShapeDtypeStruct((nt, D_QKV), F32), jax.ShapeDtypeStruct((nt, LANES), F32)],
        scratch_shapes=[pltpu.VMEM((tm, D_MODEL), BF16)],
        compiler_params=pltpu.CompilerParams(
            dimension_semantics=("arbitrary", "arbitrary"), vmem_limit_bytes=VMEM_LIMIT),
        name="inproj",
    )(x, g, w_bf, wf_bf, bf, rope)


def _fcum_body(lf_ref, fc_ref, fct_ref, *, lp):
    blk = TQ
    row = lax.broadcasted_iota(jnp.int32, (blk, blk), 0)
    col = lax.broadcasted_iota(jnp.int32, (blk, blk), 1)
    tri = jnp.where(row >= col, 1.0, 0.0).astype(BF16)
    carry = jnp.zeros((1, LANES), F32)
    for b in range(lp // blk):
        x = lf_ref[pl.ds(b * blk, blk), :]
        hi = x.astype(BF16)
        r1 = x - hi.astype(F32)
        mid = r1.astype(BF16)
        lo = (r1 - mid.astype(F32)).astype(BF16)
        c = _dot(tri, hi) + _dot(tri, mid) + _dot(tri, lo) + carry
        fc_ref[pl.ds(b * blk, blk), :] = c
        carry = c[blk - 1:blk, :]
    fct_ref[0] = fc_ref[...].T


def _fcum(lf, nb, lp):
    return pl.pallas_call(
        functools.partial(_fcum_body, lp=lp),
        grid=(nb,),
        in_specs=[pl.BlockSpec((lp, LANES), lambda b: (b, 0))],
        out_specs=[pl.BlockSpec((lp, LANES), lambda b: (b, 0)),
                   pl.BlockSpec((1, LANES, lp), lambda b: (b, 0, 0))],
        out_shape=[jax.ShapeDtypeStruct((nb * lp, LANES), F32), jax.ShapeDtypeStruct((nb, LANES, lp), F32)],
        name="fcum",
    )(lf)


def _split_maps(q):
    lane = lax.broadcasted_iota(jnp.int32, q.shape, 1)
    return jnp.concatenate([jnp.where(lane < DQK_A, q, 0.0), jnp.where(lane >= DQK_A, q, 0.0)], axis=0)


def _subln(o, g, lam_init):
    return _rms(o, g) * (1.0 - lam_init)


def _causal_rows(qs, kb, vb, i, bias=None):
    n = (i + 1) * TQ
    s = _dot_nt(qs, kb[:n])
    if bias is not None:
        s = s + bias(n)
    row = lax.broadcasted_iota(jnp.int32, (qs.shape[0], TQ), 0) % TQ
    col = lax.broadcasted_iota(jnp.int32, (qs.shape[0], TQ), 1)
    diag = jnp.where(col <= row, s[:, n - TQ:], NEG_INF)
    s = diag if i == 0 else jnp.concatenate([s[:, :n - TQ], diag], axis=1)
    p = jnp.exp(s - jnp.max(s, axis=1, keepdims=True))
    return _dot(p.astype(BF16), vb[:n]) / jnp.sum(p, axis=1, keepdims=True)


def _flash_a_body(lam_ref, q_ref, k_ref, v_ref, g_ref, o_ref, *, lam_init, nq):
    kb = k_ref[...].astype(BF16)
    vb = v_ref[...].astype(BF16)
    for i in range(nq):
        qs = _split_maps(q_ref[pl.ds(i * TQ, TQ), :] * (DQK_A ** -0.5)).astype(BF16)
        o = _causal_rows(qs, kb, vb, i)
        o = o[:TQ] - lam_ref[0] * o[TQ:]
        o_ref[pl.ds(i * TQ, TQ), :] = _subln(o, g_ref[...], lam_init)


def _flash_b_body(q_ref, k_ref, v_ref, fq_ref, fk_ref, o_ref, *, nq):
    h = pl.program_id(1)
    kb = k_ref[...].astype(BF16)
    vb = v_ref[...].astype(BF16)
    fk = fk_ref[0, 0]
    lane = lax.broadcasted_iota(jnp.int32, (TQ, LANES), 1)
    for i in range(nq):
        qs = (q_ref[pl.ds(i * TQ, TQ), :] * (DH ** -0.5)).astype(BF16)
        fq_col = jnp.sum(jnp.where(lane == h, fq_ref[pl.ds(i * TQ, TQ), :], 0.0), axis=1, keepdims=True)
        o_ref[pl.ds(i * TQ, TQ), :] = _causal_rows(qs, kb, vb, i, bias=lambda n: fq_col - fk[:, :n])


def _flash_a(p, lam, g_sub, nb, lp, lam_init):
    nq = lp // TQ
    qcol, kcol, vcol = 0, HA, 2 * HA
    seq = lambda c: pl.BlockSpec((lp, DH), lambda b, h: (b, c + h))
    return pl.pallas_call(
        functools.partial(_flash_a_body, lam_init=lam_init, nq=nq),
        grid=(nb, HA),
        in_specs=[pl.BlockSpec(memory_space=pltpu.SMEM), seq(qcol), seq(kcol), seq(vcol),
                  pl.BlockSpec((1, DH), lambda b, h: (0, 0))],
        out_specs=pl.BlockSpec((lp, DH), lambda b, h: (b, h)),
        out_shape=jax.ShapeDtypeStruct((nb * lp, HA * DH), F32),
        compiler_params=pltpu.CompilerParams(
            dimension_semantics=("arbitrary", "arbitrary"), vmem_limit_bytes=VMEM_LIMIT),
        name="flash_a",
    )(lam, p, p, p, g_sub)


def _flash_b(p, fc, fk, nb, lp):
    nq = lp // TQ
    qcol, kcol, vcol = 3 * HA, 4 * HA, 5 * HA
    seq = lambda c: pl.BlockSpec((lp, DH), lambda b, h: (b, c + h))
    return pl.pallas_call(
        functools.partial(_flash_b_body, nq=nq),
        grid=(nb, HB),
        in_specs=[seq(qcol), seq(kcol), seq(vcol),
                  pl.BlockSpec((lp, LANES), lambda b, h: (b, 0)),
                  pl.BlockSpec((1, 1, 1, lp), lambda b, h: (b, h, 0, 0))],
        out_specs=pl.BlockSpec((lp, DH), lambda b, h: (b, h)),
        out_shape=jax.ShapeDtypeStruct((nb * lp, HB * DH), F32),
        compiler_params=pltpu.CompilerParams(
            dimension_semantics=("arbitrary", "arbitrary"), vmem_limit_bytes=VMEM_LIMIT),
        name="flash_b",
    )(p, p, p, fc, fk)


def _online_update(s, pv, m_ref, l_ref, acc_ref):
    m_prev = m_ref[...]
    m_new = jnp.maximum(m_prev, jnp.max(s, axis=1, keepdims=True))
    alpha = jnp.exp(m_prev - m_new)
    p = jnp.exp(s - m_new)
    l_ref[...] = alpha * l_ref[...] + jnp.sum(p, axis=1, keepdims=True)
    acc_ref[...] = alpha * acc_ref[...] + pv(p.astype(BF16))
    m_ref[...] = m_new


def _page_suffix_bias(x, carry):
    lane = lax.broadcasted_iota(jnp.int32, x.shape, 1)
    row = lax.broadcasted_iota(jnp.int32, x.shape, 0)
    y = x
    t = x
    for s in (8, 16, 32, 64):
        y = y + jnp.where(lane + s < LANES, pltpu.roll(y, LANES - s, 1), 0.0)
        t = t + pltpu.roll(t, s, 1)
    z = t
    for s in (1, 2, 4):
        z = z + jnp.where(row + s < SUBLANES, pltpu.roll(z, SUBLANES - s, 0), 0.0)
    g = (y - x) + (z - t) + carry
    return g, carry + z[0:1, :]


def _sattn_body(*refs, n_steps, dec_seq, lam_init):
    pps = PAGES_PER_STEP
    (_, lam_ref, qa_ref, qb_ref, kan_ref, van_ref, kbn_ref, vbn_ref, lfn_ref, g_ref), refs = refs[:10], refs[10:]
    ka_refs, va_refs, kb_refs, vb_refs, lf_refs = (refs[j * pps:(j + 1) * pps] for j in range(5))
    oa_ref, ob_ref, qa_s, qb_s, ma, la, acca, mb, lb, accb, carry = refs[5 * pps:]
    pstep = pl.program_id(1)
    nrow = dec_seq * HB
    ncol = PAGE_SIZE * HB

    @pl.when(pstep == 0)
    def _():
        qa_s[...] = _split_maps(qa_ref[0] * (DQK_A ** -0.5)).astype(BF16)
        qb_s[...] = (qb_ref[0] * (DH ** -0.5)).astype(BF16)
        for m_ref, l_ref, acc_ref in ((ma, la, acca), (mb, lb, accb)):
            m_ref[...] = jnp.full(m_ref.shape, NEG_INF, F32)
            l_ref[...] = jnp.zeros(l_ref.shape, F32)
            acc_ref[...] = jnp.zeros(acc_ref.shape, F32)
        carry[...] = jnp.zeros(carry.shape, F32)

    def head_match(nr):
        r = lax.broadcasted_iota(jnp.int32, (nr, ncol), 0)
        c = lax.broadcasted_iota(jnp.int32, (nr, ncol), 1)
        return (r % HB) == (c % HB)

    def paged(q_s, k_refs, v_refs, biases, m_ref, l_ref, acc_ref):
        hm = head_match(q_s.shape[0])
        cols = []
        for j in range(pps):
            s = _dot_nt(q_s[...], k_refs[j][0].astype(BF16))
            if biases is not None:
                s = s + biases[j]
            cols.append(jnp.where(hm, s, NEG_INF))

        def pv(p):
            out = _dot(p[:, :ncol], v_refs[0][0].astype(BF16))
            for j in range(1, pps):
                out = out + _dot(p[:, j * ncol:(j + 1) * ncol], v_refs[j][0].astype(BF16))
            return out

        _online_update(jnp.concatenate(cols, axis=1), pv, m_ref, l_ref, acc_ref)

    paged(qa_s, ka_refs, va_refs, None, ma, la, acca)

    biases = []
    c = carry[...]
    for j in range(pps):
        g, c = _page_suffix_bias(lf_refs[j][0], c)
        biases.append(jnp.concatenate([g[r:r + 1, :] for r in range(SUBLANES)], axis=1))
    carry[...] = c
    paged(qb_s, kb_refs, vb_refs, biases, mb, lb, accb)

    @pl.when(pstep == n_steps - 1)
    def _():
        def new_mask(shape):
            r = lax.broadcasted_iota(jnp.int32, shape, 0)
            c = lax.broadcasted_iota(jnp.int32, shape, 1)
            return ((r % HB) == (c % HB)) & ((c // HB) <= ((r // HB) % dec_seq))

        s = _dot_nt(qa_s[...], kan_ref[0].astype(BF16))
        s = jnp.where(new_mask(s.shape), s, NEG_INF)
        _online_update(s, lambda p: _dot(p, van_ref[0].astype(BF16)), ma, la, acca)

        x = lfn_ref[0]
        cs = x
        for t in range(1, dec_seq):
            cs = cs + pltpu.roll(x, t * HB, 1)
        s = _dot_nt(qb_s[...], kbn_ref[0].astype(BF16)) - cs[:, :nrow]
        s = jnp.where(new_mask(s.shape), s, NEG_INF)
        _online_update(s, lambda p: _dot(p, vbn_ref[0].astype(BF16)), mb, lb, accb)

        o = acca[...] / la[...]
        o = o[:nrow] - lam_ref[0] * o[nrow:]
        oa_ref[0] = _subln(o, g_ref[...], lam_init)
        ob_ref[0] = accb[...] / lb[...]


def _sattn(page_table, lam, qa, qb, kan, van, kbn, vbn, lfn, g_sub, ck_a, cv_a, ck_b, cv_b, c_lf, lam_init):
    ns, n_pages = page_table.shape
    pps = PAGES_PER_STEP
    n_steps = n_pages // pps
    nrow = qa.shape[1]
    dec_seq = nrow // HB
    ncol = PAGE_SIZE * HB
    pt_flat = page_table.reshape(-1)

    def seq_map(b, p, pt):
        return (b, 0, 0)

    def page_map(j):
        return lambda b, p, pt: (pt[b * n_pages + (n_pages - 1 - (p * pps + j))], 0, 0)

    seq_spec = pl.BlockSpec((1, nrow, DH), seq_map)
    page_specs = [pl.BlockSpec((1, ncol, DH), page_map(j)) for j in range(pps)]
    lf_specs = [pl.BlockSpec((1, SUBLANES, LANES), page_map(j)) for j in range(pps)]
    return pl.pallas_call(
        functools.partial(_sattn_body, n_steps=n_steps, dec_seq=dec_seq, lam_init=lam_init),
        grid_spec=pltpu.PrefetchScalarGridSpec(
            num_scalar_prefetch=1,
            grid=(ns, n_steps),
            in_specs=[
                pl.BlockSpec(memory_space=pltpu.SMEM),
                seq_spec, seq_spec, seq_spec, seq_spec, seq_spec, seq_spec,
                pl.BlockSpec((1, 1, LANES), seq_map),
                pl.BlockSpec((1, DH), lambda b, p, pt: (0, 0)),
            ] + page_specs * 4 + lf_specs,
            out_specs=[seq_spec, seq_spec],
            scratch_shapes=[
                pltpu.VMEM((2 * nrow, DH), BF16), pltpu.VMEM((nrow, DH), BF16),
                pltpu.VMEM((2 * nrow, 1), F32), pltpu.VMEM((2 * nrow, 1), F32), pltpu.VMEM((2 * nrow, DH), F32),
                pltpu.VMEM((nrow, 1), F32), pltpu.VMEM((nrow, 1), F32), pltpu.VMEM((nrow, DH), F32),
                pltpu.VMEM((1, LANES), F32),
            ],
        ),
        out_shape=[jax.ShapeDtypeStruct((ns, nrow, DH), F32), jax.ShapeDtypeStruct((ns, nrow, DH), F32)],
        compiler_params=pltpu.CompilerParams(
            dimension_semantics=("arbitrary", "arbitrary"), vmem_limit_bytes=VMEM_LIMIT),
        name="sattn",
    )(pt_flat, lam, qa, qb, kan, van, kbn, vbn, lfn, g_sub,
      *([ck_a] * pps), *([cv_a] * pps), *([ck_b] * pps), *([cv_b] * pps), *([c_lf] * pps))


def _outproj_body(x_ref, oap_ref, obp_ref, oas_ref, obs_ref, wa_ref, wb_ref, y_ref, *, n_prompt_tiles):
    is_sample = pl.program_id(0) >= n_prompt_tiles
    oa = jnp.where(is_sample, oas_ref[...], oap_ref[...]).astype(BF16)
    ob = jnp.where(is_sample, obs_ref[...], obp_ref[...]).astype(BF16)
    y_ref[...] = x_ref[...] + _dot(oa, wa_ref[...]) + _dot(ob, wb_ref[...])


def _outproj(x, oa_p, ob_p, oa_s, ob_s, wa_bf, wb_bf):
    nt = x.shape[0]
    tm, tn = TM_PROJ // 2, D_MODEL
    ka, kb = oa_p.shape[1], ob_p.shape[1]
    npt = oa_p.shape[0] // tm
    nst = oa_s.shape[0] // tm
    assert npt * tm == oa_p.shape[0] and nst * tm == oa_s.shape[0] and (npt + nst) * tm == nt
    p_map = lambda i, j: (jnp.minimum(i, npt - 1), 0)
    s_map = lambda i, j: (jnp.maximum(i - npt, 0), 0)
    return pl.pallas_call(
        functools.partial(_outproj_body, n_prompt_tiles=npt),
        grid=(nt // tm, D_MODEL // tn),
        in_specs=[
            pl.BlockSpec((tm, tn), lambda i, j: (i, j)),
            pl.BlockSpec((tm, ka), p_map),
            pl.BlockSpec((tm, kb), p_map),
            pl.BlockSpec((tm, ka), s_map),
            pl.BlockSpec((tm, kb), s_map),
            pl.BlockSpec((ka, tn), lambda i, j: (0, j)),
            pl.BlockSpec((kb, tn), lambda i, j: (0, j)),
        ],
        out_specs=pl.BlockSpec((tm, tn), lambda i, j: (i, j)),
        out_shape=jax.ShapeDtypeStruct((nt, D_MODEL), F32),
        compiler_params=pltpu.CompilerParams(
            dimension_semantics=("arbitrary", "arbitrary"), vmem_limit_bytes=VMEM_LIMIT),
        name="outproj",
    )(x, oa_p, ob_p, oa_s, ob_s, wa_bf, wb_bf)


def _route_body(x_ref, g_ref, wr_ref, br_ref, h_ref, r_ref):
    h = _rms(x_ref[...], g_ref[...])
    h_ref[...] = h
    hi = h.astype(BF16)
    lo = (h - hi.astype(F32)).astype(BF16)
    wr = wr_ref[...]
    r_hi = _dot(hi, wr)
    lg = r_hi[:, :LANES] + (r_hi[:, LANES:] + _dot(lo, wr[:, :LANES])) + br_ref[...]
    lane = lax.broadcasted_iota(jnp.int32, lg.shape, 1).astype(F32)
    big = float(LANES)

    def first_max(vals, valid):
        v = jnp.where(valid, vals, NEG_INF)
        top = jnp.max(v, axis=1, keepdims=True)
        idx = jnp.min(jnp.where(valid & (v == top), lane, big), axis=1, keepdims=True)
        return top, idx

    is_grp = lane < N_GROUPS
    gmax, grp = first_max(lg, is_grp)
    p_grp = 1.0 / jnp.sum(jnp.where(is_grp, jnp.exp(lg - gmax), 0.0), axis=1, keepdims=True)
    lo = N_GROUPS + EXPERTS_PER_GROUP * grp
    in_grp = (lane >= lo) & (lane < lo + EXPERTS_PER_GROUP)
    v0, i0 = first_max(lg, in_grp)
    v1, i1 = first_max(lg, in_grp & (lane != i0))
    e = jnp.exp(v1 - v0)
    w0 = p_grp / (1.0 + e)
    w1 = p_grp * e / (1.0 + e)
    r = jnp.where(lane == 0, i0 - N_GROUPS, 0.0)
    r = jnp.where(lane == 1, i1 - N_GROUPS, r)
    r = jnp.where(lane == 2, w0, r)
    r = jnp.where(lane == 3, w1, r)
    r_ref[...] = r


def _route(x, g, wr, br):
    nt = x.shape[0]
    tm = TM_ROUTE
    return pl.pallas_call(
        _route_body,
        grid=(nt // tm,),
        in_specs=[
            pl.BlockSpec((tm, D_MODEL), lambda i: (i, 0)),
            pl.BlockSpec((1, D_MODEL), lambda i: (0, 0)),
            pl.BlockSpec((D_MODEL, 2 * LANES), lambda i: (0, 0)),
            pl.BlockSpec((1, LANES), lambda i: (0, 0)),
        ],
        out_specs=[pl.BlockSpec((tm, D_MODEL), lambda i: (i, 0)), pl.BlockSpec((tm, LANES), lambda i: (i, 0))],
        out_shape=[jax.ShapeDtypeStruct((nt, D_MODEL), F32), jax.ShapeDtypeStruct((nt, LANES), F32)],
        compiler_params=pltpu.CompilerParams(dimension_semantics=("arbitrary",), vmem_limit_bytes=VMEM_LIMIT),
        name="route",
    )(x, g, wr, br)


def _start_row_gather(idx_ref, idx_base, src_ref, dst_ref, sem, n_rows, straight_line=False):
    def copy(r):
        pltpu.make_async_copy(src_ref.at[pl.ds(idx_ref[idx_base + r], 1)],
                              dst_ref.at[pl.ds(r, 1)], sem).start()

    if straight_line:
        for r in range(n_rows):
            copy(r)
        return

    def issue(c, carry):
        for u in range(DMA_UNROLL):
            copy(c * DMA_UNROLL + u)
        return carry

    lax.fori_loop(0, n_rows // DMA_UNROLL, issue, 0)


def _wait_row_gather(src_ref, dst_ref, sem, n_rows):
    pltpu.make_async_copy(src_ref.at[pl.ds(0, n_rows)], dst_ref, sem).wait()


def _expert_body(te_ref, ne_ref, nu_ref, src_ref, h_ref, wg_ref, wu_ref, wd_ref, y_ref, hbuf0, hbuf1, sems,
                 wg_f, wu_f, wd_f, wsem, wg_s, wu_s, wd_s, *, n_tiles, layer):
    tm = TM_EXPERT
    t = pl.program_id(0)
    n_used = nu_ref[0]
    bufs = (hbuf0, hbuf1)
    weights = ((wg_ref, wg_f, wg_s), (wu_ref, wu_f, wu_s), (wd_ref, wd_f, wd_s))

    def weight_copies(e):
        return [pltpu.make_async_copy(w_ref.at[layer, e], w_f, wsem) for w_ref, w_f, _ in weights]

    @pl.when(t == 0)
    def _():
        _start_row_gather(src_ref, 0, h_ref, hbuf0, sems.at[0], tm)
        for c in weight_copies(te_ref[0]):
            c.start()

    @pl.when(t < n_used)
    def _():
        e = te_ref[t]
        prev = te_ref[jnp.maximum(t - 1, 0)]

        @pl.when((t == 0) | (e != prev))
        def _():
            for c in weight_copies(e):
                c.wait()
            for _, w_f, w_s in weights:
                w_s[...] = w_f[...].astype(BF16)
            nxt_e = ne_ref[t]

            @pl.when(nxt_e < N_EXPERTS)
            def _():
                for c in weight_copies(nxt_e):
                    c.start()

    nxt = jnp.minimum(t + 1, n_tiles - 1)
    for par in range(2):
        @pl.when((t < n_used) & (t % 2 == par))
        def _():
            _wait_row_gather(h_ref, bufs[par], sems.at[par], tm)
            hb = bufs[par][...].astype(BF16)
            _start_row_gather(src_ref, nxt * tm, h_ref, bufs[1 - par], sems.at[1 - par], tm, straight_line=True)
            hg = _dot(hb, wg_s[...])
            hu = _dot(hb, wu_s[...])
            act = hg * jax.nn.sigmoid(hg) * hu
            y_ref[...] = _dot(act.astype(BF16), wd_s[...])

        @pl.when((t == n_used) & (t % 2 == par))
        def _():
            _wait_row_gather(h_ref, bufs[par], sems.at[par], tm)

        @pl.when((t == n_tiles - 1) & (t < n_used) & (t % 2 == par))
        def _():
            _wait_row_gather(h_ref, bufs[1 - par], sems.at[1 - par], tm)

    @pl.when(t >= n_used)
    def _():
        y_ref[...] = jnp.zeros(y_ref.shape, F32)


def _experts(tile_expert, next_expert, n_used, src_token, h, w_gate, w_up, w_down, layer):
    tm = TM_EXPERT
    n_slots = src_token.shape[0]
    any_spec = pl.BlockSpec(memory_space=pl.ANY)
    return pl.pallas_call(
        functools.partial(_expert_body, n_tiles=n_slots // tm, layer=layer),
        grid_spec=pltpu.PrefetchScalarGridSpec(
            num_scalar_prefetch=4,
            grid=(n_slots // tm,),
            in_specs=[any_spec, any_spec, any_spec, any_spec],
            out_specs=pl.BlockSpec((tm, D_MODEL), lambda t, te, ne, nu, src: (t, 0)),
            scratch_shapes=[pltpu.VMEM((tm, D_MODEL), F32), pltpu.VMEM((tm, D_MODEL), F32),
                            pltpu.SemaphoreType.DMA((2,)),
                            pltpu.VMEM((D_MODEL, D_EXPERT), F32), pltpu.VMEM((D_MODEL, D_EXPERT), F32),
                            pltpu.VMEM((D_EXPERT, D_MODEL), F32), pltpu.SemaphoreType.DMA(()),
                            pltpu.VMEM((D_MODEL, D_EXPERT), BF16), pltpu.VMEM((D_MODEL, D_EXPERT), BF16),
                            pltpu.VMEM((D_EXPERT, D_MODEL), BF16)],
        ),
        out_shape=jax.ShapeDtypeStruct((n_slots, D_MODEL), F32),
        compiler_params=pltpu.CompilerParams(dimension_semantics=("arbitrary",), vmem_limit_bytes=VMEM_LIMIT),
        name="experts",
    )(tile_expert, next_expert, n_used, src_token, h, w_gate, w_up, w_down)


def _combine_body(slot_ref, x_ref, r_ref, y_ref, g_ref, *rest, nt, n_steps, row_start, n_first, final_norm):
    tm = TM_COMBINE
    t = pl.program_id(0)
    o_refs, (b00, b01, b10, b11, sems) = rest[:-5], rest[-5:]
    bufs = ((b00, b01), (b10, b11))

    def start(step, par, straight_line):
        for k in range(2):
            _start_row_gather(slot_ref, k * nt + row_start(step), y_ref, bufs[par][k], sems.at[par], tm,
                              straight_line)

    def wait(par):
        for k in range(2):
            _wait_row_gather(y_ref, bufs[par][k], sems.at[par], tm)

    @pl.when(t == 0)
    def _():
        start(0, 0, False)

    nxt = jnp.minimum(t + 1, n_steps - 1)
    for par in range(2):
        @pl.when(t % 2 == par)
        def _():
            wait(par)
            start(nxt, 1 - par, True)
            r = r_ref[...]
            out = x_ref[...] + r[:, 2:3] * bufs[par][0][...] + r[:, 3:4] * bufs[par][1][...]
            if final_norm:
                out = _rms(out, g_ref[...])
            if len(o_refs) == 1:
                o_refs[0][...] = out
            else:
                @pl.when(t < n_first)
                def _():
                    o_refs[0][...] = out

                @pl.when(t >= n_first)
                def _():
                    o_refs[1][...] = out

        @pl.when((t == n_steps - 1) & (t % 2 == par))
        def _():
            wait(1 - par)


def _combine(slots, x, route, y, g, out_rows=None):
    nt = x.shape[0]
    tm = TM_COMBINE
    if out_rows is None:
        n_steps = n_first = nt // tm
        row_start = lambda t: t * tm
        out_specs = [pl.BlockSpec((tm, D_MODEL), lambda i, sl: (i, 0))]
        out_shape = [jax.ShapeDtypeStruct((nt, D_MODEL), F32)]
    else:
        nb, lp, first, count, npr = out_rows
        per = count // tm
        n_first = nb * per
        n_second = (nt - npr) // tm
        assert per * tm == count and n_second * tm == nt - npr and first % SUBLANES == 0
        n_steps = n_first + n_second
        row_start = lambda t: jnp.where(t < n_first, (t // per) * lp + first + (t % per) * tm,
                                        npr + (t - n_first) * tm)
        out_specs = [pl.BlockSpec((tm, D_MODEL), lambda i, sl: (jnp.minimum(i, n_first - 1), 0)),
                     pl.BlockSpec((tm, D_MODEL), lambda i, sl: (jnp.maximum(i - n_first, 0), 0))]
        out_shape = [jax.ShapeDtypeStruct((nb * count, D_MODEL), F32),
                     jax.ShapeDtypeStruct((nt - npr, D_MODEL), F32)]
    res = pl.pallas_call(
        functools.partial(_combine_body, nt=nt, n_steps=n_steps, row_start=row_start, n_first=n_first,
                          final_norm=out_rows is not None),
        grid_spec=pltpu.PrefetchScalarGridSpec(
            num_scalar_prefetch=1,
            grid=(n_steps,),
            in_specs=[
                pl.BlockSpec((pl.Element(tm), pl.Element(D_MODEL)),
                             lambda i, sl: (pl.multiple_of(row_start(i), SUBLANES), 0)),
                pl.BlockSpec((pl.Element(tm), pl.Element(LANES)),
                             lambda i, sl: (pl.multiple_of(row_start(i), SUBLANES), 0)),
                pl.BlockSpec(memory_space=pl.ANY),
                pl.BlockSpec((1, D_MODEL), lambda i, sl: (0, 0)),
            ],
            out_specs=out_specs,
            scratch_shapes=[pltpu.VMEM((tm, D_MODEL), F32)] * 4 + [pltpu.SemaphoreType.DMA((2,))],
        ),
        out_shape=out_shape,
        compiler_params=pltpu.CompilerParams(dimension_semantics=("arbitrary",), vmem_limit_bytes=VMEM_LIMIT),
        name="combine",
    )(slots, x, route, y, g)
    return res[0] if out_rows is None else res


def _dispatch_plan(route, valid):
    nt = route.shape[0]
    tm = TM_EXPERT
    n_valid = int(np.sum(valid))
    n_slots = -(-(2 * n_valid + N_EXPERTS * (tm - 1)) // tm) * tm
    n_tiles = n_slots // tm
    valid2 = jnp.asarray(np.concatenate([valid, valid]))
    eidx = jnp.concatenate([route[:, 0], route[:, 1]]).astype(jnp.int32)
    eidx = jnp.where(valid2, eidx, N_EXPERTS)
    onehot = (eidx[:, None] == jnp.arange(N_EXPERTS, dtype=jnp.int32)[None, :]).astype(jnp.int32)
    incl = jnp.cumsum(onehot, axis=0)
    counts = incl[-1]
    rank = jnp.sum((incl - 1) * onehot, axis=1)
    padded = ((counts + tm - 1) // tm) * tm
    ends = jnp.cumsum(padded)
    starts = ends - padded
    pair = jnp.arange(2 * nt, dtype=jnp.int32)
    slot = jnp.where(valid2, jnp.take(starts, jnp.minimum(eidx, N_EXPERTS - 1)) + rank, pair % n_slots)
    slot = slot.astype(jnp.int32)
    token = pair % nt
    dest = jnp.where(valid2, slot, n_slots)
    src_token = (jnp.arange(n_slots, dtype=jnp.int32) % nt).at[dest].set(token, mode="drop")
    n_used = (ends[-1] // tm).astype(jnp.int32)
    tile_start = jnp.arange(n_tiles, dtype=jnp.int32) * tm
    tile_expert = jnp.sum((tile_start[:, None] >= ends[None, :]).astype(jnp.int32), axis=1)
    last_expert = jnp.take(tile_expert, jnp.maximum(n_used - 1, 0))
    tile_expert = jnp.where(jnp.arange(n_tiles) < n_used, tile_expert, last_expert)
    tile_expert = jnp.minimum(tile_expert, N_EXPERTS - 1).astype(jnp.int32)
    ids = jnp.arange(N_EXPERTS, dtype=jnp.int32)
    later = (ids[None, :] > ids[:, None]) & (counts[None, :] > 0)
    next_nonempty = jnp.min(jnp.where(later, ids[None, :], N_EXPERTS), axis=1)
    next_expert = jnp.take(next_nonempty, tile_expert).astype(jnp.int32)
    return slot, src_token, tile_expert, next_expert, n_used.reshape(1)


def _moe(x, valid, g_ffn, wr, br, w_gate, w_up, w_down, layer, g_final, out_rows):
    h, route = _route(x, g_ffn, wr, br)
    slot, src_token, tile_expert, next_expert, n_used = _dispatch_plan(route, valid)
    y = _experts(tile_expert, next_expert, n_used, src_token, h, w_gate, w_up, w_down, layer)
    return _combine(slot, x, route, y, g_final, out_rows)


def _pool_windows(ext_rows, u, inv_cnt, w_ref, scale):
    outs = []
    for gi, w in enumerate(POOL_WINDOWS):
        c0 = gi * POOL_CH
        win = u[:, c0:c0 + POOL_CH]
        for j in range(1, w):
            win = win + ext_rows(j, c0)
        d = win * inv_cnt[gi] - u[:, c0:c0 + POOL_CH]
        outs.append(_dot(d.astype(BF16), w_ref[gi]))
    return jnp.concatenate(outs, axis=1) * scale


def _pool_prompt_body(xc_ref, xp_ref, g_ref, w_ref, sc_ref, y_ref, st_ref, ext, *, seq_len, nblk, n_prompt_blocks):
    step = pl.program_id(0)
    i = step % nblk
    tp = TP_POOL
    pad = POOL_BUF + 1
    last = seq_len - 1

    @pl.when(step < n_prompt_blocks)
    def _():
        x = xc_ref[...]
        u = _rms(x, g_ref[...])
        tail = _rms(xp_ref[pl.ds(tp - pad, pad), :], g_ref[...])
        ext[pl.ds(0, pad), :] = jnp.where(i > 0, tail, 0.0)
        ext[pl.ds(pad, tp), :] = u
        pos = i * tp + lax.broadcasted_iota(jnp.int32, (tp, 1), 0)
        inv_cnt = [1.0 / jnp.minimum(w, pos + 1).astype(F32) for w in POOL_WINDOWS]

        def ext_rows(j, c0):
            return ext[pl.ds(pad - j, tp), c0:c0 + POOL_CH]

        y_ref[...] = x + _pool_windows(ext_rows, u, inv_cnt, w_ref, sc_ref[...])

        @pl.when(i == last // tp)
        def _():
            st_ref[0] = ext[pl.ds(last % tp + 1, pad), :]

    @pl.when(step >= n_prompt_blocks)
    def _():
        y_ref[...] = xc_ref[...]


def _pool_prompt(x, g, w_bf, scale, nb, lp, seq_len):
    tp = TP_POOL
    nt = x.shape[0]
    nblk = lp // tp
    npb = nb * nblk
    pad = POOL_BUF + 1
    assert nt % tp == 0

    def prev_map(s):
        return (jnp.where((s % nblk == 0) | (s >= npb), s, s - 1), 0)

    return pl.pallas_call(
        functools.partial(_pool_prompt_body, seq_len=seq_len, nblk=nblk, n_prompt_blocks=npb),
        grid=(nt // tp,),
        in_specs=[
            pl.BlockSpec((tp, D_MODEL), lambda s: (s, 0)),
            pl.BlockSpec((tp, D_MODEL), prev_map),
            pl.BlockSpec((1, D_MODEL), lambda s: (0, 0)),
            pl.BlockSpec((len(POOL_WINDOWS), POOL_CH, POOL_CH), lambda s: (0, 0, 0)),
            pl.BlockSpec((1, D_MODEL), lambda s: (0, 0)),
        ],
        out_specs=[
            pl.BlockSpec((tp, D_MODEL), lambda s: (s, 0)),
            pl.BlockSpec((1, pad, D_MODEL), lambda s: (jnp.minimum(s // nblk, nb - 1), 0, 0)),
        ],
        out_shape=[jax.ShapeDtypeStruct((nt, D_MODEL), F32), jax.ShapeDtypeStruct((nb, pad, D_MODEL), F32)],
        scratch_shapes=[pltpu.VMEM((tp + pad, D_MODEL), F32)],
        compiler_params=pltpu.CompilerParams(dimension_semantics=("arbitrary",), vmem_limit_bytes=VMEM_LIMIT),
        name="pool_prompt",
    )(x, x, g, w_bf, scale)


def _pool_sample_body(x_ref, st_ref, g_ref, w_ref, sc_ref, y_ref, so_ref, *, dec_seq):
    g = g_ref[...]
    xs = [x_ref[:, t, :] for t in range(dec_seq)]
    us = [_rms(x, g) for x in xs]

    def ext_row(e, c0):
        if e < POOL_BUF:
            return st_ref[:, e, c0:c0 + POOL_CH]
        return us[e - POOL_BUF][:, c0:c0 + POOL_CH]

    for t in range(dec_seq):
        inv_cnt = [1.0 / w for w in POOL_WINDOWS]
        y = _pool_windows(lambda j, c0: ext_row(POOL_BUF + t - j, c0), us[t], inv_cnt, w_ref, sc_ref[...])
        y_ref[:, t, :] = xs[t] + y
    for e in range(POOL_BUF):
        src = e + dec_seq
        so_ref[:, e, :] = st_ref[:, src, :] if src < POOL_BUF else us[src - POOL_BUF]


def _pool_sample(x3, state, g, w_bf, scale):
    ns, dec_seq, _ = x3.shape
    ts = 32
    return pl.pallas_call(
        functools.partial(_pool_sample_body, dec_seq=dec_seq),
        grid=(ns // ts,),
        in_specs=[
            pl.BlockSpec((ts, dec_seq, D_MODEL), lambda i: (i, 0, 0)),
            pl.BlockSpec((ts, POOL_BUF, D_MODEL), lambda i: (i, 0, 0)),
            pl.BlockSpec((1, D_MODEL), lambda i: (0, 0)),
            pl.BlockSpec((len(POOL_WINDOWS), POOL_CH, POOL_CH), lambda i: (0, 0, 0)),
            pl.BlockSpec((1, D_MODEL), lambda i: (0, 0)),
        ],
        out_specs=[
            pl.BlockSpec((ts, dec_seq, D_MODEL), lambda i: (i, 0, 0)),
            pl.BlockSpec((ts, POOL_BUF, D_MODEL), lambda i: (i, 0, 0)),
        ],
        out_shape=[jax.ShapeDtypeStruct(x3.shape, F32), jax.ShapeDtypeStruct(state.shape, F32)],
        compiler_params=pltpu.CompilerParams(dimension_semantics=("arbitrary",), vmem_limit_bytes=VMEM_LIMIT),
        name="pool_sample",
    )(x3, state, g, w_bf, scale)


def _rope_tables(pos):
    half = ROT_DIM // 2
    inv = ROPE_THETA ** (-jnp.arange(0, ROT_DIM, 2, dtype=F32) / ROT_DIM)
    ang = pos.astype(F32)[:, None] * inv[None, :]
    cos, sin = jnp.cos(ang), jnp.sin(ang)
    n = pos.shape[0]
    ones = jnp.ones((n, DQK_A - ROT_DIM), F32)
    zeros = jnp.zeros((n, DQK_A - ROT_DIM), F32)
    zh = jnp.zeros((n, half), F32)
    c = jnp.concatenate([cos, cos, ones], axis=1)
    s_lo = jnp.concatenate([zh, sin, zeros], axis=1)
    s_hi = jnp.concatenate([-sin, zh, zeros], axis=1)
    return jnp.stack([jnp.tile(t, (1, LANES // DQK_A)) for t in (c, s_lo, s_hi)])


def kernel(x_prompt, x_sample, cache_k_a, cache_v_a, cache_k_b, cache_v_b, cache_logf_b, state_pool, page_table,
           meta_tokens, g_mix, w_in, b_forget, lambda_qk, g_subln, w_out, w_pool, pool_scale, g_ffn,
           w_route_group, b_route_group, w_route_expert, b_route_expert, w_gate, w_up, w_down, g_final):
    nb, seq, _ = x_prompt.shape
    ns, dec_seq, _ = x_sample.shape
    depth = g_mix.shape[0]
    seq_len = seq + N_META
    lp = -(-seq_len // TQ) * TQ
    npr = nb * lp
    n_samp = ns * dec_seq
    nt = npr + n_samp
    past = page_table.shape[1] * PAGE_SIZE
    n_phys = cache_k_a.shape[1]
    assert nt % TM_PROJ == 0 and lp % TP_POOL == 0 and dec_seq * HB <= LANES
    assert page_table.shape[1] % PAGES_PER_STEP == 0 and PAGE_SIZE * HB == SUBLANES * LANES

    pieces = []
    for b in range(nb):
        pieces += [meta_tokens, x_prompt[b], jnp.zeros((lp - seq_len, D_MODEL), F32)]
    x = jnp.concatenate(pieces + [x_sample.reshape(n_samp, D_MODEL)], axis=0)
    valid = np.concatenate([np.tile(np.arange(lp) < seq_len, nb), np.ones(n_samp, bool)])
    pos = jnp.concatenate([jnp.tile(jnp.arange(lp, dtype=jnp.int32), nb),
                           past + jnp.tile(jnp.arange(dec_seq, dtype=jnp.int32), ns)])
    rope = _rope_tables(pos)

    outs = {k: [] for k in ("ka_p", "va_p", "kb_p", "vb_p", "lf_p", "pool_p",
                            "ka_s", "va_s", "kb_s", "vb_s", "lf_s", "pool_s")}
    for l in range(depth):
        if l % 2 == 0:
            i = l // 2
            lam_init = 0.8 - 0.6 * math.exp(-0.3 * l)
            lq = lambda_qk[i]
            lam = (jnp.exp(jnp.sum(lq[0] * lq[1])) - jnp.exp(jnp.sum(lq[2] * lq[3])) + lam_init).reshape(1)
            w_main = w_in[i, :, :D_QKV].astype(BF16)
            w_f = jnp.pad(w_in[i, :, D_QKV:], ((0, 0), (0, LANES - HB))).astype(BF16)
            b_f = jnp.pad(b_forget[i], (0, LANES - HB)).reshape(1, LANES)
            g_sub = g_subln[i].reshape(1, DH)
            p, lf = _inproj(x, g_mix[l].reshape(1, D_MODEL), w_main, w_f, b_f, rope)

            p_sample = p[npr:].reshape(ns, dec_seq, 6, HA, DH).transpose(2, 0, 1, 3, 4)
            for name, c in (("ka", 1), ("va", 2), ("kb", 4), ("vb", 5)):
                rows = [p[b * lp:b * lp + seq_len, c * HA * DH:(c + 1) * HA * DH] for b in range(nb)]
                outs[name + "_p"].append(jnp.stack(rows).reshape(nb, seq_len, HA, DH))
                outs[name + "_s"].append(p_sample[c])
            outs["lf_p"].append(lf[:npr].reshape(nb, lp, LANES)[:, :seq_len, :HB])
            outs["lf_s"].append(lf[npr:, :HB].reshape(ns, dec_seq, HB))

            fc, fct = _fcum(lf[:npr], nb, lp)
            fk = fct[:, :HB].reshape(nb, HB, 1, lp)
            oa_p = _flash_a(p, lam, g_sub, nb, lp, lam_init)
            ob_p = _flash_b(p, fc, fk, nb, lp)

            samp = [p_sample[c].reshape(ns, dec_seq * HA, DH) for c in range(6)]
            lfn = jnp.pad(lf[npr:, :HB].reshape(ns, 1, dec_seq * HB), ((0, 0), (0, 0), (0, LANES - dec_seq * HB)))
            oa_s, ob_s = _sattn(
                page_table, lam, samp[0], samp[3], samp[1], samp[2], samp[4], samp[5], lfn, g_sub,
                cache_k_a[i].reshape(n_phys, PAGE_SIZE * HA, DH), cache_v_a[i].reshape(n_phys, PAGE_SIZE * HA, DH),
                cache_k_b[i].reshape(n_phys, PAGE_SIZE * HB, DH), cache_v_b[i].reshape(n_phys, PAGE_SIZE * HB, DH),
                cache_logf_b[i].reshape(n_phys, SUBLANES, LANES), lam_init)
            wo = w_out[i].astype(BF16)
            x = _outproj(x, oa_p, ob_p, oa_s.reshape(n_samp, HA * DH), ob_s.reshape(n_samp, HB * DH),
                         wo[:HA * DH], wo[HA * DH:])
        else:
            j = l // 2
            g = g_mix[l].reshape(1, D_MODEL)
            wp = w_pool[j].astype(BF16)
            sc = pool_scale[j].reshape(1, D_MODEL)
            ys, st_s = _pool_sample(x[npr:].reshape(ns, dec_seq, D_MODEL), state_pool[j], g, wp, sc)
            y, st_p = _pool_prompt(x, g, wp, sc, nb, lp, seq_len)
            outs["pool_p"].append(st_p[:, 1:])
            outs["pool_s"].append(st_s)
            x = lax.dynamic_update_slice(y, ys.reshape(n_samp, D_MODEL), (npr, 0))

        wr = jnp.concatenate([w_route_group[l], w_route_expert[l].reshape(D_MODEL, N_EXPERTS),
                              jnp.zeros((D_MODEL, LANES - N_GROUPS - N_EXPERTS), F32)], axis=1)
        wr_hi = wr.astype(BF16)
        wr = jnp.concatenate([wr_hi, (wr - wr_hi.astype(F32)).astype(BF16)], axis=1)
        br =jnp.concatenate([b_route_group[l], b_route_expert[l].reshape(N_EXPERTS),
                              jnp.zeros((LANES - N_GROUPS - N_EXPERTS,), F32)]).reshape(1, LANES)
        out_rows = (nb, lp, N_META, seq, npr) if l == depth - 1 else None
        x = _moe(x, valid, g_ffn[l].reshape(1, D_MODEL), wr, br, w_gate, w_up, w_down, l,
                 g_final.reshape(1, D_MODEL), out_rows)
    y_prompt = x[0].reshape(nb, seq, D_MODEL)
    y_sample = x[1].reshape(ns, dec_seq, D_MODEL)
    st = lambda k: jnp.stack(outs[k])
    return (y_prompt, y_sample, st("ka_p"), st("va_p"), st("kb_p"), st("vb_p"), st("lf_p"), st("pool_p"),
            st("ka_s"), st("va_s"), st("kb_s"), st("vb_s"), st("lf_s"), st("pool_s"))
```

```python
import functools
import math

import jax
import jax.numpy as jnp
import numpy as np
from jax import lax
from jax.experimental import pallas as pl
from jax.experimental.pallas import tpu as pltpu

F32 = jnp.float32
BF16 = jnp.bfloat16

D_MODEL = 2048
N_META = 16
HA = 8
HB = 8
DH = 128
DQK_A = 64
ROT_DIM = 16
ROPE_THETA = 500000.0
POOL_WINDOWS = (2, 4, 8, 16)
POOL_CH = D_MODEL // len(POOL_WINDOWS)
POOL_BUF = max(POOL_WINDOWS) - 1
N_GROUPS = 4
EXPERTS_PER_GROUP = 8
N_EXPERTS = N_GROUPS * EXPERTS_PER_GROUP
D_EXPERT = D_MODEL // 4
EPS = 1e-6
NEG_INF = -1e30
PAGE_SIZE = 128
D_QKV = 6 * HA * DH

LANES = 128
SUBLANES = 8
VMEM_LIMIT = 56 * 1024 * 1024

TQ = 256
TM_PROJ = 512
TN_PROJ = 1024
TM_ROUTE = 256
TM_EXPERT = 256
TM_COMBINE = 256
TP_POOL = 256
PAGES_PER_STEP = 8
DMA_UNROLL = 8


def _rms(x, g):
    return x * lax.rsqrt(jnp.mean(x * x, axis=-1, keepdims=True) + EPS) * g


def _dot(a, b):
    return jnp.dot(a, b, preferred_element_type=F32)


def _dot_nt(a, b):
    return lax.dot_general(a, b, (((1,), (1,)), ((), ())), preferred_element_type=F32)


def _inproj_body(x_ref, g_ref, w_ref, wf_ref, bf_ref, rope_ref, p_ref, lf_ref, h_scr, *, n_rope_tiles, tn):
    j = pl.program_id(1)

    @pl.when(j == 0)
    def _():
        hb = _rms(x_ref[...], g_ref[...]).astype(BF16)
        h_scr[...] = hb
        fl = _dot(hb, wf_ref[...]) + bf_ref[...]
        lf_ref[...] = jnp.minimum(fl, 0.0) - jnp.log1p(jnp.exp(-jnp.abs(fl)))

    @pl.when(j < n_rope_tiles)
    def _():
        c = rope_ref[0]
        s_lo = rope_ref[1]
        s_hi = rope_ref[2]
        wide = 2 * LANES
        for cw in range(tn // wide):
            acc = _dot(h_scr[...], w_ref[:, cw * wide:(cw + 1) * wide])
            for cc in range(2):
                a = acc[:, cc * LANES:(cc + 1) * LANES]
                lo = cw * wide + cc * LANES
                p_ref[:, lo:lo + LANES] = (
                    a * c + pltpu.roll(a, ROT_DIM // 2, 1) * s_lo + pltpu.roll(a, LANES - ROT_DIM // 2, 1) * s_hi)

    @pl.when(j >= n_rope_tiles)
    def _():
        p_ref[...] = _dot(h_scr[...], w_ref[...])


def _inproj(x, g, w_bf, wf_bf, bf, rope):
    nt = x.shape[0]
    tm, tn = TM_PROJ, TN_PROJ
    n_rope_tiles = (2 * HA * DH) // tn
    return pl.pallas_call(
        functools.partial(_inproj_body, n_rope_tiles=n_rope_tiles, tn=tn),
        grid=(nt // tm, D_QKV // tn),
        in_specs=[
            pl.BlockSpec((tm, D_MODEL), lambda i, j: (i, 0)),
            pl.BlockSpec((1, D_MODEL), lambda i, j: (0, 0)),
            pl.BlockSpec((D_MODEL, tn), lambda i, j: (0, j)),
            pl.BlockSpec((D_MODEL, LANES), lambda i, j: (0, 0)),
            pl.BlockSpec((1, LANES), lambda i, j: (0, 0)),
            pl.BlockSpec((3, tm, LANES), lambda i, j: (0, i, 0)),
        ],
        out_specs=[
            pl.BlockSpec((tm, tn), lambda i, j: (i, j)),
            pl.BlockSpec((tm, LANES), lambda i, j: (i, 0)),
        ],
        out_shape=[jax.ShapeDtypeStruct((nt, D_QKV), F32), jax.ShapeDtypeStruct((nt, LANES), F32)],
        scratch_shapes=[pltpu.VMEM((tm, D_MODEL), BF16)],
        compiler_params=pltpu.CompilerParams(
            dimension_semantics=("arbitrary", "arbitrary"), vmem_limit_bytes=VMEM_LIMIT),
        name="inproj",
    )(x, g, w_bf, wf_bf, bf, rope)


def _fcum_body(lf_ref, fc_ref, fct_ref, *, lp):
    blk = TQ
    row = lax.broadcasted_iota(jnp.int32, (blk, blk), 0)
    col = lax.broadcasted_iota(jnp.int32, (blk, blk), 1)
    tri = jnp.where(row >= col, 1.0, 0.0).astype(BF16)
    carry = jnp.zeros((1, LANES), F32)
    for b in range(lp // blk):
        x = lf_ref[pl.ds(b * blk, blk), :]
        hi = x.astype(BF16)
        r1 = x - hi.astype(F32)
        mid = r1.astype(BF16)
        lo = (r1 - mid.astype(F32)).astype(BF16)
        c = _dot(tri, hi) + _dot(tri, mid) + _dot(tri, lo) + carry
        fc_ref[pl.ds(b * blk, blk), :] = c
        carry = c[blk - 1:blk, :]
    fct_ref[0] = fc_ref[...].T


def _fcum(lf, nb, lp):
    return pl.pallas_call(
        functools.partial(_fcum_body, lp=lp),
        grid=(nb,),
        in_specs=[pl.BlockSpec((lp, LANES), lambda b: (b, 0))],
        out_specs=[pl.BlockSpec((lp, LANES), lambda b: (b, 0)),
                   pl.BlockSpec((1, LANES, lp), lambda b: (b, 0, 0))],
        out_shape=[jax.ShapeDtypeStruct((nb * lp, LANES), F32), jax.ShapeDtypeStruct((nb, LANES, lp), F32)],
        name="fcum",
    )(lf)


def _split_maps(q):
    lane = lax.broadcasted_iota(jnp.int32, q.shape, 1)
    return jnp.concatenate([jnp.where(lane < DQK_A, q, 0.0), jnp.where(lane >= DQK_A, q, 0.0)], axis=0)


def _subln(o, g, lam_init):
    return _rms(o, g) * (1.0 - lam_init)


def _block_rows(i, seq_len):
    return min(TQ, seq_len - i * TQ)


def _causal_rows(qs, kb, vb, i, rows, bias=None):
    n = (i + 1) * TQ
    s = _dot_nt(qs, kb[:n])
    if bias is not None:
        s = s + bias(n)
    row = lax.broadcasted_iota(jnp.int32, (qs.shape[0], TQ), 0) % rows
    col = lax.broadcasted_iota(jnp.int32, (qs.shape[0], TQ), 1)
    diag = jnp.where(col <= row, s[:, n - TQ:], NEG_INF)
    s = diag if i == 0 else jnp.concatenate([s[:, :n - TQ], diag], axis=1)
    p = jnp.exp(s - jnp.max(s, axis=1, keepdims=True))
    return _dot(p.astype(BF16), vb[:n]) / jnp.sum(p, axis=1, keepdims=True)


def _flash_a_body(lam_ref, q_ref, k_ref, v_ref, g_ref, o_ref, *, lam_init, nq, seq_len):
    kb = k_ref[...].astype(BF16)
    vb = v_ref[...].astype(BF16)
    for i in range(nq):
        rows = _block_rows(i, seq_len)
        qs = _split_maps(q_ref[pl.ds(i * TQ, rows), :] * (DQK_A ** -0.5)).astype(BF16)
        o = _causal_rows(qs, kb, vb, i, rows)
        o = o[:rows] - lam_ref[0] * o[rows:]
        o_ref[pl.ds(i * TQ, rows), :] = _subln(o, g_ref[...], lam_init)
    if nq * TQ > seq_len:
        o_ref[pl.ds(seq_len, nq * TQ - seq_len), :] = jnp.zeros((nq * TQ - seq_len, DH), F32)


def _flash_b_body(q_ref, k_ref, v_ref, fq_ref, fk_ref, o_ref, *, nq, seq_len):
    h = pl.program_id(1)
    kb = k_ref[...].astype(BF16)
    vb = v_ref[...].astype(BF16)
    fk = fk_ref[0, 0]
    for i in range(nq):
        rows = _block_rows(i, seq_len)
        lane = lax.broadcasted_iota(jnp.int32, (rows, LANES), 1)
        qs = (q_ref[pl.ds(i * TQ, rows), :] * (DH ** -0.5)).astype(BF16)
        fq_col = jnp.sum(jnp.where(lane == h, fq_ref[pl.ds(i * TQ, rows), :], 0.0), axis=1, keepdims=True)
        o_ref[pl.ds(i * TQ, rows), :] = _causal_rows(qs, kb, vb, i, rows, bias=lambda n: fq_col - fk[:, :n])
    if nq * TQ > seq_len:
        o_ref[pl.ds(seq_len, nq * TQ - seq_len), :] = jnp.zeros((nq * TQ - seq_len, DH), F32)


def _flash_a(p, lam, g_sub, nb, lp, seq_len, lam_init):
    nq = lp // TQ
    qcol, kcol, vcol = 0, HA, 2 * HA
    seq = lambda c: pl.BlockSpec((lp, DH), lambda b, h: (b, c + h))
    return pl.pallas_call(
        functools.partial(_flash_a_body, lam_init=lam_init, nq=nq, seq_len=seq_len),
        grid=(nb, HA),
        in_specs=[pl.BlockSpec(memory_space=pltpu.SMEM), seq(qcol), seq(kcol), seq(vcol),
                  pl.BlockSpec((1, DH), lambda b, h: (0, 0))],
        out_specs=pl.BlockSpec((lp, DH), lambda b, h: (b, h)),
        out_shape=jax.ShapeDtypeStruct((nb * lp, HA * DH), F32),
        compiler_params=pltpu.CompilerParams(
            dimension_semantics=("arbitrary", "arbitrary"), vmem_limit_bytes=VMEM_LIMIT),
        name="flash_a",
    )(lam, p, p, p, g_sub)


def _flash_b(p, fc, fk, nb, lp, seq_len):
    nq = lp // TQ
    qcol, kcol, vcol = 3 * HA, 4 * HA, 5 * HA
    seq = lambda c: pl.BlockSpec((lp, DH), lambda b, h: (b, c + h))
    return pl.pallas_call(
        functools.partial(_flash_b_body, nq=nq, seq_len=seq_len),
        grid=(nb, HB),
        in_specs=[seq(qcol), seq(kcol), seq(vcol),
                  pl.BlockSpec((lp, LANES), lambda b, h: (b, 0)),
                  pl.BlockSpec((1, 1, 1, lp), lambda b, h: (b, h, 0, 0))],
        out_specs=pl.BlockSpec((lp, DH), lambda b, h: (b, h)),
        out_shape=jax.ShapeDtypeStruct((nb * lp, HB * DH), F32),
        compiler_params=pltpu.CompilerParams(
            dimension_semantics=("arbitrary", "arbitrary"), vmem_limit_bytes=VMEM_LIMIT),
        name="flash_b",
    )(p, p, p, fc, fk)


def _online_update(s, pv, m_ref, l_ref, acc_ref):
    m_prev = m_ref[...]
    m_new = jnp.maximum(m_prev, jnp.max(s, axis=1, keepdims=True))
    alpha = jnp.exp(m_prev - m_new)
    p = jnp.exp(s - m_new)
    l_ref[...] = alpha * l_ref[...] + jnp.sum(p, axis=1, keepdims=True)
    acc_ref[...] = alpha * acc_ref[...] + pv(p.astype(BF16))
    m_ref[...] = m_new


def _page_suffix_bias(x, carry):
    lane = lax.broadcasted_iota(jnp.int32, x.shape, 1)
    row = lax.broadcasted_iota(jnp.int32, x.shape, 0)
    y = x
    t = x
    for s in (8, 16, 32, 64):
        y = y + jnp.where(lane + s < LANES, pltpu.roll(y, LANES - s, 1), 0.0)
        t = t + pltpu.roll(t, s, 1)
    z = t
    for s in (1, 2, 4):
        z = z + jnp.where(row + s < SUBLANES, pltpu.roll(z, SUBLANES - s, 0), 0.0)
    g = (y - x) + (z - t) + carry
    return g, carry + z[0:1, :]


def _sattn_body(*refs, n_steps, dec_seq, lam_init):
    pps = PAGES_PER_STEP
    (_, lam_ref, qa_ref, qb_ref, kan_ref, van_ref, kbn_ref, vbn_ref, lfn_ref, g_ref), refs = refs[:10], refs[10:]
    ka_refs, va_refs, kb_refs, vb_refs, lf_refs = (refs[j * pps:(j + 1) * pps] for j in range(5))
    oa_ref, ob_ref, qa_s, qb_s, ma, la, acca, mb, lb, accb, carry = refs[5 * pps:]
    pstep = pl.program_id(1)
    nrow = dec_seq * HB
    ncol = PAGE_SIZE * HB

    @pl.when(pstep == 0)
    def _():
        qa_s[...] = _split_maps(qa_ref[0] * (DQK_A ** -0.5)).astype(BF16)
        qb_s[...] = (qb_ref[0] * (DH ** -0.5)).astype(BF16)
        for m_ref, l_ref, acc_ref in ((ma, la, acca), (mb, lb, accb)):
            m_ref[...] = jnp.full(m_ref.shape, NEG_INF, F32)
            l_ref[...] = jnp.zeros(l_ref.shape, F32)
            acc_ref[...] = jnp.zeros(acc_ref.shape, F32)
        carry[...] = jnp.zeros(carry.shape, F32)

    def head_match(nr):
        r = lax.broadcasted_iota(jnp.int32, (nr, ncol), 0)
        c = lax.broadcasted_iota(jnp.int32, (nr, ncol), 1)
        return (r % HB) == (c % HB)

    def paged(q_s, k_refs, v_refs, biases, m_ref, l_ref, acc_ref):
        hm = head_match(q_s.shape[0])
        cols = []
        for j in range(pps):
            s = _dot_nt(q_s[...], k_refs[j][0].astype(BF16))
            if biases is not None:
                s = s + biases[j]
            cols.append(jnp.where(hm, s, NEG_INF))

        def pv(p):
            out = _dot(p[:, :ncol], v_refs[0][0].astype(BF16))
            for j in range(1, pps):
                out = out + _dot(p[:, j * ncol:(j + 1) * ncol], v_refs[j][0].astype(BF16))
            return out

        _online_update(jnp.concatenate(cols, axis=1), pv, m_ref, l_ref, acc_ref)

    paged(qa_s, ka_refs, va_refs, None, ma, la, acca)

    biases = []
    c = carry[...]
    for j in range(pps):
        g, c = _page_suffix_bias(lf_refs[j][0], c)
        biases.append(jnp.concatenate([g[r:r + 1, :] for r in range(SUBLANES)], axis=1))
    carry[...] = c
    paged(qb_s, kb_refs, vb_refs, biases, mb, lb, accb)

    @pl.when(pstep == n_steps - 1)
    def _():
        def new_mask(shape):
            r = lax.broadcasted_iota(jnp.int32, shape, 0)
            c = lax.broadcasted_iota(jnp.int32, shape, 1)
            return ((r % HB) == (c % HB)) & ((c // HB) <= ((r // HB) % dec_seq))

        s = _dot_nt(qa_s[...], kan_ref[0].astype(BF16))
        s = jnp.where(new_mask(s.shape), s, NEG_INF)
        _online_update(s, lambda p: _dot(p, van_ref[0].astype(BF16)), ma, la, acca)

        x = lfn_ref[0]
        cs = x
        for t in range(1, dec_seq):
            cs = cs + pltpu.roll(x, t * HB, 1)
        s = _dot_nt(qb_s[...], kbn_ref[0].astype(BF16)) - cs[:, :nrow]
        s = jnp.where(new_mask(s.shape), s, NEG_INF)
        _online_update(s, lambda p: _dot(p, vbn_ref[0].astype(BF16)), mb, lb, accb)

        o = acca[...] / la[...]
        o = o[:nrow] - lam_ref[0] * o[nrow:]
        oa_ref[0] = _subln(o, g_ref[...], lam_init)
        ob_ref[0] = accb[...] / lb[...]


def _sattn(page_table, lam, qa, qb, kan, van, kbn, vbn, lfn, g_sub, ck_a, cv_a, ck_b, cv_b, c_lf, lam_init):
    ns, n_pages = page_table.shape
    pps = PAGES_PER_STEP
    n_steps = n_pages // pps
    nrow = qa.shape[1]
    dec_seq = nrow // HB
    ncol = PAGE_SIZE * HB
    pt_flat = page_table.reshape(-1)

    def seq_map(b, p, pt):
        return (b, 0, 0)

    def page_map(j):
        return lambda b, p, pt: (pt[b * n_pages + (n_pages - 1 - (p * pps + j))], 0, 0)

    seq_spec = pl.BlockSpec((1, nrow, DH), seq_map)
    page_specs = [pl.BlockSpec((1, ncol, DH), page_map(j)) for j in range(pps)]
    lf_specs = [pl.BlockSpec((1, SUBLANES, LANES), page_map(j)) for j in range(pps)]
    return pl.pallas_call(
        functools.partial(_sattn_body, n_steps=n_steps, dec_seq=dec_seq, lam_init=lam_init),
        grid_spec=pltpu.PrefetchScalarGridSpec(
            num_scalar_prefetch=1,
            grid=(ns, n_steps),
            in_specs=[
                pl.BlockSpec(memory_space=pltpu.SMEM),
                seq_spec, seq_spec, seq_spec, seq_spec, seq_spec, seq_spec,
                pl.BlockSpec((1, 1, LANES), seq_map),
                pl.BlockSpec((1, DH), lambda b, p, pt: (0, 0)),
            ] + page_specs * 4 + lf_specs,
            out_specs=[seq_spec, seq_spec],
            scratch_shapes=[
                pltpu.VMEM((2 * nrow, DH), BF16), pltpu.VMEM((nrow, DH), BF16),
                pltpu.VMEM((2 * nrow, 1), F32), pltpu.VMEM((2 * nrow, 1), F32), pltpu.VMEM((2 * nrow, DH), F32),
                pltpu.VMEM((nrow, 1), F32), pltpu.VMEM((nrow, 1), F32), pltpu.VMEM((nrow, DH), F32),
                pltpu.VMEM((1, LANES), F32),
            ],
        ),
        out_shape=[jax.ShapeDtypeStruct((ns, nrow, DH), F32), jax.ShapeDtypeStruct((ns, nrow, DH), F32)],
        compiler_params=pltpu.CompilerParams(
            dimension_semantics=("arbitrary", "arbitrary"), vmem_limit_bytes=VMEM_LIMIT),
        name="sattn",
    )(pt_flat, lam, qa, qb, kan, van, kbn, vbn, lfn, g_sub,
      *([ck_a] * pps), *([cv_a] * pps), *([ck_b] * pps), *([cv_b] * pps), *([c_lf] * pps))


def _outproj_body(x_ref, oap_ref, obp_ref, oas_ref, obs_ref, wa_ref, wb_ref, y_ref, *, n_prompt_tiles):
    is_sample = pl.program_id(0) >= n_prompt_tiles
    oa = jnp.where(is_sample, oas_ref[...], oap_ref[...]).astype(BF16)
    ob = jnp.where(is_sample, obs_ref[...], obp_ref[...]).astype(BF16)
    y_ref[...] = x_ref[...] + _dot(oa, wa_ref[...]) + _dot(ob, wb_ref[...])


def _outproj(x, oa_p, ob_p, oa_s, ob_s, wa_bf, wb_bf):
    nt = x.shape[0]
    tm, tn = TM_PROJ // 2, D_MODEL
    ka, kb = oa_p.shape[1], ob_p.shape[1]
    npt = oa_p.shape[0] // tm
    nst = oa_s.shape[0] // tm
    assert npt * tm == oa_p.shape[0] and nst * tm == oa_s.shape[0] and (npt + nst) * tm == nt
    p_map = lambda i, j: (jnp.minimum(i, npt - 1), 0)
    s_map = lambda i, j: (jnp.maximum(i - npt, 0), 0)
    return pl.pallas_call(
        functools.partial(_outproj_body, n_prompt_tiles=npt),
        grid=(nt // tm, D_MODEL // tn),
        in_specs=[
            pl.BlockSpec((tm, tn), lambda i, j: (i, j)),
            pl.BlockSpec((tm, ka), p_map),
            pl.BlockSpec((tm, kb), p_map),
            pl.BlockSpec((tm, ka), s_map),
            pl.BlockSpec((tm, kb), s_map),
            pl.BlockSpec((ka, tn), lambda i, j: (0, j)),
            pl.BlockSpec((kb, tn), lambda i, j: (0, j)),
        ],
        out_specs=pl.BlockSpec((tm, tn), lambda i, j: (i, j)),
        out_shape=jax.ShapeDtypeStruct((nt, D_MODEL), F32),
        compiler_params=pltpu.CompilerParams(
            dimension_semantics=("arbitrary", "arbitrary"), vmem_limit_bytes=VMEM_LIMIT),
        name="outproj",
    )(x, oa_p, ob_p, oa_s, ob_s, wa_bf, wb_bf)


def _route_body(x_ref, g_ref, wr_ref, br_ref, h_ref, r_ref):
    h = _rms(x_ref[...], g_ref[...])
    h_ref[...] = h
    hi = h.astype(BF16)
    lo = (h - hi.astype(F32)).astype(BF16)
    wr = wr_ref[...]
    r_hi = _dot(hi, wr)
    lg = r_hi[:, :LANES] + (r_hi[:, LANES:] + _dot(lo, wr[:, :LANES])) + br_ref[...]
    lane = lax.broadcasted_iota(jnp.int32, lg.shape, 1).astype(F32)
    big = float(LANES)

    def first_max(vals, valid):
        v = jnp.where(valid, vals, NEG_INF)
        top = jnp.max(v, axis=1, keepdims=True)
        idx = jnp.min(jnp.where(valid & (v == top), lane, big), axis=1, keepdims=True)
        return top, idx

    is_grp = lane < N_GROUPS
    gmax, grp = first_max(lg, is_grp)
    p_grp = 1.0 / jnp.sum(jnp.where(is_grp, jnp.exp(lg - gmax), 0.0), axis=1, keepdims=True)
    lo = N_GROUPS + EXPERTS_PER_GROUP * grp
    in_grp = (lane >= lo) & (lane < lo + EXPERTS_PER_GROUP)
    v0, i0 = first_max(lg, in_grp)
    v1, i1 = first_max(lg, in_grp & (lane != i0))
    e = jnp.exp(v1 - v0)
    w0 = p_grp / (1.0 + e)
    w1 = p_grp * e / (1.0 + e)
    r = jnp.where(lane == 0, i0 - N_GROUPS, 0.0)
    r = jnp.where(lane == 1, i1 - N_GROUPS, r)
    r = jnp.where(lane == 2, w0, r)
    r = jnp.where(lane == 3, w1, r)
    r_ref[...] = r


def _route(x, g, wr, br):
    nt = x.shape[0]
    tm = TM_ROUTE
    return pl.pallas_call(
        _route_body,
        grid=(nt // tm,),
        in_specs=[
            pl.BlockSpec((tm, D_MODEL), lambda i: (i, 0)),
            pl.BlockSpec((1, D_MODEL), lambda i: (0, 0)),
            pl.BlockSpec((D_MODEL, 2 * LANES), lambda i: (0, 0)),
            pl.BlockSpec((1, LANES), lambda i: (0, 0)),
        ],
        out_specs=[pl.BlockSpec((tm, D_MODEL), lambda i: (i, 0)), pl.BlockSpec((tm, LANES), lambda i: (i, 0))],
        out_shape=[jax.ShapeDtypeStruct((nt, D_MODEL), F32), jax.ShapeDtypeStruct((nt, LANES), F32)],
        compiler_params=pltpu.CompilerParams(dimension_semantics=("arbitrary",), vmem_limit_bytes=VMEM_LIMIT),
        name="route",
    )(x, g, wr, br)


def _start_row_gather(idx_ref, idx_base, src_ref, dst_ref, sem, n_rows, straight_line=False):
    def copy(r):
        pltpu.make_async_copy(src_ref.at[pl.ds(idx_ref[idx_base + r], 1)],
                              dst_ref.at[pl.ds(r, 1)], sem).start()

    if straight_line:
        for r in range(n_rows):
            copy(r)
        return

    def issue(c, carry):
        for u in range(DMA_UNROLL):
            copy(c * DMA_UNROLL + u)
        return carry

    lax.fori_loop(0, n_rows // DMA_UNROLL, issue, 0)


def _wait_row_gather(src_ref, dst_ref, sem, n_rows):
    pltpu.make_async_copy(src_ref.at[pl.ds(0, n_rows)], dst_ref, sem).wait()


def _expert_body(te_ref, ne_ref, nu_ref, src_ref, h_ref, wg_ref, wu_ref, wd_ref, y_ref, hbuf0, hbuf1, sems,
                 wg_f, wu_f, wd_f, wsem, wg_s, wu_s, wd_s, *, n_tiles, layer):
    tm = TM_EXPERT
    t = pl.program_id(0)
    n_used = nu_ref[0]
    bufs = (hbuf0, hbuf1)
    weights = ((wg_ref, wg_f, wg_s), (wu_ref, wu_f, wu_s), (wd_ref, wd_f, wd_s))

    def weight_copies(e):
        return [pltpu.make_async_copy(w_ref.at[layer, e], w_f, wsem) for w_ref, w_f, _ in weights]

    @pl.when(t == 0)
    def _():
        _start_row_gather(src_ref, 0, h_ref, hbuf0, sems.at[0], tm)
        for c in weight_copies(te_ref[0]):
            c.start()

    @pl.when(t < n_used)
    def _():
        e = te_ref[t]
        prev = te_ref[jnp.maximum(t - 1, 0)]

        @pl.when((t == 0) | (e != prev))
        def _():
            for c in weight_copies(e):
                c.wait()
            for _, w_f, w_s in weights:
                w_s[...] = w_f[...].astype(BF16)
            nxt_e = ne_ref[t]

            @pl.when(nxt_e < N_EXPERTS)
            def _():
                for c in weight_copies(nxt_e):
                    c.start()

    nxt = jnp.minimum(t + 1, n_tiles - 1)
    for par in range(2):
        @pl.when((t < n_used) & (t % 2 == par))
        def _():
            _wait_row_gather(h_ref, bufs[par], sems.at[par], tm)
            hb = bufs[par][...].astype(BF16)
            _start_row_gather(src_ref, nxt * tm, h_ref, bufs[1 - par], sems.at[1 - par], tm, straight_line=True)
            hg = _dot(hb, wg_s[...])
            hu = _dot(hb, wu_s[...])
            act = hg * jax.nn.sigmoid(hg) * hu
            y_ref[...] = _dot(act.astype(BF16), wd_s[...])

        @pl.when((t == n_used) & (t % 2 == par))
        def _():
            _wait_row_gather(h_ref, bufs[par], sems.at[par], tm)

        @pl.when((t == n_tiles - 1) & (t < n_used) & (t % 2 == par))
        def _():
            _wait_row_gather(h_ref, bufs[1 - par], sems.at[1 - par], tm)

    @pl.when(t >= n_used)
    def _():
        y_ref[...] = jnp.zeros(y_ref.shape, F32)


def _experts(tile_expert, next_expert, n_used, src_token, h, w_gate, w_up, w_down, layer):
    tm = TM_EXPERT
    n_slots = src_token.shape[0]
    any_spec = pl.BlockSpec(memory_space=pl.ANY)
    return pl.pallas_call(
        functools.partial(_expert_body, n_tiles=n_slots // tm, layer=layer),
        grid_spec=pltpu.PrefetchScalarGridSpec(
            num_scalar_prefetch=4,
            grid=(n_slots // tm,),
            in_specs=[any_spec, any_spec, any_spec, any_spec],
            out_specs=pl.BlockSpec((tm, D_MODEL), lambda t, te, ne, nu, src: (t, 0)),
            scratch_shapes=[pltpu.VMEM((tm, D_MODEL), F32), pltpu.VMEM((tm, D_MODEL), F32),
                            pltpu.SemaphoreType.DMA((2,)),
                            pltpu.VMEM((D_MODEL, D_EXPERT), F32), pltpu.VMEM((D_MODEL, D_EXPERT), F32),
                            pltpu.VMEM((D_EXPERT, D_MODEL), F32), pltpu.SemaphoreType.DMA(()),
                            pltpu.VMEM((D_MODEL, D_EXPERT), BF16), pltpu.VMEM((D_MODEL, D_EXPERT), BF16),
                            pltpu.VMEM((D_EXPERT, D_MODEL), BF16)],
        ),
        out_shape=jax.ShapeDtypeStruct((n_slots, D_MODEL), F32),
        compiler_params=pltpu.CompilerParams(dimension_semantics=("arbitrary",), vmem_limit_bytes=VMEM_LIMIT),
        name="experts",
    )(tile_expert, next_expert, n_used, src_token, h, w_gate, w_up, w_down)


def _combine_body(slot_ref, x_ref, r_ref, y_ref, g_ref, *rest, nt, n_steps, row_start, n_first, final_norm):
    tm = TM_COMBINE
    t = pl.program_id(0)
    o_refs, (b00, b01, b10, b11, sems) = rest[:-5], rest[-5:]
    bufs = ((b00, b01), (b10, b11))

    def start(step, par, straight_line):
        for k in range(2):
            _start_row_gather(slot_ref, k * nt + row_start(step), y_ref, bufs[par][k], sems.at[par], tm,
                              straight_line)

    def wait(par):
        for k in range(2):
            _wait_row_gather(y_ref, bufs[par][k], sems.at[par], tm)

    @pl.when(t == 0)
    def _():
        start(0, 0, False)

    nxt = jnp.minimum(t + 1, n_steps - 1)
    for par in range(2):
        @pl.when(t % 2 == par)
        def _():
            wait(par)
            start(nxt, 1 - par, True)
            r = r_ref[...]
            out = x_ref[...] + r[:, 2:3] * bufs[par][0][...] + r[:, 3:4] * bufs[par][1][...]
            if final_norm:
                out = _rms(out, g_ref[...])
            if len(o_refs) == 1:
                o_refs[0][...] = out
            else:
                @pl.when(t < n_first)
                def _():
                    o_refs[0][...] = out

                @pl.when(t >= n_first)
                def _():
                    o_refs[1][...] = out

        @pl.when((t == n_steps - 1) & (t % 2 == par))
        def _():
            wait(1 - par)


def _combine(slots, x, route, y, g, out_rows=None):
    nt = x.shape[0]
    tm = TM_COMBINE
    if out_rows is None:
        n_steps = n_first = nt // tm
        row_start = lambda t: t * tm
        out_specs = [pl.BlockSpec((tm, D_MODEL), lambda i, sl: (i, 0))]
        out_shape = [jax.ShapeDtypeStruct((nt, D_MODEL), F32)]
    else:
        nb, lp, first, count, npr = out_rows
        per = count // tm
        n_first = nb * per
        n_second = (nt - npr) // tm
        assert per * tm == count and n_second * tm == nt - npr and first % SUBLANES == 0
        n_steps = n_first + n_second
        row_start = lambda t: jnp.where(t < n_first, (t // per) * lp + first + (t % per) * tm,
                                        npr + (t - n_first) * tm)
        out_specs = [pl.BlockSpec((tm, D_MODEL), lambda i, sl: (jnp.minimum(i, n_first - 1), 0)),
                     pl.BlockSpec((tm, D_MODEL), lambda i, sl: (jnp.maximum(i - n_first, 0), 0))]
        out_shape = [jax.ShapeDtypeStruct((nb * count, D_MODEL), F32),
                     jax.ShapeDtypeStruct((nt - npr, D_MODEL), F32)]
    res = pl.pallas_call(
        functools.partial(_combine_body, nt=nt, n_steps=n_steps, row_start=row_start, n_first=n_first,
                          final_norm=out_rows is not None),
        grid_spec=pltpu.PrefetchScalarGridSpec(
            num_scalar_prefetch=1,
            grid=(n_steps,),
            in_specs=[
                pl.BlockSpec((pl.Element(tm), pl.Element(D_MODEL)),
                             lambda i, sl: (pl.multiple_of(row_start(i), SUBLANES), 0)),
                pl.BlockSpec((pl.Element(tm), pl.Element(LANES)),
                             lambda i, sl: (pl.multiple_of(row_start(i), SUBLANES), 0)),
                pl.BlockSpec(memory_space=pl.ANY),
                pl.BlockSpec((1, D_MODEL), lambda i, sl: (0, 0)),
            ],
            out_specs=out_specs,
            scratch_shapes=[pltpu.VMEM((tm, D_MODEL), F32)] * 4 + [pltpu.SemaphoreType.DMA((2,))],
        ),
        out_shape=out_shape,
        compiler_params=pltpu.CompilerParams(dimension_semantics=("arbitrary",), vmem_limit_bytes=VMEM_LIMIT),
        name="combine",
    )(slots, x, route, y, g)
    return res[0] if out_rows is None else res


def _dispatch_plan(route, valid):
    nt = route.shape[0]
    tm = TM_EXPERT
    n_valid = int(np.sum(valid))
    n_slots = -(-(2 * n_valid + N_EXPERTS * (tm - 1)) // tm) * tm
    n_tiles = n_slots // tm
    valid2 = jnp.asarray(np.concatenate([valid, valid]))
    eidx = jnp.concatenate([route[:, 0], route[:, 1]]).astype(jnp.int32)
    eidx = jnp.where(valid2, eidx, N_EXPERTS)
    onehot = (eidx[:, None] == jnp.arange(N_EXPERTS, dtype=jnp.int32)[None, :]).astype(jnp.int32)
    incl = jnp.cumsum(onehot, axis=0)
    counts = incl[-1]
    rank = jnp.sum((incl - 1) * onehot, axis=1)
    padded = ((counts + tm - 1) // tm) * tm
    ends = jnp.cumsum(padded)
    starts = ends - padded
    pair = jnp.arange(2 * nt, dtype=jnp.int32)
    slot = jnp.where(valid2, jnp.take(starts, jnp.minimum(eidx, N_EXPERTS - 1)) + rank, pair % n_slots)
    slot = slot.astype(jnp.int32)
    token = pair % nt
    dest = jnp.where(valid2, slot, n_slots)
    src_token = (jnp.arange(n_slots, dtype=jnp.int32) % nt).at[dest].set(token, mode="drop")
    n_used = (ends[-1] // tm).astype(jnp.int32)
    tile_start = jnp.arange(n_tiles, dtype=jnp.int32) * tm
    tile_expert = jnp.sum((tile_start[:, None] >= ends[None, :]).astype(jnp.int32), axis=1)
    last_expert = jnp.take(tile_expert, jnp.maximum(n_used - 1, 0))
    tile_expert = jnp.where(jnp.arange(n_tiles) < n_used, tile_expert, last_expert)
    tile_expert = jnp.minimum(tile_expert, N_EXPERTS - 1).astype(jnp.int32)
    ids = jnp.arange(N_EXPERTS, dtype=jnp.int32)
    later = (ids[None, :] > ids[:, None]) & (counts[None, :] > 0)
    next_nonempty = jnp.min(jnp.where(later, ids[None, :], N_EXPERTS), axis=1)
    next_expert = jnp.take(next_nonempty, tile_expert).astype(jnp.int32)
    return slot, src_token, tile_expert, next_expert, n_used.reshape(1)


def _moe(x, valid, g_ffn, wr, br, w_gate, w_up, w_down, layer, g_final, out_rows):
    h, route = _route(x, g_ffn, wr, br)
    slot, src_token, tile_expert, next_expert, n_used = _dispatch_plan(route, valid)
    y = _experts(tile_expert, next_expert, n_used, src_token, h, w_gate, w_up, w_down, layer)
    return _combine(slot, x, route, y, g_final, out_rows)


def _pool_windows(ext_rows, u, inv_cnt, w_ref, scale):
    outs = []
    for gi, w in enumerate(POOL_WINDOWS):
        c0 = gi * POOL_CH
        win = u[:, c0:c0 + POOL_CH]
        for j in range(1, w):
            win = win + ext_rows(j, c0)
        d = win * inv_cnt[gi] - u[:, c0:c0 + POOL_CH]
        outs.append(_dot(d.astype(BF16), w_ref[gi]))
    return jnp.concatenate(outs, axis=1) * scale


def _pool_prompt_body(xc_ref, xp_ref, g_ref, w_ref, sc_ref, y_ref, st_ref, ext, *, seq_len, nblk, n_prompt_blocks):
    step = pl.program_id(0)
    i = step % nblk
    tp = TP_POOL
    pad = POOL_BUF + 1
    last = seq_len - 1

    @pl.when(step < n_prompt_blocks)
    def _():
        x = xc_ref[...]
        u = _rms(x, g_ref[...])
        tail = _rms(xp_ref[pl.ds(tp - pad, pad), :], g_ref[...])
        ext[pl.ds(0, pad), :] = jnp.where(i > 0, tail, 0.0)
        ext[pl.ds(pad, tp), :] = u
        pos = i * tp + lax.broadcasted_iota(jnp.int32, (tp, 1), 0)
        inv_cnt = [1.0 / jnp.minimum(w, pos + 1).astype(F32) for w in POOL_WINDOWS]

        def ext_rows(j, c0):
            return ext[pl.ds(pad - j, tp), c0:c0 + POOL_CH]

        y_ref[...] = x + _pool_windows(ext_rows, u, inv_cnt, w_ref, sc_ref[...])

        @pl.when(i == last // tp)
        def _():
            st_ref[0] = ext[pl.ds(last % tp + 1, pad), :]

    @pl.when(step >= n_prompt_blocks)
    def _():
        y_ref[...] = xc_ref[...]


def _pool_prompt(x, g, w_bf, scale, nb, lp, seq_len):
    tp = TP_POOL
    nt = x.shape[0]
    nblk = lp // tp
    npb = nb * nblk
    pad = POOL_BUF + 1
    assert nt % tp == 0

    def prev_map(s):
        return (jnp.where((s % nblk == 0) | (s >= npb), s, s - 1), 0)

    return pl.pallas_call(
        functools.partial(_pool_prompt_body, seq_len=seq_len, nblk=nblk, n_prompt_blocks=npb),
        grid=(nt // tp,),
        in_specs=[
            pl.BlockSpec((tp, D_MODEL), lambda s: (s, 0)),
            pl.BlockSpec((tp, D_MODEL), prev_map),
            pl.BlockSpec((1, D_MODEL), lambda s: (0, 0)),
            pl.BlockSpec((len(POOL_WINDOWS), POOL_CH, POOL_CH), lambda s: (0, 0, 0)),
            pl.BlockSpec((1, D_MODEL), lambda s: (0, 0)),
        ],
        out_specs=[
            pl.BlockSpec((tp, D_MODEL), lambda s: (s, 0)),
            pl.BlockSpec((1, pad, D_MODEL), lambda s: (jnp.minimum(s // nblk, nb - 1), 0, 0)),
        ],
        out_shape=[jax.ShapeDtypeStruct((nt, D_MODEL), F32), jax.ShapeDtypeStruct((nb, pad, D_MODEL), F32)],
        scratch_shapes=[pltpu.VMEM((tp + pad, D_MODEL), F32)],
        compiler_params=pltpu.CompilerParams(dimension_semantics=("arbitrary",), vmem_limit_bytes=VMEM_LIMIT),
        name="pool_prompt",
    )(x, x, g, w_bf, scale)


def _pool_sample_body(x_ref, st_ref, g_ref, w_ref, sc_ref, y_ref, so_ref, *, dec_seq):
    g = g_ref[...]
    xs = [x_ref[:, t, :] for t in range(dec_seq)]
    us = [_rms(x, g) for x in xs]

    def ext_row(e, c0):
        if e < POOL_BUF:
            return st_ref[:, e, c0:c0 + POOL_CH]
        return us[e - POOL_BUF][:, c0:c0 + POOL_CH]

    for t in range(dec_seq):
        inv_cnt = [1.0 / w for w in POOL_WINDOWS]
        y = _pool_windows(lambda j, c0: ext_row(POOL_BUF + t - j, c0), us[t], inv_cnt, w_ref, sc_ref[...])
        y_ref[:, t, :] = xs[t] + y
    for e in range(POOL_BUF):
        src = e + dec_seq
        so_ref[:, e, :] = st_ref[:, src, :] if src < POOL_BUF else us[src - POOL_BUF]


def _pool_sample(x3, states, layer, g, w_bf, scale):
    ns, dec_seq, _ = x3.shape
    ts = 32
    state_shape = states.shape[1:]
    return pl.pallas_call(
        functools.partial(_pool_sample_body, dec_seq=dec_seq),
        grid=(ns // ts,),
        in_specs=[
            pl.BlockSpec((ts, dec_seq, D_MODEL), lambda i: (i, 0, 0)),
            pl.BlockSpec((None, ts, POOL_BUF, D_MODEL), lambda i: (layer, i, 0, 0)),
            pl.BlockSpec((1, D_MODEL), lambda i: (0, 0)),
            pl.BlockSpec((len(POOL_WINDOWS), POOL_CH, POOL_CH), lambda i: (0, 0, 0)),
            pl.BlockSpec((1, D_MODEL), lambda i: (0, 0)),
        ],
        out_specs=[
            pl.BlockSpec((ts, dec_seq, D_MODEL), lambda i: (i, 0, 0)),
            pl.BlockSpec((ts, POOL_BUF, D_MODEL), lambda i: (i, 0, 0)),
        ],
        out_shape=[jax.ShapeDtypeStruct(x3.shape, F32), jax.ShapeDtypeStruct(state_shape, F32)],
        compiler_params=pltpu.CompilerParams(dimension_semantics=("arbitrary",), vmem_limit_bytes=VMEM_LIMIT),
        name="pool_sample",
    )(x3, states, g, w_bf, scale)


def _rope_tables(pos):
    half = ROT_DIM // 2
    inv = ROPE_THETA ** (-jnp.arange(0, ROT_DIM, 2, dtype=F32) / ROT_DIM)
    ang = pos.astype(F32)[:, None] * inv[None, :]
    cos, sin = jnp.cos(ang), jnp.sin(ang)
    n = pos.shape[0]
    ones = jnp.ones((n, DQK_A - ROT_DIM), F32)
    zeros = jnp.zeros((n, DQK_A - ROT_DIM), F32)
    zh = jnp.zeros((n, half), F32)
    c = jnp.concatenate([cos, cos, ones], axis=1)
    s_lo = jnp.concatenate([zh, sin, zeros], axis=1)
    s_hi = jnp.concatenate([-sin, zh, zeros], axis=1)
    return jnp.stack([jnp.tile(t, (1, LANES // DQK_A)) for t in (c, s_lo, s_hi)])


def kernel(x_prompt, x_sample, cache_k_a, cache_v_a, cache_k_b, cache_v_b, cache_logf_b, state_pool, page_table,
           meta_tokens, g_mix, w_in, b_forget, lambda_qk, g_subln, w_out, w_pool, pool_scale, g_ffn,
           w_route_group, b_route_group, w_route_expert, b_route_expert, w_gate, w_up, w_down, g_final):
    nb, seq, _ = x_prompt.shape
    ns, dec_seq, _ = x_sample.shape
    depth = g_mix.shape[0]
    seq_len = seq + N_META
    lp = -(-seq_len // TQ) * TQ
    npr = nb * lp
    n_samp = ns * dec_seq
    nt = npr + n_samp
    past = page_table.shape[1] * PAGE_SIZE
    n_phys = cache_k_a.shape[1]
    assert nt % TM_PROJ == 0 and lp % TP_POOL == 0 and dec_seq * HB <= LANES and seq_len % SUBLANES == 0
    assert page_table.shape[1] % PAGES_PER_STEP == 0 and PAGE_SIZE * HB == SUBLANES * LANES

    pieces = []
    for b in range(nb):
        pieces += [meta_tokens, x_prompt[b], jnp.zeros((lp - seq_len, D_MODEL), F32)]
    x = jnp.concatenate(pieces + [x_sample.reshape(n_samp, D_MODEL)], axis=0)
    valid = np.concatenate([np.tile(np.arange(lp) < seq_len, nb), np.ones(n_samp, bool)])
    pos = jnp.concatenate([jnp.tile(jnp.arange(lp, dtype=jnp.int32), nb),
                           past + jnp.tile(jnp.arange(dec_seq, dtype=jnp.int32), ns)])
    rope = _rope_tables(pos)

    outs = {k: [] for k in ("ka_p", "va_p", "kb_p", "vb_p", "lf_p", "pool_p",
                            "ka_s", "va_s", "kb_s", "vb_s", "lf_s", "pool_s")}
    for l in range(depth):
        if l % 2 == 0:
            i = l // 2
            lam_init = 0.8 - 0.6 * math.exp(-0.3 * l)
            lq = lambda_qk[i]
            lam = (jnp.exp(jnp.sum(lq[0] * lq[1])) - jnp.exp(jnp.sum(lq[2] * lq[3])) + lam_init).reshape(1)
            w_main = w_in[i, :, :D_QKV].astype(BF16)
            w_f = jnp.pad(w_in[i, :, D_QKV:], ((0, 0), (0, LANES - HB))).astype(BF16)
            b_f = jnp.pad(b_forget[i], (0, LANES - HB)).reshape(1, LANES)
            g_sub = g_subln[i].reshape(1, DH)
            p, lf = _inproj(x, g_mix[l].reshape(1, D_MODEL), w_main, w_f, b_f, rope)

            p_sample = p[npr:].reshape(ns, dec_seq, 6, HA, DH).transpose(2, 0, 1, 3, 4)
            for name, c in (("ka", 1), ("va", 2), ("kb", 4), ("vb", 5)):
                rows = [p[b * lp:b * lp + seq_len, c * HA * DH:(c + 1) * HA * DH] for b in range(nb)]
                outs[name + "_p"].append(jnp.stack(rows).reshape(nb, seq_len, HA, DH))
                outs[name + "_s"].append(p_sample[c])
            outs["lf_p"].append(lf[:npr].reshape(nb, lp, LANES)[:, :seq_len, :HB])
            outs["lf_s"].append(lf[npr:, :HB].reshape(ns, dec_seq, HB))

            fc, fct = _fcum(lf[:npr], nb, lp)
            fk = fct[:, :HB].reshape(nb, HB, 1, lp)
            oa_p = _flash_a(p, lam, g_sub, nb, lp, seq_len, lam_init)
            ob_p = _flash_b(p, fc, fk, nb, lp, seq_len)

            samp = [p_sample[c].reshape(ns, dec_seq * HA, DH) for c in range(6)]
            lfn = jnp.pad(lf[npr:, :HB].reshape(ns, 1, dec_seq * HB), ((0, 0), (0, 0), (0, LANES - dec_seq * HB)))
            oa_s, ob_s = _sattn(
                page_table, lam, samp[0], samp[3], samp[1], samp[2], samp[4], samp[5], lfn, g_sub,
                cache_k_a[i].reshape(n_phys, PAGE_SIZE * HA, DH), cache_v_a[i].reshape(n_phys, PAGE_SIZE * HA, DH),
                cache_k_b[i].reshape(n_phys, PAGE_SIZE * HB, DH), cache_v_b[i].reshape(n_phys, PAGE_SIZE * HB, DH),
                cache_logf_b[i].reshape(n_phys, SUBLANES, LANES), lam_init)
            wo = w_out[i].astype(BF16)
            x = _outproj(x, oa_p, ob_p, oa_s.reshape(n_samp, HA * DH), ob_s.reshape(n_samp, HB * DH),
                         wo[:HA * DH], wo[HA * DH:])
        else:
            j = l // 2
            g = g_mix[l].reshape(1, D_MODEL)
            wp = w_pool[j].astype(BF16)
            sc = pool_scale[j].reshape(1, D_MODEL)
            ys, st_s = _pool_sample(x[npr:].reshape(ns, dec_seq, D_MODEL), state_pool, j, g, wp, sc)
            y, st_p = _pool_prompt(x, g, wp, sc, nb, lp, seq_len)
            outs["pool_p"].append(st_p[:, 1:])
            outs["pool_s"].append(st_s)
            x = lax.dynamic_update_slice(y, ys.reshape(n_samp, D_MODEL), (npr, 0))

        wr = jnp.concatenate([w_route_group[l], w_route_expert[l].reshape(D_MODEL, N_EXPERTS),
                              jnp.zeros((D_MODEL, LANES - N_GROUPS - N_EXPERTS), F32)], axis=1)
        wr_hi = wr.astype(BF16)
        wr = jnp.concatenate([wr_hi, (wr - wr_hi.astype(F32)).astype(BF16)], axis=1)
        br =jnp.concatenate([b_route_group[l], b_route_expert[l].reshape(N_EXPERTS),
                              jnp.zeros((LANES - N_GROUPS - N_EXPERTS,), F32)]).reshape(1, LANES)
        out_rows = (nb, lp, N_META, seq, npr) if l == depth - 1 else None
        x = _moe(x, valid, g_ffn[l].reshape(1, D_MODEL), wr, br, w_gate, w_up, w_down, l,
                 g_final.reshape(1, D_MODEL), out_rows)
    y_prompt = x[0].reshape(nb, seq, D_MODEL)
    y_sample = x[1].reshape(ns, dec_seq, D_MODEL)
    st = lambda k: jnp.stack(outs[k])
    return (y_prompt, y_sample, st("ka_p"), st("va_p"), st("kb_p"), st("vb_p"), st("lf_p"), st("pool_p"),
            st("ka_s"), st("va_s"), st("kb_s"), st("vb_s"), st("lf_s"), st("pool_s"))
```

```python
import functools
import math

import jax
import jax.numpy as jnp
import numpy as np
from jax import lax
from jax.experimental import pallas as pl
from jax.experimental.pallas import tpu as pltpu

F32 = jnp.float32
BF16 = jnp.bfloat16

D_MODEL = 2048
N_META = 16
HA = 8
HB = 8
DH = 128
DQK_A = 64
ROT_DIM = 16
ROPE_THETA = 500000.0
POOL_WINDOWS = (2, 4, 8, 16)
POOL_CH = D_MODEL // len(POOL_WINDOWS)
POOL_BUF = max(POOL_WINDOWS) - 1
N_GROUPS = 4
EXPERTS_PER_GROUP = 8
N_EXPERTS = N_GROUPS * EXPERTS_PER_GROUP
D_EXPERT = D_MODEL // 4
EPS = 1e-6
NEG_INF = -1e30
PAGE_SIZE = 128
D_QKV = 6 * HA * DH

LANES = 128
SUBLANES = 8
VMEM_LIMIT = 56 * 1024 * 1024

TQ = 256
TM_PROJ = 512
TN_PROJ = 1024
TM_ROUTE = 256
TM_EXPERT = 256
TM_COMBINE = 256
TP_POOL = 256
PAGES_PER_STEP = 8
DMA_UNROLL = 8


def _rms(x, g):
    return x * lax.rsqrt(jnp.mean(x * x, axis=-1, keepdims=True) + EPS) * g


def _dot(a, b):
    return jnp.dot(a, b, preferred_element_type=F32)


def _dot_nt(a, b):
    return lax.dot_general(a, b, (((1,), (1,)), ((), ())), preferred_element_type=F32)


def _inproj_body(x_ref, g_ref, w_ref, wf_ref, bf_ref, rope_ref, p_ref, lf_ref, h_scr, *, n_rope_tiles, tn):
    j = pl.program_id(1)

    @pl.when(j == 0)
    def _():
        hb = _rms(x_ref[...], g_ref[...]).astype(BF16)
        h_scr[...] = hb
        fl = _dot(hb, wf_ref[...]) + bf_ref[...]
        lf_ref[...] = jnp.minimum(fl, 0.0) - jnp.log1p(jnp.exp(-jnp.abs(fl)))

    @pl.when(j < n_rope_tiles)
    def _():
        c = rope_ref[0]
        s_lo = rope_ref[1]
        s_hi = rope_ref[2]
        wide = 2 * LANES
        for cw in range(tn // wide):
            acc = _dot(h_scr[...], w_ref[:, cw * wide:(cw + 1) * wide])
            for cc in range(2):
                a = acc[:, cc * LANES:(cc + 1) * LANES]
                lo = cw * wide + cc * LANES
                p_ref[:, lo:lo + LANES] = (
                    a * c + pltpu.roll(a, ROT_DIM // 2, 1) * s_lo + pltpu.roll(a, LANES - ROT_DIM // 2, 1) * s_hi)

    @pl.when(j >= n_rope_tiles)
    def _():
        p_ref[...] = _dot(h_scr[...], w_ref[...])


def _inproj(x, g, w_bf, wf_bf, bf, rope):
    nt = x.shape[0]
    tm, tn = TM_PROJ, TN_PROJ
    n_rope_tiles = (2 * HA * DH) // tn
    return pl.pallas_call(
        functools.partial(_inproj_body, n_rope_tiles=n_rope_tiles, tn=tn),
        grid=(nt // tm, D_QKV // tn),
        in_specs=[
            pl.BlockSpec((tm, D_MODEL), lambda i, j: (i, 0)),
            pl.BlockSpec((1, D_MODEL), lambda i, j: (0, 0)),
            pl.BlockSpec((D_MODEL, tn), lambda i, j: (0, j)),
            pl.BlockSpec((D_MODEL, LANES), lambda i, j: (0, 0)),
            pl.BlockSpec((1, LANES), lambda i, j: (0, 0)),
            pl.BlockSpec((3, tm, LANES), lambda i, j: (0, i, 0)),
        ],
        out_specs=[
            pl.BlockSpec((tm, tn), lambda i, j: (i, j)),
            pl.BlockSpec((tm, LANES), lambda i, j: (i, 0)),
        ],
        out_shape=[jax.ShapeDtypeStruct((nt, D_QKV), F32), jax.ShapeDtypeStruct((nt, LANES), F32)],
        scratch_shapes=[pltpu.VMEM((tm, D_MODEL), BF16)],
        compiler_params=pltpu.CompilerParams(
            dimension_semantics=("arbitrary", "arbitrary"), vmem_limit_bytes=VMEM_LIMIT),
        name="inproj",
    )(x, g, w_bf, wf_bf, bf, rope)


def _fcum_body(lf_ref, fc_ref, fct_ref, *, lp):
    blk = TQ
    row = lax.broadcasted_iota(jnp.int32, (blk, blk), 0)
    col = lax.broadcasted_iota(jnp.int32, (blk, blk), 1)
    tri = jnp.where(row >= col, 1.0, 0.0).astype(BF16)
    carry = jnp.zeros((1, LANES), F32)
    for b in range(lp // blk):
        x = lf_ref[pl.ds(b * blk, blk), :]
        hi = x.astype(BF16)
        r1 = x - hi.astype(F32)
        mid = r1.astype(BF16)
        lo = (r1 - mid.astype(F32)).astype(BF16)
        c = _dot(tri, hi) + _dot(tri, mid) + _dot(tri, lo) + carry
        fc_ref[pl.ds(b * blk, blk), :] = c
        carry = c[blk - 1:blk, :]
    fct_ref[0] = fc_ref[...].T


def _fcum(lf, nb, lp):
    return pl.pallas_call(
        functools.partial(_fcum_body, lp=lp),
        grid=(nb,),
        in_specs=[pl.BlockSpec((lp, LANES), lambda b: (b, 0))],
        out_specs=[pl.BlockSpec((lp, LANES), lambda b: (b, 0)),
                   pl.BlockSpec((1, LANES, lp), lambda b: (b, 0, 0))],
        out_shape=[jax.ShapeDtypeStruct((nb * lp, LANES), F32), jax.ShapeDtypeStruct((nb, LANES, lp), F32)],
        name="fcum",
    )(lf)


def _split_maps(q):
    lane = lax.broadcasted_iota(jnp.int32, q.shape, 1)
    return jnp.concatenate([jnp.where(lane < DQK_A, q, 0.0), jnp.where(lane >= DQK_A, q, 0.0)], axis=0)


def _subln(o, g, lam_init):
    return _rms(o, g) * (1.0 - lam_init)


def _block_rows(i, seq_len):
    return min(TQ, seq_len - i * TQ)


def _causal_rows(qs, kb, vb, i, rows, bias=None):
    n = (i + 1) * TQ
    s = _dot_nt(qs, kb[:n])
    if bias is not None:
        s = s + bias(n)
    row = lax.broadcasted_iota(jnp.int32, (qs.shape[0], TQ), 0) % rows
    col = lax.broadcasted_iota(jnp.int32, (qs.shape[0], TQ), 1)
    diag = jnp.where(col <= row, s[:, n - TQ:], NEG_INF)
    s = diag if i == 0 else jnp.concatenate([s[:, :n - TQ], diag], axis=1)
    p = jnp.exp(s - jnp.max(s, axis=1, keepdims=True))
    return _dot(p.astype(BF16), vb[:n]) / jnp.sum(p, axis=1, keepdims=True)


def _flash_a_body(lam_ref, q_ref, k_ref, v_ref, g_ref, o_ref, *, lam_init, nq, seq_len):
    kb = k_ref[...].astype(BF16)
    vb = v_ref[...].astype(BF16)
    for i in range(nq):
        rows = _block_rows(i, seq_len)
        qs = _split_maps(q_ref[pl.ds(i * TQ, rows), :] * (DQK_A ** -0.5)).astype(BF16)
        o = _causal_rows(qs, kb, vb, i, rows)
        o = o[:rows] - lam_ref[0] * o[rows:]
        o_ref[pl.ds(i * TQ, rows), :] = _subln(o, g_ref[...], lam_init)
    if nq * TQ > seq_len:
        o_ref[pl.ds(seq_len, nq * TQ - seq_len), :] = jnp.zeros((nq * TQ - seq_len, DH), F32)


def _flash_b_body(q_ref, k_ref, v_ref, fq_ref, fk_ref, o_ref, *, nq, seq_len):
    h = pl.program_id(1)
    kb = k_ref[...].astype(BF16)
    vb = v_ref[...].astype(BF16)
    fk = fk_ref[0, 0]
    for i in range(nq):
        rows = _block_rows(i, seq_len)
        lane = lax.broadcasted_iota(jnp.int32, (rows, LANES), 1)
        qs = (q_ref[pl.ds(i * TQ, rows), :] * (DH ** -0.5)).astype(BF16)
        fq_col = jnp.sum(jnp.where(lane == h, fq_ref[pl.ds(i * TQ, rows), :], 0.0), axis=1, keepdims=True)
        o_ref[pl.ds(i * TQ, rows), :] = _causal_rows(qs, kb, vb, i, rows, bias=lambda n: fq_col - fk[:, :n])
    if nq * TQ > seq_len:
        o_ref[pl.ds(seq_len, nq * TQ - seq_len), :] = jnp.zeros((nq * TQ - seq_len, DH), F32)


def _flash_a(p, lam, g_sub, nb, lp, seq_len, lam_init):
    nq = lp // TQ
    qcol, kcol, vcol = 0, HA, 2 * HA
    seq = lambda c: pl.BlockSpec((lp, DH), lambda b, h: (b, c + h))
    return pl.pallas_call(
        functools.partial(_flash_a_body, lam_init=lam_init, nq=nq, seq_len=seq_len),
        grid=(nb, HA),
        in_specs=[pl.BlockSpec(memory_space=pltpu.SMEM), seq(qcol), seq(kcol), seq(vcol),
                  pl.BlockSpec((1, DH), lambda b, h: (0, 0))],
        out_specs=pl.BlockSpec((lp, DH), lambda b, h: (b, h)),
        out_shape=jax.ShapeDtypeStruct((nb * lp, HA * DH), F32),
        compiler_params=pltpu.CompilerParams(
            dimension_semantics=("arbitrary", "arbitrary"), vmem_limit_bytes=VMEM_LIMIT),
        name="flash_a",
    )(lam, p, p, p, g_sub)


def _flash_b(p, fc, fk, nb, lp, seq_len):
    nq = lp // TQ
    qcol, kcol, vcol = 3 * HA, 4 * HA, 5 * HA
    seq = lambda c: pl.BlockSpec((lp, DH), lambda b, h: (b, c + h))
    return pl.pallas_call(
        functools.partial(_flash_b_body, nq=nq, seq_len=seq_len),
        grid=(nb, HB),
        in_specs=[seq(qcol), seq(kcol), seq(vcol),
                  pl.BlockSpec((lp, LANES), lambda b, h: (b, 0)),
                  pl.BlockSpec((1, 1, 1, lp), lambda b, h: (b, h, 0, 0))],
        out_specs=pl.BlockSpec((lp, DH), lambda b, h: (b, h)),
        out_shape=jax.ShapeDtypeStruct((nb * lp, HB * DH), F32),
        compiler_params=pltpu.CompilerParams(
            dimension_semantics=("arbitrary", "arbitrary"), vmem_limit_bytes=VMEM_LIMIT),
        name="flash_b",
    )(p, p, p, fc, fk)


def _online_update(s, pv, m_ref, l_ref, acc_ref):
    m_prev = m_ref[...]
    m_new = jnp.maximum(m_prev, jnp.max(s, axis=1, keepdims=True))
    alpha = jnp.exp(m_prev - m_new)
    p = jnp.exp(s - m_new)
    l_ref[...] = alpha * l_ref[...] + jnp.sum(p, axis=1, keepdims=True)
    acc_ref[...] = alpha * acc_ref[...] + pv(p.astype(BF16))
    m_ref[...] = m_new


def _page_suffix_bias(x, carry):
    lane = lax.broadcasted_iota(jnp.int32, x.shape, 1)
    row = lax.broadcasted_iota(jnp.int32, x.shape, 0)
    y = x
    t = x
    for s in (8, 16, 32, 64):
        y = y + jnp.where(lane + s < LANES, pltpu.roll(y, LANES - s, 1), 0.0)
        t = t + pltpu.roll(t, s, 1)
    z = t
    for s in (1, 2, 4):
        z = z + jnp.where(row + s < SUBLANES, pltpu.roll(z, SUBLANES - s, 0), 0.0)
    g = (y - x) + (z - t) + carry
    return g, carry + z[0:1, :]


def _sattn_body(*refs, n_steps, dec_seq, lam_init):
    pps = PAGES_PER_STEP
    (_, lam_ref, qa_ref, qb_ref, kan_ref, van_ref, kbn_ref, vbn_ref, lfn_ref, g_ref), refs = refs[:10], refs[10:]
    ka_refs, va_refs, kb_refs, vb_refs, lf_refs = (refs[j * pps:(j + 1) * pps] for j in range(5))
    oa_ref, ob_ref, qa_s, qb_s, ma, la, acca, mb, lb, accb, carry = refs[5 * pps:]
    pstep = pl.program_id(1)
    nrow = dec_seq * HB
    ncol = PAGE_SIZE * HB

    @pl.when(pstep == 0)
    def _():
        qa_s[...] = _split_maps(qa_ref[0] * (DQK_A ** -0.5)).astype(BF16)
        qb_s[...] = (qb_ref[0] * (DH ** -0.5)).astype(BF16)
        for m_ref, l_ref, acc_ref in ((ma, la, acca), (mb, lb, accb)):
            m_ref[...] = jnp.full(m_ref.shape, NEG_INF, F32)
            l_ref[...] = jnp.zeros(l_ref.shape, F32)
            acc_ref[...] = jnp.zeros(acc_ref.shape, F32)
        carry[...] = jnp.zeros(carry.shape, F32)

    def head_match(nr):
        r = lax.broadcasted_iota(jnp.int32, (nr, ncol), 0)
        c = lax.broadcasted_iota(jnp.int32, (nr, ncol), 1)
        return (r % HB) == (c % HB)

    def paged(q_s, k_refs, v_refs, biases, m_ref, l_ref, acc_ref):
        hm = head_match(q_s.shape[0])
        cols = []
        for j in range(pps):
            s = _dot_nt(q_s[...], k_refs[j][0].astype(BF16))
            if biases is not None:
                s = s + biases[j]
            cols.append(jnp.where(hm, s, NEG_INF))

        def pv(p):
            out = _dot(p[:, :ncol], v_refs[0][0].astype(BF16))
            for j in range(1, pps):
                out = out + _dot(p[:, j * ncol:(j + 1) * ncol], v_refs[j][0].astype(BF16))
            return out

        _online_update(jnp.concatenate(cols, axis=1), pv, m_ref, l_ref, acc_ref)

    paged(qa_s, ka_refs, va_refs, None, ma, la, acca)

    biases = []
    c = carry[...]
    for j in range(pps):
        g, c = _page_suffix_bias(lf_refs[j][0], c)
        biases.append(jnp.concatenate([g[r:r + 1, :] for r in range(SUBLANES)], axis=1))
    carry[...] = c
    paged(qb_s, kb_refs, vb_refs, biases, mb, lb, accb)

    @pl.when(pstep == n_steps - 1)
    def _():
        def new_mask(shape):
            r = lax.broadcasted_iota(jnp.int32, shape, 0)
            c = lax.broadcasted_iota(jnp.int32, shape, 1)
            return ((r % HB) == (c % HB)) & ((c // HB) <= ((r // HB) % dec_seq))

        s = _dot_nt(qa_s[...], kan_ref[0].astype(BF16))
        s = jnp.where(new_mask(s.shape), s, NEG_INF)
        _online_update(s, lambda p: _dot(p, van_ref[0].astype(BF16)), ma, la, acca)

        x = lfn_ref[0]
        cs = x
        for t in range(1, dec_seq):
            cs = cs + pltpu.roll(x, t * HB, 1)
        s = _dot_nt(qb_s[...], kbn_ref[0].astype(BF16)) - cs[:, :nrow]
        s = jnp.where(new_mask(s.shape), s, NEG_INF)
        _online_update(s, lambda p: _dot(p, vbn_ref[0].astype(BF16)), mb, lb, accb)

        o = acca[...] / la[...]
        o = o[:nrow] - lam_ref[0] * o[nrow:]
        oa_ref[0] = _subln(o, g_ref[...], lam_init)
        ob_ref[0] = accb[...] / lb[...]


def _sattn(page_table, lam, qa, qb, kan, van, kbn, vbn, lfn, g_sub, ck_a, cv_a, ck_b, cv_b, c_lf, lam_init):
    ns, n_pages = page_table.shape
    pps = PAGES_PER_STEP
    n_steps = n_pages // pps
    nrow = qa.shape[1]
    dec_seq = nrow // HB
    ncol = PAGE_SIZE * HB
    pt_flat = page_table.reshape(-1)

    def seq_map(b, p, pt):
        return (b, 0, 0)

    def page_map(j):
        return lambda b, p, pt: (pt[b * n_pages + (n_pages - 1 - (p * pps + j))], 0, 0)

    seq_spec = pl.BlockSpec((1, nrow, DH), seq_map)
    page_specs = [pl.BlockSpec((1, ncol, DH), page_map(j)) for j in range(pps)]
    lf_specs = [pl.BlockSpec((1, SUBLANES, LANES), page_map(j)) for j in range(pps)]
    return pl.pallas_call(
        functools.partial(_sattn_body, n_steps=n_steps, dec_seq=dec_seq, lam_init=lam_init),
        grid_spec=pltpu.PrefetchScalarGridSpec(
            num_scalar_prefetch=1,
            grid=(ns, n_steps),
            in_specs=[
                pl.BlockSpec(memory_space=pltpu.SMEM),
                seq_spec, seq_spec, seq_spec, seq_spec, seq_spec, seq_spec,
                pl.BlockSpec((1, 1, LANES), seq_map),
                pl.BlockSpec((1, DH), lambda b, p, pt: (0, 0)),
            ] + page_specs * 4 + lf_specs,
            out_specs=[seq_spec, seq_spec],
            scratch_shapes=[
                pltpu.VMEM((2 * nrow, DH), BF16), pltpu.VMEM((nrow, DH), BF16),
                pltpu.VMEM((2 * nrow, 1), F32), pltpu.VMEM((2 * nrow, 1), F32), pltpu.VMEM((2 * nrow, DH), F32),
                pltpu.VMEM((nrow, 1), F32), pltpu.VMEM((nrow, 1), F32), pltpu.VMEM((nrow, DH), F32),
                pltpu.VMEM((1, LANES), F32),
            ],
        ),
        out_shape=[jax.ShapeDtypeStruct((ns, nrow, DH), F32), jax.ShapeDtypeStruct((ns, nrow, DH), F32)],
        compiler_params=pltpu.CompilerParams(
            dimension_semantics=("arbitrary", "arbitrary"), vmem_limit_bytes=VMEM_LIMIT),
        name="sattn",
    )(pt_flat, lam, qa, qb, kan, van, kbn, vbn, lfn, g_sub,
      *([ck_a] * pps), *([cv_a] * pps), *([ck_b] * pps), *([cv_b] * pps), *([c_lf] * pps))


def _outproj_body(x_ref, oap_ref, obp_ref, oas_ref, obs_ref, wa_ref, wb_ref, y_ref, *, n_prompt_tiles):
    is_sample = pl.program_id(0) >= n_prompt_tiles
    oa = jnp.where(is_sample, oas_ref[...], oap_ref[...]).astype(BF16)
    ob = jnp.where(is_sample, obs_ref[...], obp_ref[...]).astype(BF16)
    y_ref[...] = x_ref[...] + _dot(oa, wa_ref[...]) + _dot(ob, wb_ref[...])


def _outproj(x, oa_p, ob_p, oa_s, ob_s, wa_bf, wb_bf):
    nt = x.shape[0]
    tm, tn = TM_PROJ // 2, D_MODEL
    ka, kb = oa_p.shape[1], ob_p.shape[1]
    npt = oa_p.shape[0] // tm
    nst = oa_s.shape[0] // tm
    assert npt * tm == oa_p.shape[0] and nst * tm == oa_s.shape[0] and (npt + nst) * tm == nt
    p_map = lambda i, j: (jnp.minimum(i, npt - 1), 0)
    s_map = lambda i, j: (jnp.maximum(i - npt, 0), 0)
    return pl.pallas_call(
        functools.partial(_outproj_body, n_prompt_tiles=npt),
        grid=(nt // tm, D_MODEL // tn),
        in_specs=[
            pl.BlockSpec((tm, tn), lambda i, j: (i, j)),
            pl.BlockSpec((tm, ka), p_map),
            pl.BlockSpec((tm, kb), p_map),
            pl.BlockSpec((tm, ka), s_map),
            pl.BlockSpec((tm, kb), s_map),
            pl.BlockSpec((ka, tn), lambda i, j: (0, j)),
            pl.BlockSpec((kb, tn), lambda i, j: (0, j)),
        ],
        out_specs=pl.BlockSpec((tm, tn), lambda i, j: (i, j)),
        out_shape=jax.ShapeDtypeStruct((nt, D_MODEL), F32),
        compiler_params=pltpu.CompilerParams(
            dimension_semantics=("arbitrary", "arbitrary"), vmem_limit_bytes=VMEM_LIMIT),
        name="outproj",
    )(x, oa_p, ob_p, oa_s, ob_s, wa_bf, wb_bf)


def _route_body(x_ref, g_ref, wr_ref, br_ref, h_ref, r_ref):
    h = _rms(x_ref[...], g_ref[...])
    h_ref[...] = h
    hi = h.astype(BF16)
    lo = (h - hi.astype(F32)).astype(BF16)
    wr = wr_ref[...]
    r_hi = _dot(hi, wr)
    lg = r_hi[:, :LANES] + (r_hi[:, LANES:] + _dot(lo, wr[:, :LANES])) + br_ref[...]
    lane = lax.broadcasted_iota(jnp.int32, lg.shape, 1).astype(F32)
    big = float(LANES)

    def first_max(vals, valid):
        v = jnp.where(valid, vals, NEG_INF)
        top = jnp.max(v, axis=1, keepdims=True)
        idx = jnp.min(jnp.where(valid & (v == top), lane, big), axis=1, keepdims=True)
        return top, idx

    is_grp = lane < N_GROUPS
    gmax, grp = first_max(lg, is_grp)
    p_grp = 1.0 / jnp.sum(jnp.where(is_grp, jnp.exp(lg - gmax), 0.0), axis=1, keepdims=True)
    lo = N_GROUPS + EXPERTS_PER_GROUP * grp
    in_grp = (lane >= lo) & (lane < lo + EXPERTS_PER_GROUP)
    v0, i0 = first_max(lg, in_grp)
    v1, i1 = first_max(lg, in_grp & (lane != i0))
    e = jnp.exp(v1 - v0)
    w0 = p_grp / (1.0 + e)
    w1 = p_grp * e / (1.0 + e)
    r = jnp.where(lane == 0, i0 - N_GROUPS, 0.0)
    r = jnp.where(lane == 1, i1 - N_GROUPS, r)
    r = jnp.where(lane == 2, w0, r)
    r = jnp.where(lane == 3, w1, r)
    r_ref[...] = r


def _route(x, g, wr, br):
    nt = x.shape[0]
    tm = TM_ROUTE
    return pl.pallas_call(
        _route_body,
        grid=(nt // tm,),
        in_specs=[
            pl.BlockSpec((tm, D_MODEL), lambda i: (i, 0)),
            pl.BlockSpec((1, D_MODEL), lambda i: (0, 0)),
            pl.BlockSpec((D_MODEL, 2 * LANES), lambda i: (0, 0)),
            pl.BlockSpec((1, LANES), lambda i: (0, 0)),
        ],
        out_specs=[pl.BlockSpec((tm, D_MODEL), lambda i: (i, 0)), pl.BlockSpec((tm, LANES), lambda i: (i, 0))],
        out_shape=[jax.ShapeDtypeStruct((nt, D_MODEL), F32), jax.ShapeDtypeStruct((nt, LANES), F32)],
        compiler_params=pltpu.CompilerParams(dimension_semantics=("arbitrary",), vmem_limit_bytes=VMEM_LIMIT),
        name="route",
    )(x, g, wr, br)


def _start_row_gather(idx_ref, idx_base, src_ref, dst_ref, sem, n_rows, straight_line=False):
    def copy(r):
        pltpu.make_async_copy(src_ref.at[pl.ds(idx_ref[idx_base + r], 1)],
                              dst_ref.at[pl.ds(r, 1)], sem).start()

    if straight_line:
        for r in range(n_rows):
            copy(r)
        return

    def issue(c, carry):
        for u in range(DMA_UNROLL):
            copy(c * DMA_UNROLL + u)
        return carry

    lax.fori_loop(0, n_rows // DMA_UNROLL, issue, 0)


def _wait_row_gather(src_ref, dst_ref, sem, n_rows):
    pltpu.make_async_copy(src_ref.at[pl.ds(0, n_rows)], dst_ref, sem).wait()


def _expert_body(te_ref, ne_ref, nu_ref, src_ref, h_ref, wg_ref, wu_ref, wd_ref, y_ref, hbuf0, hbuf1, sems,
                 wg_f, wu_f, wd_f, wsem, wg_s, wu_s, wd_s, *, n_tiles, layer):
    tm = TM_EXPERT
    t = pl.program_id(0)
    n_used = nu_ref[0]
    bufs = (hbuf0, hbuf1)
    weights = ((wg_ref, wg_f, wg_s), (wu_ref, wu_f, wu_s), (wd_ref, wd_f, wd_s))

    def weight_copies(e):
        return [pltpu.make_async_copy(w_ref.at[layer, e], w_f, wsem) for w_ref, w_f, _ in weights]

    @pl.when(t == 0)
    def _():
        _start_row_gather(src_ref, 0, h_ref, hbuf0, sems.at[0], tm)
        for c in weight_copies(te_ref[0]):
            c.start()

    @pl.when(t < n_used)
    def _():
        e = te_ref[t]
        prev = te_ref[jnp.maximum(t - 1, 0)]

        @pl.when((t == 0) | (e != prev))
        def _():
            for c in weight_copies(e):
                c.wait()
            for _, w_f, w_s in weights:
                w_s[...] = w_f[...].astype(BF16)
            nxt_e = ne_ref[t]

            @pl.when(nxt_e < N_EXPERTS)
            def _():
                for c in weight_copies(nxt_e):
                    c.start()

    nxt = jnp.minimum(t + 1, n_tiles - 1)
    for par in range(2):
        @pl.when((t < n_used) & (t % 2 == par))
        def _():
            _wait_row_gather(h_ref, bufs[par], sems.at[par], tm)
            hb = bufs[par][...].astype(BF16)
            _start_row_gather(src_ref, nxt * tm, h_ref, bufs[1 - par], sems.at[1 - par], tm, straight_line=True)
            hg = _dot(hb, wg_s[...])
            hu = _dot(hb, wu_s[...])
            act = hg * jax.nn.sigmoid(hg) * hu
            y_ref[...] = _dot(act.astype(BF16), wd_s[...])

        @pl.when((t == n_used) & (t % 2 == par))
        def _():
            _wait_row_gather(h_ref, bufs[par], sems.at[par], tm)

        @pl.when((t == n_tiles - 1) & (t < n_used) & (t % 2 == par))
        def _():
            _wait_row_gather(h_ref, bufs[1 - par], sems.at[1 - par], tm)

    @pl.when(t >= n_used)
    def _():
        y_ref[...] = jnp.zeros(y_ref.shape, F32)


def _experts(tile_expert, next_expert, n_used, src_token, h, w_gate, w_up, w_down, layer):
    tm = TM_EXPERT
    n_slots = src_token.shape[0]
    any_spec = pl.BlockSpec(memory_space=pl.ANY)
    return pl.pallas_call(
        functools.partial(_expert_body, n_tiles=n_slots // tm, layer=layer),
        grid_spec=pltpu.PrefetchScalarGridSpec(
            num_scalar_prefetch=4,
            grid=(n_slots // tm,),
            in_specs=[any_spec, any_spec, any_spec, any_spec],
            out_specs=pl.BlockSpec((tm, D_MODEL), lambda t, te, ne, nu, src: (t, 0)),
            scratch_shapes=[pltpu.VMEM((tm, D_MODEL), F32), pltpu.VMEM((tm, D_MODEL), F32),
                            pltpu.SemaphoreType.DMA((2,)),
                            pltpu.VMEM((D_MODEL, D_EXPERT), F32), pltpu.VMEM((D_MODEL, D_EXPERT), F32),
                            pltpu.VMEM((D_EXPERT, D_MODEL), F32), pltpu.SemaphoreType.DMA(()),
                            pltpu.VMEM((D_MODEL, D_EXPERT), BF16), pltpu.VMEM((D_MODEL, D_EXPERT), BF16),
                            pltpu.VMEM((D_EXPERT, D_MODEL), BF16)],
        ),
        out_shape=jax.ShapeDtypeStruct((n_slots, D_MODEL), F32),
        compiler_params=pltpu.CompilerParams(dimension_semantics=("arbitrary",), vmem_limit_bytes=VMEM_LIMIT),
        name="experts",
    )(tile_expert, next_expert, n_used, src_token, h, w_gate, w_up, w_down)


def _combine_body(slot_ref, x_ref, r_ref, y_ref, g_ref, *rest, nt, n_steps, row_start, n_first, final_norm):
    tm = TM_COMBINE
    t = pl.program_id(0)
    o_refs, (b00, b01, b10, b11, sems) = rest[:-5], rest[-5:]
    bufs = ((b00, b01), (b10, b11))

    def start(step, par, straight_line):
        for k in range(2):
            _start_row_gather(slot_ref, k * nt + row_start(step), y_ref, bufs[par][k], sems.at[par], tm,
                              straight_line)

    def wait(par):
        for k in range(2):
            _wait_row_gather(y_ref, bufs[par][k], sems.at[par], tm)

    @pl.when(t == 0)
    def _():
        start(0, 0, False)

    nxt = jnp.minimum(t + 1, n_steps - 1)
    for par in range(2):
        @pl.when(t % 2 == par)
        def _():
            wait(par)
            start(nxt, 1 - par, True)
            r = r_ref[...]
            out = x_ref[...] + r[:, 2:3] * bufs[par][0][...] + r[:, 3:4] * bufs[par][1][...]
            if final_norm:
                out = _rms(out, g_ref[...])
            if len(o_refs) == 1:
                o_refs[0][...] = out
            else:
                @pl.when(t < n_first)
                def _():
                    o_refs[0][...] = out

                @pl.when(t >= n_first)
                def _():
                    o_refs[1][...] = out

        @pl.when((t == n_steps - 1) & (t % 2 == par))
        def _():
            wait(1 - par)


def _combine(slots, x, route, y, g, out_rows=None):
    nt = x.shape[0]
    tm = TM_COMBINE
    if out_rows is None:
        n_steps = n_first = nt // tm
        row_start = lambda t: t * tm
        out_specs = [pl.BlockSpec((tm, D_MODEL), lambda i, sl: (i, 0))]
        out_shape = [jax.ShapeDtypeStruct((nt, D_MODEL), F32)]
    else:
        nb, lp, first, count, npr = out_rows
        per = count // tm
        n_first = nb * per
        n_second = (nt - npr) // tm
        assert per * tm == count and n_second * tm == nt - npr and first % SUBLANES == 0
        n_steps = n_first + n_second
        row_start = lambda t: jnp.where(t < n_first, (t // per) * lp + first + (t % per) * tm,
                                        npr + (t - n_first) * tm)
        out_specs = [pl.BlockSpec((tm, D_MODEL), lambda i, sl: (jnp.minimum(i, n_first - 1), 0)),
                     pl.BlockSpec((tm, D_MODEL), lambda i, sl: (jnp.maximum(i - n_first, 0), 0))]
        out_shape = [jax.ShapeDtypeStruct((nb * count, D_MODEL), F32),
                     jax.ShapeDtypeStruct((nt - npr, D_MODEL), F32)]
    res = pl.pallas_call(
        functools.partial(_combine_body, nt=nt, n_steps=n_steps, row_start=row_start, n_first=n_first,
                          final_norm=out_rows is not None),
        grid_spec=pltpu.PrefetchScalarGridSpec(
            num_scalar_prefetch=1,
            grid=(n_steps,),
            in_specs=[
                pl.BlockSpec((pl.Element(tm), pl.Element(D_MODEL)),
                             lambda i, sl: (pl.multiple_of(row_start(i), SUBLANES), 0)),
                pl.BlockSpec((pl.Element(tm), pl.Element(LANES)),
                             lambda i, sl: (pl.multiple_of(row_start(i), SUBLANES), 0)),
                pl.BlockSpec(memory_space=pl.ANY),
                pl.BlockSpec((1, D_MODEL), lambda i, sl: (0, 0)),
            ],
            out_specs=out_specs,
            scratch_shapes=[pltpu.VMEM((tm, D_MODEL), F32)] * 4 + [pltpu.SemaphoreType.DMA((2,))],
        ),
        out_shape=out_shape,
        compiler_params=pltpu.CompilerParams(dimension_semantics=("arbitrary",), vmem_limit_bytes=VMEM_LIMIT),
        name="combine",
    )(slots, x, route, y, g)
    return res[0] if out_rows is None else res


def _dispatch_plan(route, valid):
    nt = route.shape[0]
    tm = TM_EXPERT
    n_valid = int(np.sum(valid))
    n_slots = -(-(2 * n_valid + N_EXPERTS * (tm - 1)) // tm) * tm
    n_tiles = n_slots // tm
    valid2 = jnp.asarray(np.concatenate([valid, valid]))
    eidx = jnp.concatenate([route[:, 0], route[:, 1]]).astype(jnp.int32)
    eidx = jnp.where(valid2, eidx, N_EXPERTS)
    onehot = (eidx[:, None] == jnp.arange(N_EXPERTS, dtype=jnp.int32)[None, :]).astype(jnp.int32)
    incl = jnp.cumsum(onehot, axis=0)
    counts = incl[-1]
    rank = jnp.sum((incl - 1) * onehot, axis=1)
    padded = ((counts + tm - 1) // tm) * tm
    ends = jnp.cumsum(padded)
    starts = ends - padded
    pair = jnp.arange(2 * nt, dtype=jnp.int32)
    slot = jnp.where(valid2, jnp.take(starts, jnp.minimum(eidx, N_EXPERTS - 1)) + rank, pair % n_slots)
    slot = slot.astype(jnp.int32)
    token = pair % nt
    dest = jnp.where(valid2, slot, n_slots)
    src_token = (jnp.arange(n_slots, dtype=jnp.int32) % nt).at[dest].set(token, mode="drop")
    n_used = (ends[-1] // tm).astype(jnp.int32)
    tile_start = jnp.arange(n_tiles, dtype=jnp.int32) * tm
    tile_expert = jnp.sum((tile_start[:, None] >= ends[None, :]).astype(jnp.int32), axis=1)
    last_expert = jnp.take(tile_expert, jnp.maximum(n_used - 1, 0))
    tile_expert = jnp.where(jnp.arange(n_tiles) < n_used, tile_expert, last_expert)
    tile_expert = jnp.minimum(tile_expert, N_EXPERTS - 1).astype(jnp.int32)
    ids = jnp.arange(N_EXPERTS, dtype=jnp.int32)
    later = (ids[None, :] > ids[:, None]) & (counts[None, :] > 0)
    next_nonempty = jnp.min(jnp.where(later, ids[None, :], N_EXPERTS), axis=1)
    next_expert = jnp.take(next_nonempty, tile_expert).astype(jnp.int32)
    return slot, src_token, tile_expert, next_expert, n_used.reshape(1)


def _moe(x, valid, g_ffn, wr, br, w_gate, w_up, w_down, layer, g_final, out_rows):
    h, route = _route(x, g_ffn, wr, br)
    slot, src_token, tile_expert, next_expert, n_used = _dispatch_plan(route, valid)
    y = _experts(tile_expert, next_expert, n_used, src_token, h, w_gate, w_up, w_down, layer)
    return _combine(slot, x, route, y, g_final, out_rows)


def _pool_windows(ext_rows, u, inv_cnt, w_ref, scale):
    outs = []
    for gi, w in enumerate(POOL_WINDOWS):
        c0 = gi * POOL_CH
        win = u[:, c0:c0 + POOL_CH]
        for j in range(1, w):
            win = win + ext_rows(j, c0)
        d = win * inv_cnt[gi] - u[:, c0:c0 + POOL_CH]
        outs.append(_dot(d.astype(BF16), w_ref[gi]))
    return jnp.concatenate(outs, axis=1) * scale


def _pool_prompt_body(xc_ref, xp_ref, g_ref, w_ref, sc_ref, y_ref, st_ref, ext, lev_a, lev_b, *, seq_len, nblk,
                      n_prompt_blocks):
    step = pl.program_id(0)
    i = step % nblk
    tp = TP_POOL
    pad = POOL_BUF + 1
    lead = SUBLANES
    first = lead + pad
    n_lev = pad + tp
    last = seq_len - 1
    assert POOL_WINDOWS == (2, 4, 8, 16)

    @pl.when(step < n_prompt_blocks)
    def _():
        x = xc_ref[...]
        u = _rms(x, g_ref[...])
        tail = _rms(xp_ref[...], g_ref[...])
        ext[pl.ds(0, lead), :] = jnp.zeros((lead, D_MODEL), F32)
        ext[pl.ds(lead, pad), :] = jnp.where(i > 0, tail, 0.0)
        ext[pl.ds(first, tp), :] = u
        lev_a[pl.ds(0, lead), :] = jnp.zeros((lead, POOL_CH), F32)
        lev_b[pl.ds(0, lead), :] = jnp.zeros((lead, POOL_CH), F32)
        pos = i * tp + lax.broadcasted_iota(jnp.int32, (tp, 1), 0)
        outs = []
        for gi, w in enumerate(POOL_WINDOWS):
            c0 = gi * POOL_CH
            inv_cnt = 1.0 / jnp.minimum(w, pos + 1).astype(F32)
            src = lambda r0, n: ext[pl.ds(r0, n), c0:c0 + POOL_CH]
            bufs = (lev_a, lev_b)
            m, k = 1, 0
            while 2 * m < w:
                bufs[k][pl.ds(lead, n_lev), :] = src(lead, n_lev) + src(lead - m, n_lev)
                src = (lambda buf: lambda r0, n: buf[pl.ds(r0, n), :])(bufs[k])
                m, k = 2 * m, 1 - k
            win = src(first, tp) + src(first - m, tp)
            d = win * inv_cnt - u[:, c0:c0 + POOL_CH]
            outs.append(_dot(d.astype(BF16), w_ref[gi]))
        y_ref[...] = x + jnp.concatenate(outs, axis=1) * sc_ref[...]

        @pl.when(i == last // tp)
        def _():
            st_ref[0] = ext[pl.ds(lead + last % tp + 1, pad), :]

    @pl.when(step >= n_prompt_blocks)
    def _():
        y_ref[...] = xc_ref[...]


def _pool_prompt(x, g, w_bf, scale, nb, lp, seq_len):
    tp = TP_POOL
    nt = x.shape[0]
    nblk = lp // tp
    npb = nb * nblk
    pad = POOL_BUF + 1
    assert nt % tp == 0

    def prev_map(s):
        return (pl.multiple_of(jnp.maximum(s * tp - pad, 0), SUBLANES), 0)

    return pl.pallas_call(
        functools.partial(_pool_prompt_body, seq_len=seq_len, nblk=nblk, n_prompt_blocks=npb),
        grid=(nt // tp,),
        in_specs=[
            pl.BlockSpec((tp, D_MODEL), lambda s: (s, 0)),
            pl.BlockSpec((pl.Element(pad), pl.Element(D_MODEL)), prev_map),
            pl.BlockSpec((1, D_MODEL), lambda s: (0, 0)),
            pl.BlockSpec((len(POOL_WINDOWS), POOL_CH, POOL_CH), lambda s: (0, 0, 0)),
            pl.BlockSpec((1, D_MODEL), lambda s: (0, 0)),
        ],
        out_specs=[
            pl.BlockSpec((tp, D_MODEL), lambda s: (s, 0)),
            pl.BlockSpec((1, pad, D_MODEL), lambda s: (jnp.minimum(s // nblk, nb - 1), 0, 0)),
        ],
        out_shape=[jax.ShapeDtypeStruct((nt, D_MODEL), F32), jax.ShapeDtypeStruct((nb, pad, D_MODEL), F32)],
        scratch_shapes=[pltpu.VMEM((SUBLANES + pad + tp, D_MODEL), F32),
                        pltpu.VMEM((SUBLANES + pad + tp, POOL_CH), F32),
                        pltpu.VMEM((SUBLANES + pad + tp, POOL_CH), F32)],
        compiler_params=pltpu.CompilerParams(dimension_semantics=("arbitrary",), vmem_limit_bytes=VMEM_LIMIT),
        name="pool_prompt",
    )(x, x, g, w_bf, scale)


def _pool_sample_body(x_ref, st_ref, g_ref, w_ref, sc_ref, y_ref, so_ref, *, dec_seq):
    g = g_ref[...]
    xs = [x_ref[:, t, :] for t in range(dec_seq)]
    us = [_rms(x, g) for x in xs]

    def ext_row(e, c0):
        if e < POOL_BUF:
            return st_ref[:, e, c0:c0 + POOL_CH]
        return us[e - POOL_BUF][:, c0:c0 + POOL_CH]

    for t in range(dec_seq):
        inv_cnt = [1.0 / w for w in POOL_WINDOWS]
        y = _pool_windows(lambda j, c0: ext_row(POOL_BUF + t - j, c0), us[t], inv_cnt, w_ref, sc_ref[...])
        y_ref[:, t, :] = xs[t] + y
    for e in range(POOL_BUF):
        src = e + dec_seq
        so_ref[:, e, :] = st_ref[:, src, :] if src < POOL_BUF else us[src - POOL_BUF]


def _pool_sample(x3, states, layer, g, w_bf, scale):
    ns, dec_seq, _ = x3.shape
    ts = 32
    state_shape = states.shape[1:]
    return pl.pallas_call(
        functools.partial(_pool_sample_body, dec_seq=dec_seq),
        grid=(ns // ts,),
        in_specs=[
            pl.BlockSpec((ts, dec_seq, D_MODEL), lambda i: (i, 0, 0)),
            pl.BlockSpec((None, ts, POOL_BUF, D_MODEL), lambda i: (layer, i, 0, 0)),
            pl.BlockSpec((1, D_MODEL), lambda i: (0, 0)),
            pl.BlockSpec((len(POOL_WINDOWS), POOL_CH, POOL_CH), lambda i: (0, 0, 0)),
            pl.BlockSpec((1, D_MODEL), lambda i: (0, 0)),
        ],
        out_specs=[
            pl.BlockSpec((ts, dec_seq, D_MODEL), lambda i: (i, 0, 0)),
            pl.BlockSpec((ts, POOL_BUF, D_MODEL), lambda i: (i, 0, 0)),
        ],
        out_shape=[jax.ShapeDtypeStruct(x3.shape, F32), jax.ShapeDtypeStruct(state_shape, F32)],
        compiler_params=pltpu.CompilerParams(dimension_semantics=("arbitrary",), vmem_limit_bytes=VMEM_LIMIT),
        name="pool_sample",
    )(x3, states, g, w_bf, scale)


def _rope_tables(pos):
    half = ROT_DIM // 2
    inv = ROPE_THETA ** (-jnp.arange(0, ROT_DIM, 2, dtype=F32) / ROT_DIM)
    ang = pos.astype(F32)[:, None] * inv[None, :]
    cos, sin = jnp.cos(ang), jnp.sin(ang)
    n = pos.shape[0]
    ones = jnp.ones((n, DQK_A - ROT_DIM), F32)
    zeros = jnp.zeros((n, DQK_A - ROT_DIM), F32)
    zh = jnp.zeros((n, half), F32)
    c = jnp.concatenate([cos, cos, ones], axis=1)
    s_lo = jnp.concatenate([zh, sin, zeros], axis=1)
    s_hi = jnp.concatenate([-sin, zh, zeros], axis=1)
    return jnp.stack([jnp.tile(t, (1, LANES // DQK_A)) for t in (c, s_lo, s_hi)])


def kernel(x_prompt, x_sample, cache_k_a, cache_v_a, cache_k_b, cache_v_b, cache_logf_b, state_pool, page_table,
           meta_tokens, g_mix, w_in, b_forget, lambda_qk, g_subln, w_out, w_pool, pool_scale, g_ffn,
           w_route_group, b_route_group, w_route_expert, b_route_expert, w_gate, w_up, w_down, g_final):
    nb, seq, _ = x_prompt.shape
    ns, dec_seq, _ = x_sample.shape
    depth = g_mix.shape[0]
    seq_len = seq + N_META
    lp = -(-seq_len // TQ) * TQ
    npr = nb * lp
    n_samp = ns * dec_seq
    nt = npr + n_samp
    past = page_table.shape[1] * PAGE_SIZE
    n_phys = cache_k_a.shape[1]
    assert nt % TM_PROJ == 0 and lp % TP_POOL == 0 and dec_seq * HB <= LANES and seq_len % SUBLANES == 0
    assert page_table.shape[1] % PAGES_PER_STEP == 0 and PAGE_SIZE * HB == SUBLANES * LANES

    pieces = []
    for b in range(nb):
        pieces += [meta_tokens, x_prompt[b], jnp.zeros((lp - seq_len, D_MODEL), F32)]
    x = jnp.concatenate(pieces + [x_sample.reshape(n_samp, D_MODEL)], axis=0)
    valid = np.concatenate([np.tile(np.arange(lp) < seq_len, nb), np.ones(n_samp, bool)])
    pos = jnp.concatenate([jnp.tile(jnp.arange(lp, dtype=jnp.int32), nb),
                           past + jnp.tile(jnp.arange(dec_seq, dtype=jnp.int32), ns)])
    rope = _rope_tables(pos)

    outs = {k: [] for k in ("ka_p", "va_p", "kb_p", "vb_p", "lf_p", "pool_p",
                            "ka_s", "va_s", "kb_s", "vb_s", "lf_s", "pool_s")}
    for l in range(depth):
        if l % 2 == 0:
            i = l // 2
            lam_init = 0.8 - 0.6 * math.exp(-0.3 * l)
            lq = lambda_qk[i]
            lam = (jnp.exp(jnp.sum(lq[0] * lq[1])) - jnp.exp(jnp.sum(lq[2] * lq[3])) + lam_init).reshape(1)
            w_main = w_in[i, :, :D_QKV].astype(BF16)
            w_f = jnp.pad(w_in[i, :, D_QKV:], ((0, 0), (0, LANES - HB))).astype(BF16)
            b_f = jnp.pad(b_forget[i], (0, LANES - HB)).reshape(1, LANES)
            g_sub = g_subln[i].reshape(1, DH)
            p, lf = _inproj(x, g_mix[l].reshape(1, D_MODEL), w_main, w_f, b_f, rope)

            p_sample = p[npr:].reshape(ns, dec_seq, 6, HA, DH).transpose(2, 0, 1, 3, 4)
            for name, c in (("ka", 1), ("va", 2), ("kb", 4), ("vb", 5)):
                rows = [p[b * lp:b * lp + seq_len, c * HA * DH:(c + 1) * HA * DH] for b in range(nb)]
                outs[name + "_p"].append(jnp.stack(rows).reshape(nb, seq_len, HA, DH))
                outs[name + "_s"].append(p_sample[c])
            outs["lf_p"].append(lf[:npr].reshape(nb, lp, LANES)[:, :seq_len, :HB])
            outs["lf_s"].append(lf[npr:, :HB].reshape(ns, dec_seq, HB))

            fc, fct = _fcum(lf[:npr], nb, lp)
            fk = fct[:, :HB].reshape(nb, HB, 1, lp)
            oa_p = _flash_a(p, lam, g_sub, nb, lp, seq_len, lam_init)
            ob_p = _flash_b(p, fc, fk, nb, lp, seq_len)

            samp = [p_sample[c].reshape(ns, dec_seq * HA, DH) for c in range(6)]
            lfn = jnp.pad(lf[npr:, :HB].reshape(ns, 1, dec_seq * HB), ((0, 0), (0, 0), (0, LANES - dec_seq * HB)))
            oa_s, ob_s = _sattn(
                page_table, lam, samp[0], samp[3], samp[1], samp[2], samp[4], samp[5], lfn, g_sub,
                cache_k_a[i].reshape(n_phys, PAGE_SIZE * HA, DH), cache_v_a[i].reshape(n_phys, PAGE_SIZE * HA, DH),
                cache_k_b[i].reshape(n_phys, PAGE_SIZE * HB, DH), cache_v_b[i].reshape(n_phys, PAGE_SIZE * HB, DH),
                cache_logf_b[i].reshape(n_phys, SUBLANES, LANES), lam_init)
            wo = w_out[i].astype(BF16)
            x = _outproj(x, oa_p, ob_p, oa_s.reshape(n_samp, HA * DH), ob_s.reshape(n_samp, HB * DH),
                         wo[:HA * DH], wo[HA * DH:])
        else:
            j = l // 2
            g = g_mix[l].reshape(1, D_MODEL)
            wp = w_pool[j].astype(BF16)
            sc = pool_scale[j].reshape(1, D_MODEL)
            ys, st_s = _pool_sample(x[npr:].reshape(ns, dec_seq, D_MODEL), state_pool, j, g, wp, sc)
            y, st_p = _pool_prompt(x, g, wp, sc, nb, lp, seq_len)
            outs["pool_p"].append(st_p[:, 1:])
            outs["pool_s"].append(st_s)
            x = lax.dynamic_update_slice(y, ys.reshape(n_samp, D_MODEL), (npr, 0))

        wr = jnp.concatenate([w_route_group[l], w_route_expert[l].reshape(D_MODEL, N_EXPERTS),
                              jnp.zeros((D_MODEL, LANES - N_GROUPS - N_EXPERTS), F32)], axis=1)
        wr_hi = wr.astype(BF16)
        wr = jnp.concatenate([wr_hi, (wr - wr_hi.astype(F32)).astype(BF16)], axis=1)
        br =jnp.concatenate([b_route_group[l], b_route_expert[l].reshape(N_EXPERTS),
                              jnp.zeros((LANES - N_GROUPS - N_EXPERTS,), F32)]).reshape(1, LANES)
        out_rows = (nb, lp, N_META, seq, npr) if l == depth - 1 else None
        x = _moe(x, valid, g_ffn[l].reshape(1, D_MODEL), wr, br, w_gate, w_up, w_down, l,
                 g_final.reshape(1, D_MODEL), out_rows)
    y_prompt = x[0].reshape(nb, seq, D_MODEL)
    y_sample = x[1].reshape(ns, dec_seq, D_MODEL)
    st = lambda k: jnp.stack(outs[k])
    return (y_prompt, y_sample, st("ka_p"), st("va_p"), st("kb_p"), st("vb_p"), st("lf_p"), st("pool_p"),
            st("ka_s"), st("va_s"), st("kb_s"), st("vb_s"), st("lf_s"), st("pool_s"))
```

```python
import functools
import math

import jax
import jax.numpy as jnp
import numpy as np
from jax import lax
from jax.experimental import pallas as pl
from jax.experimental.pallas import tpu as pltpu

F32 = jnp.float32
BF16 = jnp.bfloat16

D_MODEL = 2048
N_META = 16
HA = 8
HB = 8
DH = 128
DQK_A = 64
ROT_DIM = 16
ROPE_THETA = 500000.0
POOL_WINDOWS = (2, 4, 8, 16)
POOL_CH = D_MODEL // len(POOL_WINDOWS)
POOL_BUF = max(POOL_WINDOWS) - 1
N_GROUPS = 4
EXPERTS_PER_GROUP = 8
N_EXPERTS = N_GROUPS * EXPERTS_PER_GROUP
D_EXPERT = D_MODEL // 4
EPS = 1e-6
NEG_INF = -1e30
PAGE_SIZE = 128
D_QKV = 6 * HA * DH

LANES = 128
SUBLANES = 8
VMEM_LIMIT = 56 * 1024 * 1024

TQ = 256
TM_PROJ = 512
TN_PROJ = 2048
TM_ROUTE = 256
TM_EXPERT = 256
TM_COMBINE = 256
TP_POOL = 256
PAGES_PER_STEP = 8
DMA_UNROLL = 8


def _rms(x, g):
    return x * lax.rsqrt(jnp.mean(x * x, axis=-1, keepdims=True) + EPS) * g


def _dot(a, b):
    return jnp.dot(a, b, preferred_element_type=F32)


def _dot_nt(a, b):
    return lax.dot_general(a, b, (((1,), (1,)), ((), ())), preferred_element_type=F32)


def _inproj_body(x_ref, g_ref, w_ref, wf_ref, bf_ref, rope_ref, p_ref, lf_ref, h_scr, *, n_rope_tiles, tn):
    j = pl.program_id(1)

    @pl.when(j == 0)
    def _():
        hb = _rms(x_ref[...], g_ref[...]).astype(BF16)
        h_scr[...] = hb
        fl = _dot(hb, wf_ref[...]) + bf_ref[...]
        lf_ref[...] = jnp.minimum(fl, 0.0) - jnp.log1p(jnp.exp(-jnp.abs(fl)))

    @pl.when(j < n_rope_tiles)
    def _():
        c = rope_ref[0]
        s_lo = rope_ref[1]
        s_hi = rope_ref[2]
        wide = 2 * LANES
        for cw in range(tn // wide):
            acc = _dot(h_scr[...], w_ref[:, cw * wide:(cw + 1) * wide])
            for cc in range(2):
                a = acc[:, cc * LANES:(cc + 1) * LANES]
                lo = cw * wide + cc * LANES
                p_ref[:, lo:lo + LANES] = (
                    a * c + pltpu.roll(a, ROT_DIM // 2, 1) * s_lo + pltpu.roll(a, LANES - ROT_DIM // 2, 1) * s_hi)

    @pl.when(j >= n_rope_tiles)
    def _():
        p_ref[...] = _dot(h_scr[...], w_ref[...])


def _inproj(x, g, w_bf, wf_bf, bf, rope):
    nt = x.shape[0]
    tm, tn = TM_PROJ, TN_PROJ
    n_rope_tiles = (2 * HA * DH) // tn
    return pl.pallas_call(
        functools.partial(_inproj_body, n_rope_tiles=n_rope_tiles, tn=tn),
        grid=(nt // tm, D_QKV // tn),
        in_specs=[
            pl.BlockSpec((tm, D_MODEL), lambda i, j: (i, 0)),
            pl.BlockSpec((1, D_MODEL), lambda i, j: (0, 0)),
            pl.BlockSpec((D_MODEL, tn), lambda i, j: (0, j)),
            pl.BlockSpec((D_MODEL, LANES), lambda i, j: (0, 0)),
            pl.BlockSpec((1, LANES), lambda i, j: (0, 0)),
            pl.BlockSpec((3, tm, LANES), lambda i, j: (0, i, 0)),
        ],
        out_specs=[
            pl.BlockSpec((tm, tn), lambda i, j: (i, j)),
            pl.BlockSpec((tm, LANES), lambda i, j: (i, 0)),
        ],
        out_shape=[jax.ShapeDtypeStruct((nt, D_QKV), F32), jax.ShapeDtypeStruct((nt, LANES), F32)],
        scratch_shapes=[pltpu.VMEM((tm, D_MODEL), BF16)],
        compiler_params=pltpu.CompilerParams(
            dimension_semantics=("arbitrary", "arbitrary"), vmem_limit_bytes=VMEM_LIMIT),
        name="inproj",
    )(x, g, w_bf, wf_bf, bf, rope)


def _fcum_body(lf_ref, fc_ref, fct_ref, *, lp):
    blk = TQ
    row = lax.broadcasted_iota(jnp.int32, (blk, blk), 0)
    col = lax.broadcasted_iota(jnp.int32, (blk, blk), 1)
    tri = jnp.where(row >= col, 1.0, 0.0).astype(BF16)
    carry = jnp.zeros((1, LANES), F32)
    for b in range(lp // blk):
        x = lf_ref[pl.ds(b * blk, blk), :]
        hi = x.astype(BF16)
        r1 = x - hi.astype(F32)
        mid = r1.astype(BF16)
        lo = (r1 - mid.astype(F32)).astype(BF16)
        c = _dot(tri, hi) + _dot(tri, mid) + _dot(tri, lo) + carry
        fc_ref[pl.ds(b * blk, blk), :] = c
        carry = c[blk - 1:blk, :]
    fct_ref[0] = fc_ref[...].T


def _fcum(lf, nb, lp):
    return pl.pallas_call(
        functools.partial(_fcum_body, lp=lp),
        grid=(nb,),
        in_specs=[pl.BlockSpec((lp, LANES), lambda b: (b, 0))],
        out_specs=[pl.BlockSpec((lp, LANES), lambda b: (b, 0)),
                   pl.BlockSpec((1, LANES, lp), lambda b: (b, 0, 0))],
        out_shape=[jax.ShapeDtypeStruct((nb * lp, LANES), F32), jax.ShapeDtypeStruct((nb, LANES, lp), F32)],
        name="fcum",
    )(lf)


def _split_maps(q):
    lane = lax.broadcasted_iota(jnp.int32, q.shape, 1)
    return jnp.concatenate([jnp.where(lane < DQK_A, q, 0.0), jnp.where(lane >= DQK_A, q, 0.0)], axis=0)


def _subln(o, g, lam_init):
    return _rms(o, g) * (1.0 - lam_init)


def _block_rows(i, seq_len):
    return min(TQ, seq_len - i * TQ)


def _causal_rows(qs, kb, vb, i, rows, bias=None):
    n = (i + 1) * TQ
    s = _dot_nt(qs, kb[:n])
    if bias is not None:
        s = s + bias(n)
    row = lax.broadcasted_iota(jnp.int32, (qs.shape[0], TQ), 0) % rows
    col = lax.broadcasted_iota(jnp.int32, (qs.shape[0], TQ), 1)
    diag = jnp.where(col <= row, s[:, n - TQ:], NEG_INF)
    s = diag if i == 0 else jnp.concatenate([s[:, :n - TQ], diag], axis=1)
    p = jnp.exp(s - jnp.max(s, axis=1, keepdims=True))
    return _dot(p.astype(BF16), vb[:n]) / jnp.sum(p, axis=1, keepdims=True)


def _flash_a_body(lam_ref, q_ref, k_ref, v_ref, g_ref, o_ref, *, lam_init, nq, seq_len):
    kb = k_ref[...].astype(BF16)
    vb = v_ref[...].astype(BF16)
    for i in range(nq):
        rows = _block_rows(i, seq_len)
        qs = _split_maps(q_ref[pl.ds(i * TQ, rows), :] * (DQK_A ** -0.5)).astype(BF16)
        o = _causal_rows(qs, kb, vb, i, rows)
        o = o[:rows] - lam_ref[0] * o[rows:]
        o_ref[pl.ds(i * TQ, rows), :] = _subln(o, g_ref[...], lam_init)
    if nq * TQ > seq_len:
        o_ref[pl.ds(seq_len, nq * TQ - seq_len), :] = jnp.zeros((nq * TQ - seq_len, DH), F32)


def _flash_b_body(q_ref, k_ref, v_ref, fq_ref, fk_ref, o_ref, *, nq, seq_len):
    h = pl.program_id(1)
    kb = k_ref[...].astype(BF16)
    vb = v_ref[...].astype(BF16)
    fk = fk_ref[0, 0]
    for i in range(nq):
        rows = _block_rows(i, seq_len)
        lane = lax.broadcasted_iota(jnp.int32, (rows, LANES), 1)
        qs = (q_ref[pl.ds(i * TQ, rows), :] * (DH ** -0.5)).astype(BF16)
        fq_col = jnp.sum(jnp.where(lane == h, fq_ref[pl.ds(i * TQ, rows), :], 0.0), axis=1, keepdims=True)
        o_ref[pl.ds(i * TQ, rows), :] = _causal_rows(qs, kb, vb, i, rows, bias=lambda n: fq_col - fk[:, :n])
    if nq * TQ > seq_len:
        o_ref[pl.ds(seq_len, nq * TQ - seq_len), :] = jnp.zeros((nq * TQ - seq_len, DH), F32)


def _flash_a(p, lam, g_sub, nb, lp, seq_len, lam_init):
    nq = lp // TQ
    qcol, kcol, vcol = 0, HA, 2 * HA
    seq = lambda c: pl.BlockSpec((lp, DH), lambda b, h: (b, c + h))
    return pl.pallas_call(
        functools.partial(_flash_a_body, lam_init=lam_init, nq=nq, seq_len=seq_len),
        grid=(nb, HA),
        in_specs=[pl.BlockSpec(memory_space=pltpu.SMEM), seq(qcol), seq(kcol), seq(vcol),
                  pl.BlockSpec((1, DH), lambda b, h: (0, 0))],
        out_specs=pl.BlockSpec((lp, DH), lambda b, h: (b, h)),
        out_shape=jax.ShapeDtypeStruct((nb * lp, HA * DH), F32),
        compiler_params=pltpu.CompilerParams(
            dimension_semantics=("arbitrary", "arbitrary"), vmem_limit_bytes=VMEM_LIMIT),
        name="flash_a",
    )(lam, p, p, p, g_sub)


def _flash_b(p, fc, fk, nb, lp, seq_len):
    nq = lp // TQ
    qcol, kcol, vcol = 3 * HA, 4 * HA, 5 * HA
    seq = lambda c: pl.BlockSpec((lp, DH), lambda b, h: (b, c + h))
    return pl.pallas_call(
        functools.partial(_flash_b_body, nq=nq, seq_len=seq_len),
        grid=(nb, HB),
        in_specs=[seq(qcol), seq(kcol), seq(vcol),
                  pl.BlockSpec((lp, LANES), lambda b, h: (b, 0)),
                  pl.BlockSpec((1, 1, 1, lp), lambda b, h: (b, h, 0, 0))],
        out_specs=pl.BlockSpec((lp, DH), lambda b, h: (b, h)),
        out_shape=jax.ShapeDtypeStruct((nb * lp, HB * DH), F32),
        compiler_params=pltpu.CompilerParams(
            dimension_semantics=("arbitrary", "arbitrary"), vmem_limit_bytes=VMEM_LIMIT),
        name="flash_b",
    )(p, p, p, fc, fk)


def _online_update(s, pv, m_ref, l_ref, acc_ref):
    m_prev = m_ref[...]
    m_new = jnp.maximum(m_prev, jnp.max(s, axis=1, keepdims=True))
    alpha = jnp.exp(m_prev - m_new)
    p = jnp.exp(s - m_new)
    l_ref[...] = alpha * l_ref[...] + jnp.sum(p, axis=1, keepdims=True)
    acc_ref[...] = alpha * acc_ref[...] + pv(p.astype(BF16))
    m_ref[...] = m_new


def _page_suffix_bias(x, carry):
    lane = lax.broadcasted_iota(jnp.int32, x.shape, 1)
    row = lax.broadcasted_iota(jnp.int32, x.shape, 0)
    y = x
    t = x
    for s in (8, 16, 32, 64):
        y = y + jnp.where(lane + s < LANES, pltpu.roll(y, LANES - s, 1), 0.0)
        t = t + pltpu.roll(t, s, 1)
    z = t
    for s in (1, 2, 4):
        z = z + jnp.where(row + s < SUBLANES, pltpu.roll(z, SUBLANES - s, 0), 0.0)
    g = (y - x) + (z - t) + carry
    return g, carry + z[0:1, :]


def _sattn_body(*refs, n_steps, dec_seq, lam_init):
    pps = PAGES_PER_STEP
    (_, lam_ref, qa_ref, qb_ref, kan_ref, van_ref, kbn_ref, vbn_ref, lfn_ref, g_ref), refs = refs[:10], refs[10:]
    ka_refs, va_refs, kb_refs, vb_refs, lf_refs = (refs[j * pps:(j + 1) * pps] for j in range(5))
    oa_ref, ob_ref, qa_s, qb_s, ma, la, acca, mb, lb, accb, carry = refs[5 * pps:]
    pstep = pl.program_id(1)
    nrow = dec_seq * HB
    ncol = PAGE_SIZE * HB

    @pl.when(pstep == 0)
    def _():
        qa_s[...] = _split_maps(qa_ref[0] * (DQK_A ** -0.5)).astype(BF16)
        qb_s[...] = (qb_ref[0] * (DH ** -0.5)).astype(BF16)
        for m_ref, l_ref, acc_ref in ((ma, la, acca), (mb, lb, accb)):
            m_ref[...] = jnp.full(m_ref.shape, NEG_INF, F32)
            l_ref[...] = jnp.zeros(l_ref.shape, F32)
            acc_ref[...] = jnp.zeros(acc_ref.shape, F32)
        carry[...] = jnp.zeros(carry.shape, F32)

    def head_match(nr):
        r = lax.broadcasted_iota(jnp.int32, (nr, ncol), 0)
        c = lax.broadcasted_iota(jnp.int32, (nr, ncol), 1)
        return (r % HB) == (c % HB)

    def paged(q_s, k_refs, v_refs, biases, m_ref, l_ref, acc_ref):
        hm = head_match(q_s.shape[0])
        cols = []
        for j in range(pps):
            s = _dot_nt(q_s[...], k_refs[j][0].astype(BF16))
            if biases is not None:
                s = s + biases[j]
            cols.append(jnp.where(hm, s, NEG_INF))

        def pv(p):
            out = _dot(p[:, :ncol], v_refs[0][0].astype(BF16))
            for j in range(1, pps):
                out = out + _dot(p[:, j * ncol:(j + 1) * ncol], v_refs[j][0].astype(BF16))
            return out

        _online_update(jnp.concatenate(cols, axis=1), pv, m_ref, l_ref, acc_ref)

    paged(qa_s, ka_refs, va_refs, None, ma, la, acca)

    biases = []
    c = carry[...]
    for j in range(pps):
        g, c = _page_suffix_bias(lf_refs[j][0], c)
        biases.append(jnp.concatenate([g[r:r + 1, :] for r in range(SUBLANES)], axis=1))
    carry[...] = c
    paged(qb_s, kb_refs, vb_refs, biases, mb, lb, accb)

    @pl.when(pstep == n_steps - 1)
    def _():
        def new_mask(shape):
            r = lax.broadcasted_iota(jnp.int32, shape, 0)
            c = lax.broadcasted_iota(jnp.int32, shape, 1)
            return ((r % HB) == (c % HB)) & ((c // HB) <= ((r // HB) % dec_seq))

        s = _dot_nt(qa_s[...], kan_ref[0].astype(BF16))
        s = jnp.where(new_mask(s.shape), s, NEG_INF)
        _online_update(s, lambda p: _dot(p, van_ref[0].astype(BF16)), ma, la, acca)

        x = lfn_ref[0]
        cs = x
        for t in range(1, dec_seq):
            cs = cs + pltpu.roll(x, t * HB, 1)
        s = _dot_nt(qb_s[...], kbn_ref[0].astype(BF16)) - cs[:, :nrow]
        s = jnp.where(new_mask(s.shape), s, NEG_INF)
        _online_update(s, lambda p: _dot(p, vbn_ref[0].astype(BF16)), mb, lb, accb)

        o = acca[...] / la[...]
        o = o[:nrow] - lam_ref[0] * o[nrow:]
        oa_ref[0] = _subln(o, g_ref[...], lam_init)
        ob_ref[0] = accb[...] / lb[...]


def _sattn(page_table, lam, qa, qb, kan, van, kbn, vbn, lfn, g_sub, ck_a, cv_a, ck_b, cv_b, c_lf, lam_init):
    ns, n_pages = page_table.shape
    pps = PAGES_PER_STEP
    n_steps = n_pages // pps
    nrow = qa.shape[1]
    dec_seq = nrow // HB
    ncol = PAGE_SIZE * HB
    pt_flat = page_table.reshape(-1)

    def seq_map(b, p, pt):
        return (b, 0, 0)

    def page_map(j):
        return lambda b, p, pt: (pt[b * n_pages + (n_pages - 1 - (p * pps + j))], 0, 0)

    seq_spec = pl.BlockSpec((1, nrow, DH), seq_map)
    page_specs = [pl.BlockSpec((1, ncol, DH), page_map(j)) for j in range(pps)]
    lf_specs = [pl.BlockSpec((1, SUBLANES, LANES), page_map(j)) for j in range(pps)]
    return pl.pallas_call(
        functools.partial(_sattn_body, n_steps=n_steps, dec_seq=dec_seq, lam_init=lam_init),
        grid_spec=pltpu.PrefetchScalarGridSpec(
            num_scalar_prefetch=1,
            grid=(ns, n_steps),
            in_specs=[
                pl.BlockSpec(memory_space=pltpu.SMEM),
                seq_spec, seq_spec, seq_spec, seq_spec, seq_spec, seq_spec,
                pl.BlockSpec((1, 1, LANES), seq_map),
                pl.BlockSpec((1, DH), lambda b, p, pt: (0, 0)),
            ] + page_specs * 4 + lf_specs,
            out_specs=[seq_spec, seq_spec],
            scratch_shapes=[
                pltpu.VMEM((2 * nrow, DH), BF16), pltpu.VMEM((nrow, DH), BF16),
                pltpu.VMEM((2 * nrow, 1), F32), pltpu.VMEM((2 * nrow, 1), F32), pltpu.VMEM((2 * nrow, DH), F32),
                pltpu.VMEM((nrow, 1), F32), pltpu.VMEM((nrow, 1), F32), pltpu.VMEM((nrow, DH), F32),
                pltpu.VMEM((1, LANES), F32),
            ],
        ),
        out_shape=[jax.ShapeDtypeStruct((ns, nrow, DH), F32), jax.ShapeDtypeStruct((ns, nrow, DH), F32)],
        compiler_params=pltpu.CompilerParams(
            dimension_semantics=("arbitrary", "arbitrary"), vmem_limit_bytes=VMEM_LIMIT),
        name="sattn",
    )(pt_flat, lam, qa, qb, kan, van, kbn, vbn, lfn, g_sub,
      *([ck_a] * pps), *([cv_a] * pps), *([ck_b] * pps), *([cv_b] * pps), *([c_lf] * pps))


def _outproj_body(x_ref, oap_ref, obp_ref, oas_ref, obs_ref, wa_ref, wb_ref, y_ref, *, n_prompt_tiles):
    is_sample = pl.program_id(0) >= n_prompt_tiles
    oa = jnp.where(is_sample, oas_ref[...], oap_ref[...]).astype(BF16)
    ob = jnp.where(is_sample, obs_ref[...], obp_ref[...]).astype(BF16)
    y_ref[...] = x_ref[...] + _dot(oa, wa_ref[...]) + _dot(ob, wb_ref[...])


def _outproj(x, oa_p, ob_p, oa_s, ob_s, wa_bf, wb_bf):
    nt = x.shape[0]
    tm, tn = TM_PROJ // 2, D_MODEL
    ka, kb = oa_p.shape[1], ob_p.shape[1]
    npt = oa_p.shape[0] // tm
    nst = oa_s.shape[0] // tm
    assert npt * tm == oa_p.shape[0] and nst * tm == oa_s.shape[0] and (npt + nst) * tm == nt
    p_map = lambda i, j: (jnp.minimum(i, npt - 1), 0)
    s_map = lambda i, j: (jnp.maximum(i - npt, 0), 0)
    return pl.pallas_call(
        functools.partial(_outproj_body, n_prompt_tiles=npt),
        grid=(nt // tm, D_MODEL // tn),
        in_specs=[
            pl.BlockSpec((tm, tn), lambda i, j: (i, j)),
            pl.BlockSpec((tm, ka), p_map),
            pl.BlockSpec((tm, kb), p_map),
            pl.BlockSpec((tm, ka), s_map),
            pl.BlockSpec((tm, kb), s_map),
            pl.BlockSpec((ka, tn), lambda i, j: (0, j)),
            pl.BlockSpec((kb, tn), lambda i, j: (0, j)),
        ],
        out_specs=pl.BlockSpec((tm, tn), lambda i, j: (i, j)),
        out_shape=jax.ShapeDtypeStruct((nt, D_MODEL), F32),
        compiler_params=pltpu.CompilerParams(
            dimension_semantics=("arbitrary", "arbitrary"), vmem_limit_bytes=VMEM_LIMIT),
        name="outproj",
    )(x, oa_p, ob_p, oa_s, ob_s, wa_bf, wb_bf)


def _route_body(x_ref, g_ref, wr_ref, br_ref, h_ref, r_ref):
    h = _rms(x_ref[...], g_ref[...])
    h_ref[...] = h
    hi = h.astype(BF16)
    lo = (h - hi.astype(F32)).astype(BF16)
    wr = wr_ref[...]
    r_hi = _dot(hi, wr)
    lg = r_hi[:, :LANES] + (r_hi[:, LANES:] + _dot(lo, wr[:, :LANES])) + br_ref[...]
    lane = lax.broadcasted_iota(jnp.int32, lg.shape, 1).astype(F32)
    big = float(LANES)

    def first_max(vals, valid):
        v = jnp.where(valid, vals, NEG_INF)
        top = jnp.max(v, axis=1, keepdims=True)
        idx = jnp.min(jnp.where(valid & (v == top), lane, big), axis=1, keepdims=True)
        return top, idx

    is_grp = lane < N_GROUPS
    gmax, grp = first_max(lg, is_grp)
    p_grp = 1.0 / jnp.sum(jnp.where(is_grp, jnp.exp(lg - gmax), 0.0), axis=1, keepdims=True)
    lo = N_GROUPS + EXPERTS_PER_GROUP * grp
    in_grp = (lane >= lo) & (lane < lo + EXPERTS_PER_GROUP)
    v0, i0 = first_max(lg, in_grp)
    v1, i1 = first_max(lg, in_grp & (lane != i0))
    e = jnp.exp(v1 - v0)
    w0 = p_grp / (1.0 + e)
    w1 = p_grp * e / (1.0 + e)
    r = jnp.where(lane == 0, i0 - N_GROUPS, 0.0)
    r = jnp.where(lane == 1, i1 - N_GROUPS, r)
    r = jnp.where(lane == 2, w0, r)
    r = jnp.where(lane == 3, w1, r)
    r_ref[...] = r


def _route(x, g, wr, br):
    nt = x.shape[0]
    tm = TM_ROUTE
    return pl.pallas_call(
        _route_body,
        grid=(nt // tm,),
        in_specs=[
            pl.BlockSpec((tm, D_MODEL), lambda i: (i, 0)),
            pl.BlockSpec((1, D_MODEL), lambda i: (0, 0)),
            pl.BlockSpec((D_MODEL, 2 * LANES), lambda i: (0, 0)),
            pl.BlockSpec((1, LANES), lambda i: (0, 0)),
        ],
        out_specs=[pl.BlockSpec((tm, D_MODEL), lambda i: (i, 0)), pl.BlockSpec((tm, LANES), lambda i: (i, 0))],
        out_shape=[jax.ShapeDtypeStruct((nt, D_MODEL), F32), jax.ShapeDtypeStruct((nt, LANES), F32)],
        compiler_params=pltpu.CompilerParams(dimension_semantics=("arbitrary",), vmem_limit_bytes=VMEM_LIMIT),
        name="route",
    )(x, g, wr, br)


def _start_row_gather(idx_ref, idx_base, src_ref, dst_ref, sem, n_rows, straight_line=False):
    def copy(r):
        pltpu.make_async_copy(src_ref.at[pl.ds(idx_ref[idx_base + r], 1)],
                              dst_ref.at[pl.ds(r, 1)], sem).start()

    if straight_line:
        for r in range(n_rows):
            copy(r)
        return

    def issue(c, carry):
        for u in range(DMA_UNROLL):
            copy(c * DMA_UNROLL + u)
        return carry

    lax.fori_loop(0, n_rows // DMA_UNROLL, issue, 0)


def _wait_row_gather(src_ref, dst_ref, sem, n_rows):
    pltpu.make_async_copy(src_ref.at[pl.ds(0, n_rows)], dst_ref, sem).wait()


def _expert_body(te_ref, ne_ref, nu_ref, src_ref, h_ref, wg_ref, wu_ref, wd_ref, y_ref, hbuf0, hbuf1, sems,
                 wg_f, wu_f, wd_f, wsem, wg_s, wu_s, wd_s, *, n_tiles, layer):
    tm = TM_EXPERT
    t = pl.program_id(0)
    n_used = nu_ref[0]
    bufs = (hbuf0, hbuf1)
    weights = ((wg_ref, wg_f, wg_s), (wu_ref, wu_f, wu_s), (wd_ref, wd_f, wd_s))

    def weight_copies(e):
        return [pltpu.make_async_copy(w_ref.at[layer, e], w_f, wsem) for w_ref, w_f, _ in weights]

    @pl.when(t == 0)
    def _():
        _start_row_gather(src_ref, 0, h_ref, hbuf0, sems.at[0], tm)
        for c in weight_copies(te_ref[0]):
            c.start()

    @pl.when(t < n_used)
    def _():
        e = te_ref[t]
        prev = te_ref[jnp.maximum(t - 1, 0)]

        @pl.when((t == 0) | (e != prev))
        def _():
            for c in weight_copies(e):
                c.wait()
            for _, w_f, w_s in weights:
                w_s[...] = w_f[...].astype(BF16)
            nxt_e = ne_ref[t]

            @pl.when(nxt_e < N_EXPERTS)
            def _():
                for c in weight_copies(nxt_e):
                    c.start()

    nxt = jnp.minimum(t + 1, n_tiles - 1)
    for par in range(2):
        @pl.when((t < n_used) & (t % 2 == par))
        def _():
            _wait_row_gather(h_ref, bufs[par], sems.at[par], tm)
            hb = bufs[par][...].astype(BF16)
            _start_row_gather(src_ref, nxt * tm, h_ref, bufs[1 - par], sems.at[1 - par], tm, straight_line=True)
            hg = _dot(hb, wg_s[...])
            hu = _dot(hb, wu_s[...])
            act = hg * jax.nn.sigmoid(hg) * hu
            y_ref[...] = _dot(act.astype(BF16), wd_s[...])

        @pl.when((t == n_used) & (t % 2 == par))
        def _():
            _wait_row_gather(h_ref, bufs[par], sems.at[par], tm)

        @pl.when((t == n_tiles - 1) & (t < n_used) & (t % 2 == par))
        def _():
            _wait_row_gather(h_ref, bufs[1 - par], sems.at[1 - par], tm)

    @pl.when(t >= n_used)
    def _():
        y_ref[...] = jnp.zeros(y_ref.shape, F32)


def _experts(tile_expert, next_expert, n_used, src_token, h, w_gate, w_up, w_down, layer):
    tm = TM_EXPERT
    n_slots = src_token.shape[0]
    any_spec = pl.BlockSpec(memory_space=pl.ANY)
    return pl.pallas_call(
        functools.partial(_expert_body, n_tiles=n_slots // tm, layer=layer),
        grid_spec=pltpu.PrefetchScalarGridSpec(
            num_scalar_prefetch=4,
            grid=(n_slots // tm,),
            in_specs=[any_spec, any_spec, any_spec, any_spec],
            out_specs=pl.BlockSpec((tm, D_MODEL), lambda t, te, ne, nu, src: (t, 0)),
            scratch_shapes=[pltpu.VMEM((tm, D_MODEL), F32), pltpu.VMEM((tm, D_MODEL), F32),
                            pltpu.SemaphoreType.DMA((2,)),
                            pltpu.VMEM((D_MODEL, D_EXPERT), F32), pltpu.VMEM((D_MODEL, D_EXPERT), F32),
                            pltpu.VMEM((D_EXPERT, D_MODEL), F32), pltpu.SemaphoreType.DMA(()),
                            pltpu.VMEM((D_MODEL, D_EXPERT), BF16), pltpu.VMEM((D_MODEL, D_EXPERT), BF16),
                            pltpu.VMEM((D_EXPERT, D_MODEL), BF16)],
        ),
        out_shape=jax.ShapeDtypeStruct((n_slots, D_MODEL), F32),
        compiler_params=pltpu.CompilerParams(dimension_semantics=("arbitrary",), vmem_limit_bytes=VMEM_LIMIT),
        name="experts",
    )(tile_expert, next_expert, n_used, src_token, h, w_gate, w_up, w_down)


def _combine_body(slot_ref, x_ref, r_ref, y_ref, g_ref, *rest, nt, n_steps, row_start, n_first, final_norm):
    tm = TM_COMBINE
    t = pl.program_id(0)
    o_refs, (b00, b01, b10, b11, sems) = rest[:-5], rest[-5:]
    bufs = ((b00, b01), (b10, b11))

    def start(step, par, straight_line):
        for k in range(2):
            _start_row_gather(slot_ref, k * nt + row_start(step), y_ref, bufs[par][k], sems.at[par], tm,
                              straight_line)

    def wait(par):
        for k in range(2):
            _wait_row_gather(y_ref, bufs[par][k], sems.at[par], tm)

    @pl.when(t == 0)
    def _():
        start(0, 0, False)

    nxt = jnp.minimum(t + 1, n_steps - 1)
    for par in range(2):
        @pl.when(t % 2 == par)
        def _():
            wait(par)
            start(nxt, 1 - par, True)
            r = r_ref[...]
            out = x_ref[...] + r[:, 2:3] * bufs[par][0][...] + r[:, 3:4] * bufs[par][1][...]
            if final_norm:
                out = _rms(out, g_ref[...])
            if len(o_refs) == 1:
                o_refs[0][...] = out
            else:
                @pl.when(t < n_first)
                def _():
                    o_refs[0][...] = out

                @pl.when(t >= n_first)
                def _():
                    o_refs[1][...] = out

        @pl.when((t == n_steps - 1) & (t % 2 == par))
        def _():
            wait(1 - par)


def _combine(slots, x, route, y, g, out_rows=None):
    nt = x.shape[0]
    tm = TM_COMBINE
    if out_rows is None:
        n_steps = n_first = nt // tm
        row_start = lambda t: t * tm
        out_specs = [pl.BlockSpec((tm, D_MODEL), lambda i, sl: (i, 0))]
        out_shape = [jax.ShapeDtypeStruct((nt, D_MODEL), F32)]
    else:
        nb, lp, first, count, npr = out_rows
        per = count // tm
        n_first = nb * per
        n_second = (nt - npr) // tm
        assert per * tm == count and n_second * tm == nt - npr and first % SUBLANES == 0
        n_steps = n_first + n_second
        row_start = lambda t: jnp.where(t < n_first, (t // per) * lp + first + (t % per) * tm,
                                        npr + (t - n_first) * tm)
        out_specs = [pl.BlockSpec((tm, D_MODEL), lambda i, sl: (jnp.minimum(i, n_first - 1), 0)),
                     pl.BlockSpec((tm, D_MODEL), lambda i, sl: (jnp.maximum(i - n_first, 0), 0))]
        out_shape = [jax.ShapeDtypeStruct((nb * count, D_MODEL), F32),
                     jax.ShapeDtypeStruct((nt - npr, D_MODEL), F32)]
    res = pl.pallas_call(
        functools.partial(_combine_body, nt=nt, n_steps=n_steps, row_start=row_start, n_first=n_first,
                          final_norm=out_rows is not None),
        grid_spec=pltpu.PrefetchScalarGridSpec(
            num_scalar_prefetch=1,
            grid=(n_steps,),
            in_specs=[
                pl.BlockSpec((pl.Element(tm), pl.Element(D_MODEL)),
                             lambda i, sl: (pl.multiple_of(row_start(i), SUBLANES), 0)),
                pl.BlockSpec((pl.Element(tm), pl.Element(LANES)),
                             lambda i, sl: (pl.multiple_of(row_start(i), SUBLANES), 0)),
                pl.BlockSpec(memory_space=pl.ANY),
                pl.BlockSpec((1, D_MODEL), lambda i, sl: (0, 0)),
            ],
            out_specs=out_specs,
            scratch_shapes=[pltpu.VMEM((tm, D_MODEL), F32)] * 4 + [pltpu.SemaphoreType.DMA((2,))],
        ),
        out_shape=out_shape,
        compiler_params=pltpu.CompilerParams(dimension_semantics=("arbitrary",), vmem_limit_bytes=VMEM_LIMIT),
        name="combine",
    )(slots, x, route, y, g)
    return res[0] if out_rows is None else res


def _dispatch_plan(route, valid):
    nt = route.shape[0]
    tm = TM_EXPERT
    n_valid = int(np.sum(valid))
    n_slots = -(-(2 * n_valid + N_EXPERTS * (tm - 1)) // tm) * tm
    n_tiles = n_slots // tm
    valid2 = jnp.asarray(np.concatenate([valid, valid]))
    eidx = jnp.concatenate([route[:, 0], route[:, 1]]).astype(jnp.int32)
    eidx = jnp.where(valid2, eidx, N_EXPERTS)
    onehot = (eidx[:, None] == jnp.arange(N_EXPERTS, dtype=jnp.int32)[None, :]).astype(jnp.int32)
    incl = jnp.cumsum(onehot, axis=0)
    counts = incl[-1]
    rank = jnp.sum((incl - 1) * onehot, axis=1)
    padded = ((counts + tm - 1) // tm) * tm
    ends = jnp.cumsum(padded)
    starts = ends - padded
    pair = jnp.arange(2 * nt, dtype=jnp.int32)
    slot = jnp.where(valid2, jnp.take(starts, jnp.minimum(eidx, N_EXPERTS - 1)) + rank, pair % n_slots)
    slot = slot.astype(jnp.int32)
    token = pair % nt
    dest = jnp.where(valid2, slot, n_slots)
    src_token = (jnp.arange(n_slots, dtype=jnp.int32) % nt).at[dest].set(token, mode="drop")
    n_used = (ends[-1] // tm).astype(jnp.int32)
    tile_start = jnp.arange(n_tiles, dtype=jnp.int32) * tm
    tile_expert = jnp.sum((tile_start[:, None] >= ends[None, :]).astype(jnp.int32), axis=1)
    last_expert = jnp.take(tile_expert, jnp.maximum(n_used - 1, 0))
    tile_expert = jnp.where(jnp.arange(n_tiles) < n_used, tile_expert, last_expert)
    tile_expert = jnp.minimum(tile_expert, N_EXPERTS - 1).astype(jnp.int32)
    ids = jnp.arange(N_EXPERTS, dtype=jnp.int32)
    later = (ids[None, :] > ids[:, None]) & (counts[None, :] > 0)
    next_nonempty = jnp.min(jnp.where(later, ids[None, :], N_EXPERTS), axis=1)
    next_expert = jnp.take(next_nonempty, tile_expert).astype(jnp.int32)
    return slot, src_token, tile_expert, next_expert, n_used.reshape(1)


def _moe(x, valid, g_ffn, wr, br, w_gate, w_up, w_down, layer, g_final, out_rows):
    h, route = _route(x, g_ffn, wr, br)
    slot, src_token, tile_expert, next_expert, n_used = _dispatch_plan(route, valid)
    y = _experts(tile_expert, next_expert, n_used, src_token, h, w_gate, w_up, w_down, layer)
    return _combine(slot, x, route, y, g_final, out_rows)


def _pool_windows(ext_rows, u, inv_cnt, w_ref, scale):
    outs = []
    for gi, w in enumerate(POOL_WINDOWS):
        c0 = gi * POOL_CH
        win = u[:, c0:c0 + POOL_CH]
        for j in range(1, w):
            win = win + ext_rows(j, c0)
        d = win * inv_cnt[gi] - u[:, c0:c0 + POOL_CH]
        outs.append(_dot(d.astype(BF16), w_ref[gi]))
    return jnp.concatenate(outs, axis=1) * scale


def _pool_prompt_body(xc_ref, xp_ref, g_ref, w_ref, sc_ref, y_ref, st_ref, ext, lev_a, lev_b, *, seq_len, nblk,
                      n_prompt_blocks):
    step = pl.program_id(0)
    i = step % nblk
    tp = TP_POOL
    pad = POOL_BUF + 1
    lead = SUBLANES
    first = lead + pad
    n_lev = pad + tp
    last = seq_len - 1
    assert POOL_WINDOWS == (2, 4, 8, 16)

    @pl.when(step < n_prompt_blocks)
    def _():
        x = xc_ref[...]
        u = _rms(x, g_ref[...])
        tail = _rms(xp_ref[...], g_ref[...])
        ext[pl.ds(0, lead), :] = jnp.zeros((lead, D_MODEL), F32)
        ext[pl.ds(lead, pad), :] = jnp.where(i > 0, tail, 0.0)
        ext[pl.ds(first, tp), :] = u
        lev_a[pl.ds(0, lead), :] = jnp.zeros((lead, POOL_CH), F32)
        lev_b[pl.ds(0, lead), :] = jnp.zeros((lead, POOL_CH), F32)
        pos = i * tp + lax.broadcasted_iota(jnp.int32, (tp, 1), 0)
        outs = []
        for gi, w in enumerate(POOL_WINDOWS):
            c0 = gi * POOL_CH
            inv_cnt = 1.0 / jnp.minimum(w, pos + 1).astype(F32)
            src = lambda r0, n: ext[pl.ds(r0, n), c0:c0 + POOL_CH]
            bufs = (lev_a, lev_b)
            m, k = 1, 0
            while 2 * m < w:
                bufs[k][pl.ds(lead, n_lev), :] = src(lead, n_lev) + src(lead - m, n_lev)
                src = (lambda buf: lambda r0, n: buf[pl.ds(r0, n), :])(bufs[k])
                m, k = 2 * m, 1 - k
            win = src(first, tp) + src(first - m, tp)
            d = win * inv_cnt - u[:, c0:c0 + POOL_CH]
            outs.append(_dot(d.astype(BF16), w_ref[gi]))
        y_ref[...] = x + jnp.concatenate(outs, axis=1) * sc_ref[...]

        @pl.when(i == last // tp)
        def _():
            st_ref[0] = ext[pl.ds(lead + last % tp + 1, pad), :]

    @pl.when(step >= n_prompt_blocks)
    def _():
        y_ref[...] = xc_ref[...]


def _pool_prompt(x, g, w_bf, scale, nb, lp, seq_len):
    tp = TP_POOL
    nt = x.shape[0]
    nblk = lp // tp
    npb = nb * nblk
    pad = POOL_BUF + 1
    assert nt % tp == 0

    def prev_map(s):
        return (pl.multiple_of(jnp.maximum(s * tp - pad, 0), SUBLANES), 0)

    return pl.pallas_call(
        functools.partial(_pool_prompt_body, seq_len=seq_len, nblk=nblk, n_prompt_blocks=npb),
        grid=(nt // tp,),
        in_specs=[
            pl.BlockSpec((tp, D_MODEL), lambda s: (s, 0)),
            pl.BlockSpec((pl.Element(pad), pl.Element(D_MODEL)), prev_map),
            pl.BlockSpec((1, D_MODEL), lambda s: (0, 0)),
            pl.BlockSpec((len(POOL_WINDOWS), POOL_CH, POOL_CH), lambda s: (0, 0, 0)),
            pl.BlockSpec((1, D_MODEL), lambda s: (0, 0)),
        ],
        out_specs=[
            pl.BlockSpec((tp, D_MODEL), lambda s: (s, 0)),
            pl.BlockSpec((1, pad, D_MODEL), lambda s: (jnp.minimum(s // nblk, nb - 1), 0, 0)),
        ],
        out_shape=[jax.ShapeDtypeStruct((nt, D_MODEL), F32), jax.ShapeDtypeStruct((nb, pad, D_MODEL), F32)],
        scratch_shapes=[pltpu.VMEM((SUBLANES + pad + tp, D_MODEL), F32),
                        pltpu.VMEM((SUBLANES + pad + tp, POOL_CH), F32),
                        pltpu.VMEM((SUBLANES + pad + tp, POOL_CH), F32)],
        compiler_params=pltpu.CompilerParams(dimension_semantics=("arbitrary",), vmem_limit_bytes=VMEM_LIMIT),
        name="pool_prompt",
    )(x, x, g, w_bf, scale)


def _pool_sample_body(x_ref, st_ref, g_ref, w_ref, sc_ref, y_ref, so_ref, *, dec_seq):
    g = g_ref[...]
    xs = [x_ref[:, t, :] for t in range(dec_seq)]
    us = [_rms(x, g) for x in xs]

    def ext_row(e, c0):
        if e < POOL_BUF:
            return st_ref[:, e, c0:c0 + POOL_CH]
        return us[e - POOL_BUF][:, c0:c0 + POOL_CH]

    for t in range(dec_seq):
        inv_cnt = [1.0 / w for w in POOL_WINDOWS]
        y = _pool_windows(lambda j, c0: ext_row(POOL_BUF + t - j, c0), us[t], inv_cnt, w_ref, sc_ref[...])
        y_ref[:, t, :] = xs[t] + y
    for e in range(POOL_BUF):
        src = e + dec_seq
        so_ref[:, e, :] = st_ref[:, src, :] if src < POOL_BUF else us[src - POOL_BUF]


def _pool_sample(x3, states, layer, g, w_bf, scale):
    ns, dec_seq, _ = x3.shape
    ts = 32
    state_shape = states.shape[1:]
    return pl.pallas_call(
        functools.partial(_pool_sample_body, dec_seq=dec_seq),
        grid=(ns // ts,),
        in_specs=[
            pl.BlockSpec((ts, dec_seq, D_MODEL), lambda i: (i, 0, 0)),
            pl.BlockSpec((None, ts, POOL_BUF, D_MODEL), lambda i: (layer, i, 0, 0)),
            pl.BlockSpec((1, D_MODEL), lambda i: (0, 0)),
            pl.BlockSpec((len(POOL_WINDOWS), POOL_CH, POOL_CH), lambda i: (0, 0, 0)),
            pl.BlockSpec((1, D_MODEL), lambda i: (0, 0)),
        ],
        out_specs=[
            pl.BlockSpec((ts, dec_seq, D_MODEL), lambda i: (i, 0, 0)),
            pl.BlockSpec((ts, POOL_BUF, D_MODEL), lambda i: (i, 0, 0)),
        ],
        out_shape=[jax.ShapeDtypeStruct(x3.shape, F32), jax.ShapeDtypeStruct(state_shape, F32)],
        compiler_params=pltpu.CompilerParams(dimension_semantics=("arbitrary",), vmem_limit_bytes=VMEM_LIMIT),
        name="pool_sample",
    )(x3, states, g, w_bf, scale)


def _rope_tables(pos):
    half = ROT_DIM // 2
    inv = ROPE_THETA ** (-jnp.arange(0, ROT_DIM, 2, dtype=F32) / ROT_DIM)
    ang = pos.astype(F32)[:, None] * inv[None, :]
    cos, sin = jnp.cos(ang), jnp.sin(ang)
    n = pos.shape[0]
    ones = jnp.ones((n, DQK_A - ROT_DIM), F32)
    zeros = jnp.zeros((n, DQK_A - ROT_DIM), F32)
    zh = jnp.zeros((n, half), F32)
    c = jnp.concatenate([cos, cos, ones], axis=1)
    s_lo = jnp.concatenate([zh, sin, zeros], axis=1)
    s_hi = jnp.concatenate([-sin, zh, zeros], axis=1)
    return jnp.stack([jnp.tile(t, (1, LANES // DQK_A)) for t in (c, s_lo, s_hi)])


def kernel(x_prompt, x_sample, cache_k_a, cache_v_a, cache_k_b, cache_v_b, cache_logf_b, state_pool, page_table,
           meta_tokens, g_mix, w_in, b_forget, lambda_qk, g_subln, w_out, w_pool, pool_scale, g_ffn,
           w_route_group, b_route_group, w_route_expert, b_route_expert, w_gate, w_up, w_down, g_final):
    nb, seq, _ = x_prompt.shape
    ns, dec_seq, _ = x_sample.shape
    depth = g_mix.shape[0]
    seq_len = seq + N_META
    lp = -(-seq_len // TQ) * TQ
    npr = nb * lp
    n_samp = ns * dec_seq
    nt = npr + n_samp
    past = page_table.shape[1] * PAGE_SIZE
    n_phys = cache_k_a.shape[1]
    assert nt % TM_PROJ == 0 and lp % TP_POOL == 0 and dec_seq * HB <= LANES and seq_len % SUBLANES == 0
    assert page_table.shape[1] % PAGES_PER_STEP == 0 and PAGE_SIZE * HB == SUBLANES * LANES

    pieces = []
    for b in range(nb):
        pieces += [meta_tokens, x_prompt[b], jnp.zeros((lp - seq_len, D_MODEL), F32)]
    x = jnp.concatenate(pieces + [x_sample.reshape(n_samp, D_MODEL)], axis=0)
    valid = np.concatenate([np.tile(np.arange(lp) < seq_len, nb), np.ones(n_samp, bool)])
    pos = jnp.concatenate([jnp.tile(jnp.arange(lp, dtype=jnp.int32), nb),
                           past + jnp.tile(jnp.arange(dec_seq, dtype=jnp.int32), ns)])
    rope = _rope_tables(pos)

    outs = {k: [] for k in ("ka_p", "va_p", "kb_p", "vb_p", "lf_p", "pool_p",
                            "ka_s", "va_s", "kb_s", "vb_s", "lf_s", "pool_s")}
    for l in range(depth):
        if l % 2 == 0:
            i = l // 2
            lam_init = 0.8 - 0.6 * math.exp(-0.3 * l)
            lq = lambda_qk[i]
            lam = (jnp.exp(jnp.sum(lq[0] * lq[1])) - jnp.exp(jnp.sum(lq[2] * lq[3])) + lam_init).reshape(1)
            w_main = w_in[i, :, :D_QKV].astype(BF16)
            w_f = jnp.pad(w_in[i, :, D_QKV:], ((0, 0), (0, LANES - HB))).astype(BF16)
            b_f = jnp.pad(b_forget[i], (0, LANES - HB)).reshape(1, LANES)
            g_sub = g_subln[i].reshape(1, DH)
            p, lf = _inproj(x, g_mix[l].reshape(1, D_MODEL), w_main, w_f, b_f, rope)

            p_sample = p[npr:].reshape(ns, dec_seq, 6, HA, DH).transpose(2, 0, 1, 3, 4)
            for name, c in (("ka", 1), ("va", 2), ("kb", 4), ("vb", 5)):
                rows = [p[b * lp:b * lp + seq_len, c * HA * DH:(c + 1) * HA * DH] for b in range(nb)]
                outs[name + "_p"].append(jnp.stack(rows).reshape(nb, seq_len, HA, DH))
                outs[name + "_s"].append(p_sample[c])
            outs["lf_p"].append(lf[:npr].reshape(nb, lp, LANES)[:, :seq_len, :HB])
            outs["lf_s"].append(lf[npr:, :HB].reshape(ns, dec_seq, HB))

            fc, fct = _fcum(lf[:npr], nb, lp)
            fk = fct[:, :HB].reshape(nb, HB, 1, lp)
            oa_p = _flash_a(p, lam, g_sub, nb, lp, seq_len, lam_init)
            ob_p = _flash_b(p, fc, fk, nb, lp, seq_len)

            samp = [p_sample[c].reshape(ns, dec_seq * HA, DH) for c in range(6)]
            lfn = jnp.pad(lf[npr:, :HB].reshape(ns, 1, dec_seq * HB), ((0, 0), (0, 0), (0, LANES - dec_seq * HB)))
            oa_s, ob_s = _sattn(
                page_table, lam, samp[0], samp[3], samp[1], samp[2], samp[4], samp[5], lfn, g_sub,
                cache_k_a[i].reshape(n_phys, PAGE_SIZE * HA, DH), cache_v_a[i].reshape(n_phys, PAGE_SIZE * HA, DH),
                cache_k_b[i].reshape(n_phys, PAGE_SIZE * HB, DH), cache_v_b[i].reshape(n_phys, PAGE_SIZE * HB, DH),
                cache_logf_b[i].reshape(n_phys, SUBLANES, LANES), lam_init)
            wo = w_out[i].astype(BF16)
            x = _outproj(x, oa_p, ob_p, oa_s.reshape(n_samp, HA * DH), ob_s.reshape(n_samp, HB * DH),
                         wo[:HA * DH], wo[HA * DH:])
        else:
            j = l // 2
            g = g_mix[l].reshape(1, D_MODEL)
            wp = w_pool[j].astype(BF16)
            sc = pool_scale[j].reshape(1, D_MODEL)
            ys, st_s = _pool_sample(x[npr:].reshape(ns, dec_seq, D_MODEL), state_pool, j, g, wp, sc)
            y, st_p = _pool_prompt(x, g, wp, sc, nb, lp, seq_len)
            outs["pool_p"].append(st_p[:, 1:])
            outs["pool_s"].append(st_s)
            x = lax.dynamic_update_slice(y, ys.reshape(n_samp, D_MODEL), (npr, 0))

        wr = jnp.concatenate([w_route_group[l], w_route_expert[l].reshape(D_MODEL, N_EXPERTS),
                              jnp.zeros((D_MODEL, LANES - N_GROUPS - N_EXPERTS), F32)], axis=1)
        wr_hi = wr.astype(BF16)
        wr = jnp.concatenate([wr_hi, (wr - wr_hi.astype(F32)).astype(BF16)], axis=1)
        br =jnp.concatenate([b_route_group[l], b_route_expert[l].reshape(N_EXPERTS),
                              jnp.zeros((LANES - N_GROUPS - N_EXPERTS,), F32)]).reshape(1, LANES)
        out_rows = (nb, lp, N_META, seq, npr) if l == depth - 1 else None
        x = _moe(x, valid, g_ffn[l].reshape(1, D_MODEL), wr, br, w_gate, w_up, w_down, l,
                 g_final.reshape(1, D_MODEL), out_rows)
    y_prompt = x[0].reshape(nb, seq, D_MODEL)
    y_sample = x[1].reshape(ns, dec_seq, D_MODEL)
    st = lambda k: jnp.stack(outs[k])
    return (y_prompt, y_sample, st("ka_p"), st("va_p"), st("kb_p"), st("vb_p"), st("lf_p"), st("pool_p"),
            st("ka_s"), st("va_s"), st("kb_s"), st("vb_s"), st("lf_s"), st("pool_s"))
```
